```python
import math
import jax, jax.numpy as jnp
from jax import lax
import numpy as np

D_MODEL = 4096
BATCH = 32
SEQ = 256
DEPTH = 2
DEC_BATCH = 2
DEC_SEQ = 2048
PAST_LEN = 512

GRID_W = 64
Q_BLOCK = 128
ROPE_BASE = 10000.0
EPS = 1e-6
A_HEADS = 16
A_QK_HALF = 64
A_HEAD_DIM = 2 * A_QK_HALF
A_V_DIM = 128
B_HEADS = 16
Q_LORA = 1024
KV_LORA = 512
NOPE_DIM = 128
ROPE_DIM = 64
B_V_DIM = 128
POOL_WINDOWS = (2, 4, 8, 16)
POOL_GROUPS = 4
POOL_GROUP_DIM = D_MODEL // POOL_GROUPS
D_FF = 11008
CONV_W = 3
N_MOD = 6
N_ATTN_LAYERS = (DEPTH + 1) // 2
N_POOL_LAYERS = DEPTH // 2

A_Q = A_HEADS * A_HEAD_DIM
A_V = A_HEADS * A_V_DIM
B_OUT = B_HEADS * B_V_DIM
MIX_OUT = A_V + B_OUT
IN_COLS = 2 * A_Q + A_V + Q_LORA + KV_LORA + ROPE_DIM
IN_SPLITS = (A_Q, 2 * A_Q, 2 * A_Q + A_V, 2 * A_Q + A_V + Q_LORA, 2 * A_Q + A_V + Q_LORA + KV_LORA)

kernel_name = 'hybrid_diffusion_prefix_step'


def rmsnorm(x, g=None):
    xf = x.astype(jnp.float32)
    y = xf * lax.rsqrt(jnp.mean(xf * xf, axis=-1, keepdims=True) + EPS)
    if g is not None:
        y = y * g.astype(jnp.float32)
    return y.astype(x.dtype)


def rope_1d(x, pos):
    d = x.shape[-1]
    inv = ROPE_BASE ** (-jnp.arange(0, d, 2, dtype=jnp.float32) / d)
    ang = pos.astype(jnp.float32)[:, None] * inv[None, :]
    shape = (x.shape[1],) + (1,) * (x.ndim - 3) + (d // 2,)
    cos = jnp.cos(ang).reshape(shape)
    sin = jnp.sin(ang).reshape(shape)
    x1, x2 = jnp.split(x.astype(jnp.float32), 2, axis=-1)
    return jnp.concatenate([x1 * cos - x2 * sin, x1 * sin + x2 * cos], axis=-1).astype(x.dtype)


def axial_rope(x):
    n_tok = x.shape[1]
    grid_rows = n_tok // GRID_W
    rows = jnp.repeat(jnp.arange(grid_rows), GRID_W)
    cols = jnp.tile(jnp.arange(GRID_W), grid_rows)
    xr, xc = jnp.split(x, 2, axis=-1)
    return jnp.concatenate([rope_1d(xr, rows), rope_1d(xc, cols)], axis=-1)


def map_query_blocks(fn, *qs):
    B, S = qs[0].shape[:2]
    nb = S // Q_BLOCK
    blk = lambda a: jnp.moveaxis(a.reshape((B, nb, Q_BLOCK) + a.shape[2:]), 1, 0)
    out = lax.map(lambda args: fn(*args), tuple(blk(q) for q in qs))
    out = jnp.moveaxis(out, 0, 1)
    return out.reshape((B, S) + out.shape[3:])


def adaln(cond, w_mod, b_mod):
    m = jax.nn.silu(cond) @ w_mod + b_mod
    return [mi[:, None, :] for mi in jnp.split(m, N_MOD, axis=-1)]


def modulate(x, shift, scale):
    return x * (1 + scale) + shift


def diff_lambda(lq1, lk1, lq2, lk2, lam_init):
    f = jnp.float32
    return (jnp.exp(jnp.sum(lq1.astype(f) * lk1.astype(f)))
            - jnp.exp(jnp.sum(lq2.astype(f) * lk2.astype(f))) + lam_init)


def attn_project(h, w_in, g_cq, w_q_up, g_ckv):
    B, S, _ = h.shape
    qa, ka, va, cq, ckv, krope = jnp.split(h @ w_in, IN_SPLITS, axis=-1)
    qa = qa.reshape(B, S, A_HEADS, 2, A_QK_HALF)
    ka = ka.reshape(B, S, A_HEADS, 2, A_QK_HALF)
    va = va.reshape(B, S, A_HEADS, A_V_DIM)
    q = (rmsnorm(cq, g_cq) @ w_q_up).reshape(B, S, B_HEADS, NOPE_DIM + ROPE_DIM)
    q_nope, q_rope = jnp.split(q, [NOPE_DIM], axis=-1)
    return qa, ka, va, q_nope, q_rope, rmsnorm(ckv, g_ckv), krope


def mla_expand(ckv, w_kv_up):
    B, K = ckv.shape[:2]
    kv = (ckv @ w_kv_up).reshape(B, K, B_HEADS, NOPE_DIM + B_V_DIM)
    k_nope, v = jnp.split(kv, [NOPE_DIM], axis=-1)
    return k_nope, v


def diff_attention(q_segs, k_segs, v, lam, lam_init):
    scale = A_QK_HALF ** -0.5
    def block(*qb):
        s = jnp.concatenate([jnp.einsum('bqhmd,bkhmd->bhmqk', qi, ki) for qi, ki in zip(qb, k_segs)], axis=-1)
        p = jax.nn.softmax(s.astype(jnp.float32) * scale, axis=-1)
        a = p[:, :, 0] - lam * p[:, :, 1]
        return jnp.einsum('bhqk,bkhe->bqhe', a.astype(v.dtype), v)
    o = map_query_blocks(block, *q_segs)
    o = rmsnorm(o) * (1.0 - lam_init)
    return o.reshape(o.shape[0], o.shape[1], A_V)


def mla_attention(q_nope, qr_segs, k_nope, kr_segs, v):
    scale = (NOPE_DIM + ROPE_DIM) ** -0.5
    def block(qn, *qrb):
        s = jnp.einsum('bqhd,bkhd->bhqk', qn, k_nope).astype(jnp.float32)
        s_r = jnp.concatenate([jnp.einsum('bqhd,bkd->bhqk', qi, ki) for qi, ki in zip(qrb, kr_segs)], axis=-1)
        p = jax.nn.softmax((s + s_r.astype(jnp.float32)) * scale, axis=-1)
        return jnp.einsum('bhqk,bkhe->bqhe', p.astype(v.dtype), v)
    o = map_query_blocks(block, q_nope, *qr_segs)
    return o.reshape(o.shape[0], o.shape[1], B_OUT)


def attn_mixer_context(h, w_in, g_cq, w_q_up, g_ckv, w_kv_up, w_o, lam, lam_init):
    B, S, _ = h.shape
    qa, ka, va, qn, qr, ckv, kr = attn_project(h, w_in, g_cq, w_q_up, g_ckv)
    kn, vb = mla_expand(ckv, w_kv_up)
    oa = diff_attention([qa], [ka], va, lam, lam_init)
    ob = mla_attention(qn, [qr], kn, [kr], vb)
    out = jnp.concatenate([oa, ob], axis=-1) @ w_o
    return out, (ka.reshape(B, S, A_HEADS, A_HEAD_DIM), va, ckv, kr)


def attn_mixer_latent(h, ka_c, va_c, ckv_c, kr_c, w_in, g_cq, w_q_up, g_ckv, w_kv_up, w_o, lam, lam_init):
    B, L = ka_c.shape[:2]
    qa, ka, va, qn, qr, ckv, kr = attn_project(h, w_in, g_cq, w_q_up, g_ckv)
    ka_c = ka_c.reshape(B, L, A_HEADS, 2, A_QK_HALF)
    oa = diff_attention([qa, axial_rope(qa)], [ka_c, axial_rope(ka)],
                        jnp.concatenate([va_c, va], axis=1), lam, lam_init)
    kn_c, vb_c = mla_expand(ckv_c, w_kv_up)
    kn_l, vb_l = mla_expand(ckv, w_kv_up)
    kr_l = axial_rope(kr[:, :, None, :])[:, :, 0]
    ob = mla_attention(qn, [qr, axial_rope(qr)], jnp.concatenate([kn_c, kn_l], axis=1),
                       [kr_c, kr_l], jnp.concatenate([vb_c, vb_l], axis=1))
    return jnp.concatenate([oa, ob], axis=-1) @ w_o


def pool_mixer(h, w_pool, pool_scale):
    B, S, _ = h.shape
    g = h.reshape(B, S, POOL_GROUPS, POOL_GROUP_DIM)
    cs = jnp.pad(jnp.cumsum(g.astype(jnp.float32), axis=1), ((0, 0), (1, 0), (0, 0), (0, 0)))
    t = jnp.arange(S)
    outs = []
    for i, w in enumerate(POOL_WINDOWS):
        lo = jnp.clip(t - w // 2, 0, S)
        hi = jnp.clip(t + w - w // 2, 0, S)
        csi = cs[:, :, i]
        mean = (jnp.take(csi, hi, axis=1) - jnp.take(csi, lo, axis=1)) / (hi - lo).astype(jnp.float32)[None, :, None]
        outs.append(mean - g[:, :, i].astype(jnp.float32))
    pooled = jnp.stack(outs, axis=2).astype(h.dtype)
    y = jnp.einsum('bsgc,gcd->bsgd', pooled, w_pool).reshape(B, S, D_MODEL)
    return y * pool_scale


def conv_ffn(h, w_up, conv_w, conv_b, w_down):
    u = h @ w_up
    S = u.shape[1]
    pad = CONV_W // 2
    up = jnp.pad(u, ((0, 0), (pad, CONV_W - 1 - pad), (0, 0)))
    u = sum(up[:, j:j + S] * conv_w[j] for j in range(CONV_W)) + conv_b
    a, b = jnp.split(u, 2, axis=-1)
    return (jax.nn.silu(a) * b) @ w_down


def setup_inputs(seed: int = 0) -> dict:
    key = jax.random.key(seed)
    ks = iter(jax.random.split(key, 40))
    f32 = jnp.float32
    def nrm(shape, scale):
        return jax.random.normal(next(ks), shape, f32) * scale
    def gain(shape):
        return 1.0 + nrm(shape, 0.01)
    NA, NP = N_ATTN_LAYERS, N_POOL_LAYERS
    return {
        'x_prompt': nrm((BATCH, SEQ, D_MODEL), 1.0),
        'x_sample': nrm((DEC_BATCH, DEC_SEQ, D_MODEL), 1.0),
        'cache_diff_k': nrm((DEC_BATCH, NA, PAST_LEN, A_HEADS, A_HEAD_DIM), 1.0),
        'cache_diff_v': nrm((DEC_BATCH, NA, PAST_LEN, A_HEADS, A_V_DIM), 1.0),
        'cache_mla_ckv': nrm((DEC_BATCH, NA, PAST_LEN, KV_LORA), 1.0),
        'cache_mla_krope': nrm((DEC_BATCH, NA, PAST_LEN, ROPE_DIM), 1.0),
        'c': nrm((DEC_BATCH, D_MODEL), 1.0),
        'c_ctx': nrm((D_MODEL,), 1.0),
        'norm1_g': gain((DEPTH, D_MODEL)),
        'norm2_g': gain((DEPTH, D_MODEL)),
        'w_mod': nrm((DEPTH, D_MODEL, N_MOD * D_MODEL), 0.5 * D_MODEL ** -0.5),
        'b_mod': nrm((DEPTH, N_MOD * D_MODEL), 0.01),
        'w_in': nrm((NA, D_MODEL, IN_COLS), D_MODEL ** -0.5),
        'g_cq': gain((NA, Q_LORA)),
        'w_q_up': nrm((NA, Q_LORA, B_HEADS * (NOPE_DIM + ROPE_DIM)), Q_LORA ** -0.5),
        'g_ckv': gain((NA, KV_LORA)),
        'w_kv_up': nrm((NA, KV_LORA, B_HEADS * (NOPE_DIM + B_V_DIM)), KV_LORA ** -0.5),
        'lambda_q1': nrm((NA, A_QK_HALF), 0.1),
        'lambda_k1': nrm((NA, A_QK_HALF), 0.1),
        'lambda_q2': nrm((NA, A_QK_HALF), 0.1),
        'lambda_k2': nrm((NA, A_QK_HALF), 0.1),
        'w_o': nrm((NA, MIX_OUT, D_MODEL), MIX_OUT ** -0.5),
        'w_pool': nrm((NP, POOL_GROUPS, POOL_GROUP_DIM, POOL_GROUP_DIM), POOL_GROUP_DIM ** -0.5),
        'pool_scale': 1.0 + nrm((NP, D_MODEL), 0.1),
        'w_up': nrm((DEPTH, D_MODEL, 2 * D_FF), D_MODEL ** -0.5),
        'conv_w': nrm((DEPTH, CONV_W, 2 * D_FF), CONV_W ** -0.5),
        'conv_b': nrm((DEPTH, 2 * D_FF), 0.01),
        'w_down': nrm((DEPTH, D_FF, D_MODEL), D_FF ** -0.5),
        'g_final': gain((D_MODEL,)),
    }


def reference(x_prompt, x_sample, cache_diff_k, cache_diff_v, cache_mla_ckv, cache_mla_krope, c, c_ctx,
              norm1_g, norm2_g, w_mod, b_mod, w_in, g_cq, w_q_up, g_ckv, w_kv_up,
              lambda_q1, lambda_k1, lambda_q2, lambda_k2, w_o, w_pool, pool_scale,
              w_up, conv_w, conv_b, w_down, g_final):
    def layer(i, x, cond, cache):
        shift1, scale1, gate1, shift2, scale2, gate2 = adaln(cond, w_mod[i], b_mod[i])
        h = modulate(rmsnorm(x, norm1_g[i]), shift1, scale1)
        state = None
        if i % 2 == 0:
            j = i // 2
            lam_init = 0.8 - 0.6 * math.exp(-0.3 * i)
            lam = diff_lambda(lambda_q1[j], lambda_k1[j], lambda_q2[j], lambda_k2[j], lam_init)
            attn_w = (w_in[j], g_cq[j], w_q_up[j], g_ckv[j], w_kv_up[j], w_o[j])
            if cache is None:
                out, state = attn_mixer_context(h, *attn_w, lam, lam_init)
            else:
                out = attn_mixer_latent(h, *(a[:, j] for a in cache), *attn_w, lam, lam_init)
        else:
            out = pool_mixer(h, w_pool[i // 2], pool_scale[i // 2])
        x = x + gate1 * out
        h = modulate(rmsnorm(x, norm2_g[i]), shift2, scale2)
        x = x + gate2 * conv_ffn(h, w_up[i], conv_w[i], conv_b[i], w_down[i])
        return x, state

    xp = x_prompt
    states = []
    for i in range(DEPTH):
        xp, st = layer(i, xp, c_ctx[None, :], None)
        if st is not None:
            states.append(st)
    y_prompt = rmsnorm(xp, g_final)
    new_diff_k = jnp.stack([s[0] for s in states], axis=1)
    new_diff_v = jnp.stack([s[1] for s in states], axis=1)
    new_mla_ckv = jnp.stack([s[2] for s in states], axis=1)
    new_mla_krope = jnp.stack([s[3] for s in states], axis=1)

    caches = (cache_diff_k, cache_diff_v, cache_mla_ckv, cache_mla_krope)
    xs = x_sample
    for i in range(DEPTH):
        xs, _ = layer(i, xs, c, caches)
    y_sample = rmsnorm(xs, g_final)
    return (y_prompt, y_sample, new_diff_k, new_diff_v, new_mla_ckv, new_mla_krope)
```

```python
import functools
import math
from typing import NamedTuple

import jax
import jax.numpy as jnp
from jax import lax
from jax.experimental import pallas as pl
from jax.experimental.pallas import tpu as pltpu

F32 = jnp.float32
BF16 = jnp.bfloat16

GRID_W = 64
ROPE_BASE = 10000.0
EPS = 1e-6
A_HEADS = 16
A_QK_HALF = 64
A_HEAD_DIM = 2 * A_QK_HALF
A_V_DIM = 128
B_HEADS = 16
NOPE_DIM = 128
ROPE_DIM = 64
B_V_DIM = 128
POOL_WINDOWS = (2, 4, 8, 16)
N_MOD = 6
CONV_W = 3

V7X_VMEM_BYTES = 64 * 1024 * 1024
V7X_LANES = 128
V7X_SUBLANES_BF16 = 16

MLA_Q_PAD = 256
MLA_KR_PAD = MLA_Q_PAD - NOPE_DIM
ROPE_QUARTER = ROPE_DIM // 4
POOL_HALO = V7X_SUBLANES_BF16


class _Group(NamedTuple):
    n_seq: int
    seq: int
    rope: bool


def _tile(n, pref, mult=8):
    if n <= pref:
        return n
    t = (pref // mult) * mult
    while t >= mult:
        if n % t == 0:
            return t
        t -= mult
    return n


def _nbytes(shape, dtype):
    return math.prod(shape) * jnp.dtype(dtype).itemsize


def _call(body, *, name, grid, in_specs, out_specs, out_shape, args, temp_bytes=0, single_buffered=()):
    multi = isinstance(out_shape, (list, tuple))
    outs = list(out_shape) if multi else [out_shape]
    ospecs = list(out_specs) if multi else [out_specs]
    total = temp_bytes
    for k, (a, s) in enumerate(zip(args, in_specs)):
        blk = [1 if b is None else b for b in s.block_shape]
        total += _nbytes(blk, a.dtype) * (1 if k in single_buffered else 2)
    for o, s in zip(outs, ospecs):
        blk = [1 if b is None else b for b in s.block_shape]
        total += 2 * _nbytes(blk, o.dtype)
    limit = min(V7X_VMEM_BYTES - 4 * 1024 * 1024, max(total, 16 * 1024 * 1024))
    return pl.pallas_call(
        body, name=name, grid=grid, in_specs=in_specs, out_specs=out_specs, out_shape=out_shape,
        compiler_params=pltpu.CompilerParams(
            dimension_semantics=("parallel",) * len(grid), vmem_limit_bytes=int(limit)),
    )(*args)


def _dot(a, b):
    return jnp.dot(a, b, preferred_element_type=F32)


def _dot_nt(a, b):
    return lax.dot_general(a, b, (((1,), (1,)), ((), ())), preferred_element_type=F32)


def _silu(x):
    return x / (1.0 + jnp.exp(-x))


def _adaln(cond8, w_mod, b_mod):
    depth, d, n = w_mod.shape
    tn = next(t for t in (512, 256, 128) if n % (2 * t) == 0)

    def body(c_ref, wa_ref, wb_ref, b_ref, o_ref):
        s = _silu(c_ref[...]).astype(BF16)
        ya = _dot(s, wa_ref[...].astype(BF16))
        yb = _dot(s, wb_ref[...].astype(BF16))
        o_ref[...] = jnp.concatenate([ya, yb], axis=1) + b_ref[...]

    return _call(
        body, name="adaln", grid=(depth, n // (2 * tn)),
        in_specs=[pl.BlockSpec((8, d), lambda l, j: (0, 0)),
                  pl.BlockSpec((None, d, tn), lambda l, j: (l, 0, 2 * j)),
                  pl.BlockSpec((None, d, tn), lambda l, j: (l, 0, 2 * j + 1)),
                  pl.BlockSpec((None, 1, 2 * tn), lambda l, j: (l, 0, j))],
        out_specs=pl.BlockSpec((None, 8, 2 * tn), lambda l, j: (l, 0, j)),
        out_shape=jax.ShapeDtypeStruct((depth, 8, n), F32),
        args=(cond8, w_mod, w_mod, b_mod.reshape(depth, 1, n)),
        temp_bytes=2 * _nbytes((d, tn), BF16) + 2 * _nbytes((d, tn), F32))


def _norm(x, g, mods, k_shift, k_scale, out_dtype):
    m, d = x.shape
    modulated = mods is not None
    tm = _tile(m // mods.shape[0] if modulated else m, 512)

    def body(*refs):
        x_ref, g_ref = refs[0], refs[1]
        o_ref = refs[-1]
        xf = x_ref[...]
        y = xf * lax.rsqrt(jnp.mean(xf * xf, axis=-1, keepdims=True) + EPS) * g_ref[...]
        if modulated:
            mm = refs[2][...]
            y = y * (1.0 + mm[k_scale:k_scale + 1]) + mm[k_shift:k_shift + 1]
        o_ref[...] = y.astype(o_ref.dtype)

    in_specs = [pl.BlockSpec((tm, d), lambda i: (i, 0)), pl.BlockSpec((1, d), lambda i: (0, 0))]
    args = [x, g.reshape(1, d)]
    if modulated:
        rows_per_cond = m // mods.shape[0]
        in_specs.append(pl.BlockSpec((None, N_MOD, d), lambda i: ((i * tm) // rows_per_cond, 0, 0)))
        args.append(mods)
    return _call(body, name="norm", grid=(m // tm,), in_specs=in_specs,
                 out_specs=pl.BlockSpec((tm, d), lambda i: (i, 0)),
                 out_shape=jax.ShapeDtypeStruct((m, d), out_dtype), args=args,
                 temp_bytes=3 * _nbytes((tm, d), F32))


def _proj(name, lhs, rhs, epilogue, outs, *, tm, tn, extras=()):
    m = lhs[0].shape[0]
    n = rhs[0].shape[1]
    np_ = len(lhs)
    ne = len(extras)

    def body(*refs):
        acc = None
        for p in range(np_):
            part = _dot(refs[p][...].astype(BF16), refs[np_ + p][...].astype(BF16))
            acc = part if acc is None else acc + part
        epilogue(acc, refs[2 * np_:2 * np_ + ne], refs[2 * np_ + ne:])

    in_specs = ([pl.BlockSpec((tm, a.shape[1]), lambda i, j: (i, 0)) for a in lhs]
                + [pl.BlockSpec((b.shape[0], tn), lambda i, j: (0, j)) for b in rhs]
                + [pl.BlockSpec(bs, im) for (_, bs, im) in extras])
    args = list(lhs) + list(rhs) + [e[0] for e in extras]
    out_shape = [jax.ShapeDtypeStruct((m, n), dt) for dt in outs]
    out_specs = [pl.BlockSpec((tm, tn), lambda i, j: (i, j)) for _ in outs]
    return _call(body, name=name, grid=(m // tm, n // tn), in_specs=in_specs, out_specs=out_specs,
                 out_shape=out_shape, args=args, temp_bytes=6 * _nbytes((tm, tn), F32))


def _ep_scale(scale):
    def ep(acc, ex, outs):
        outs[0][...] = (acc * scale).astype(outs[0].dtype)
    return ep


def _rotate(a, cos, sin):
    width = a.shape[1]
    lane = lax.broadcasted_iota(jnp.int32, a.shape, 1)
    low = (lane & (2 * ROPE_QUARTER - 1)) < ROPE_QUARTER
    partner = jnp.where(low, pltpu.roll(a, width - ROPE_QUARTER, 1), pltpu.roll(a, ROPE_QUARTER, 1))
    return a * cos + partner * sin


def _ep_rope(scale, with_plain):
    def ep(acc, ex, outs):
        a = acc * scale
        r = _rotate(a, ex[0][...], ex[1][...])
        if with_plain:
            outs[0][...] = a.astype(outs[0].dtype)
        outs[-1][...] = r.astype(outs[-1].dtype)
    return ep


def _ep_rmsnorm(acc, ex, outs):
    y = acc * lax.rsqrt(jnp.mean(acc * acc, axis=-1, keepdims=True) + EPS) * ex[0][...]
    outs[0][...] = y.astype(outs[0].dtype)


def _ep_residual(k_gate):
    def ep(acc, ex, outs):
        gate = ex[1][...][k_gate:k_gate + 1]
        outs[0][...] = ex[0][...] + gate * acc
    return ep


def _rope_tables(seq):
    half = ROPE_DIM // 2
    inv = ROPE_BASE ** (-jnp.arange(0, half, 2, dtype=F32) / half)
    pos = jnp.arange(seq)
    ar = (pos // GRID_W).astype(F32)[:, None] * inv[None, :]
    ac = (pos % GRID_W).astype(F32)[:, None] * inv[None, :]
    cos = jnp.concatenate([jnp.cos(ar), jnp.cos(ar), jnp.cos(ac), jnp.cos(ac)], axis=1)
    sin = jnp.concatenate([-jnp.sin(ar), jnp.sin(ar), -jnp.sin(ac), jnp.sin(ac)], axis=1)
    return cos, sin


def _widen_tables(cos, sin, lead, trail, width):
    seq = cos.shape[0]
    c = jnp.concatenate([jnp.ones((seq, lead), F32), cos, jnp.ones((seq, trail), F32)], axis=1)
    s = jnp.concatenate([jnp.zeros((seq, lead), F32), sin, jnp.zeros((seq, trail), F32)], axis=1)
    reps = width // c.shape[1]
    return jnp.tile(c, (1, reps)), jnp.tile(s, (1, reps))


def _diff_attention(lam_vecs, lam_init, qs, ks, vs, grp, hps):
    m = qs[0].shape[0]
    nseg = len(qs)
    tq = _tile(grp.seq, 256)
    qt = grp.seq // tq
    width = hps * A_HEAD_DIM
    n_hg = A_HEADS // hps
    k_lens = [k.shape[0] // grp.n_seq for k in ks]

    def body(*refs):
        lam_ref = refs[0]
        q_refs = refs[1:1 + nseg]
        k_refs = refs[1 + nseg:1 + 2 * nseg]
        v_refs = refs[1 + 2 * nseg:1 + 3 * nseg]
        o_ref = refs[-1]
        lv = lam_ref[...]
        lam = (jnp.exp(jnp.sum(lv[0:1] * lv[1:2], axis=-1, keepdims=True))
               - jnp.exp(jnp.sum(lv[2:3] * lv[3:4], axis=-1, keepdims=True)) + lam_init)
        first = lax.broadcasted_iota(jnp.int32, (tq, A_HEAD_DIM), 1) < A_QK_HALF
        for g in range(hps):
            cols = slice(g * A_HEAD_DIM, (g + 1) * A_HEAD_DIM)
            scores = []
            for q_ref, k_ref in zip(q_refs, k_refs):
                q = q_ref[:, cols]
                zero = jnp.zeros_like(q)
                q2 = jnp.concatenate([jnp.where(first, q, zero), jnp.where(first, zero, q)], axis=0)
                scores.append(_dot_nt(q2, k_ref[:, cols].astype(BF16)))
            mx = functools.reduce(jnp.maximum, [jnp.max(s, axis=-1, keepdims=True) for s in scores])
            ps = [jnp.exp(s - mx) for s in scores]
            den = functools.reduce(jnp.add, [jnp.sum(p, axis=-1, keepdims=True) for p in ps])
            o2 = functools.reduce(
                jnp.add, [_dot(p.astype(BF16), v_ref[:, cols].astype(BF16)) for p, v_ref in zip(ps, v_refs)])
            o2 = o2 / den
            o = o2[:tq] - lam * o2[tq:]
            o = o * lax.rsqrt(jnp.mean(o * o, axis=-1, keepdims=True) + EPS) * (1.0 - lam_init)
            o_ref[:, cols] = o.astype(o_ref.dtype)

    q_spec = pl.BlockSpec((tq, width), lambda b, h, i: (b * qt + i, h))
    in_specs = ([pl.BlockSpec(lam_vecs.shape, lambda b, h, i: (0, 0))]
                + [q_spec] * nseg
                + [pl.BlockSpec((kl, width), lambda b, h, i: (b, h)) for kl in k_lens] * 2)
    nk = sum(k_lens)
    return _call(body, name="diff_attention", grid=(grp.n_seq, n_hg, qt), in_specs=in_specs,
                 out_specs=q_spec, out_shape=jax.ShapeDtypeStruct((m, A_HEADS * A_V_DIM), BF16),
                 args=[lam_vecs] + list(qs) + list(ks) + list(vs),
                 temp_bytes=4 * _nbytes((2 * tq, nk), F32))


def _mla_attention(qs, kns, krs, vs, grp, hps):
    m = qs[0].shape[0]
    nseg = len(qs)
    tq = _tile(grp.seq, 256)
    qt = grp.seq // tq
    n_hg = B_HEADS // hps
    k_lens = [k.shape[0] // grp.n_seq for k in kns]

    def body(*refs):
        q_refs = refs[:nseg]
        kn_refs = refs[nseg:2 * nseg]
        kr_refs = refs[2 * nseg:3 * nseg]
        v_refs = refs[3 * nseg:4 * nseg]
        o_ref = refs[-1]
        krs_v = [r[...].astype(BF16) for r in kr_refs]
        for g in range(hps):
            kcols = slice(g * NOPE_DIM, (g + 1) * NOPE_DIM)
            vcols = slice(g * B_V_DIM, (g + 1) * B_V_DIM)
            scores = []
            for q_ref, kn_ref, kr in zip(q_refs, kn_refs, krs_v):
                q = q_ref[:, g * MLA_Q_PAD:(g + 1) * MLA_Q_PAD]
                k = jnp.concatenate([kn_ref[:, kcols], kr], axis=1)
                scores.append(_dot_nt(q, k))
            mx = functools.reduce(jnp.maximum, [jnp.max(s, axis=-1, keepdims=True) for s in scores])
            ps = [jnp.exp(s - mx) for s in scores]
            den = functools.reduce(jnp.add, [jnp.sum(p, axis=-1, keepdims=True) for p in ps])
            o = functools.reduce(
                jnp.add, [_dot(p.astype(BF16), v_ref[:, vcols]) for p, v_ref in zip(ps, v_refs)])
            o_ref[:, vcols] = (o / den).astype(o_ref.dtype)

    q_spec = pl.BlockSpec((tq, hps * MLA_Q_PAD), lambda b, h, i: (b * qt + i, h))
    kv_specs = [pl.BlockSpec((kl, hps * NOPE_DIM), lambda b, h, i: (b, h)) for kl in k_lens]
    kr_specs = [pl.BlockSpec((kl, MLA_KR_PAD), lambda b, h, i: (b, 0)) for kl in k_lens]
    nk = sum(k_lens)
    return _call(body, name="mla_attention", grid=(grp.n_seq, n_hg, qt),
                 in_specs=[q_spec] * nseg + kv_specs + kr_specs + kv_specs,
                 out_specs=pl.BlockSpec((tq, hps * B_V_DIM), lambda b, h, i: (b * qt + i, h)),
                 out_shape=jax.ShapeDtypeStruct((m, B_HEADS * B_V_DIM), BF16),
                 args=list(qs) + list(kns) + list(krs) + list(vs),
                 temp_bytes=4 * _nbytes((tq, nk), F32) + 2 * _nbytes((nk, MLA_Q_PAD), BF16))


def _pool_mixer(h, x, mods, w_pool, pool_scale, grp, k_gate):
    m, d = x.shape
    n_groups, c, _ = w_pool.shape
    assert max(POOL_WINDOWS) // 2 <= POOL_HALO and n_groups == len(POOL_WINDOWS)
    r = _tile(grp.seq, 512, POOL_HALO)
    nt = m // r
    halo_per_tile = r // POOL_HALO
    rows_per_cond = m // mods.shape[0]
    seq = grp.seq
    kc = r + 2 * POOL_HALO

    def body(hp_ref, hc_ref, hn_ref, w_ref, ps_ref, x_ref, m_ref, o_ref):
        g = pl.program_id(0)
        i = pl.program_id(1)
        back = functools.reduce(jnp.add, [jnp.where(g == k, w // 2, 0) for k, w in enumerate(POOL_WINDOWS)])
        fwd = functools.reduce(jnp.add, [jnp.where(g == k, w - w // 2, 0) for k, w in enumerate(POOL_WINDOWS)])
        seq_start = ((i * r) // seq) * seq
        row = i * r + lax.broadcasted_iota(jnp.int32, (r, kc), 0)
        col = i * r - POOL_HALO + lax.broadcasted_iota(jnp.int32, (r, kc), 1)
        lo = jnp.maximum(row - back, seq_start)
        hi = jnp.minimum(row + fwd, seq_start + seq)
        member = jnp.where((col >= lo) & (col < hi), 1.0, 0.0).astype(BF16)
        hc = hc_ref[...]
        hcat = jnp.concatenate([hp_ref[...], hc, hn_ref[...]], axis=0)
        win = _dot(member, hcat)
        row_c = i * r + lax.broadcasted_iota(jnp.int32, (r, c), 0)
        cnt = (jnp.minimum(row_c + fwd, seq_start + seq) - jnp.maximum(row_c - back, seq_start)).astype(F32)
        pooled = win / cnt - hc.astype(F32)
        y = _dot(pooled.astype(BF16), w_ref[...]) * ps_ref[...]
        gate = m_ref[...][k_gate:k_gate + 1]
        o_ref[...] = x_ref[...] + gate * y

    last_halo = m // POOL_HALO - 1
    in_specs = [
        pl.BlockSpec((POOL_HALO, c), lambda g, i: (jnp.maximum(i * halo_per_tile - 1, 0), g)),
        pl.BlockSpec((r, c), lambda g, i: (i, g)),
        pl.BlockSpec((POOL_HALO, c), lambda g, i: (jnp.minimum((i + 1) * halo_per_tile, last_halo), g)),
        pl.BlockSpec((None, c, c), lambda g, i: (g, 0, 0)),
        pl.BlockSpec((1, c), lambda g, i: (0, g)),
        pl.BlockSpec((r, c), lambda g, i: (i, g)),
        pl.BlockSpec((None, N_MOD, c), lambda g, i: ((i * r) // rows_per_cond, 0, g)),
    ]
    return _call(body, name="pool_mixer", grid=(n_groups, nt), in_specs=in_specs,
                 out_specs=pl.BlockSpec((r, c), lambda g, i: (i, g)),
                 out_shape=jax.ShapeDtypeStruct((m, d), F32),
                 args=[h, h, h, w_pool, pool_scale.reshape(1, d), x, mods],
                 temp_bytes=4 * _nbytes((r, kc), F32) + 4 * _nbytes((r, c), F32))


def _ffn_up(h, w_up, conv_w, conv_b, grp):
    m, d = h.shape
    f = w_up.shape[1] // 2
    tm = _tile(m, max(grp.seq, 2048), grp.seq)
    tn = _tile(f, 256, V7X_LANES)
    nj = f // tn
    seq = grp.seq

    def body(h_ref, wa_ref, wb_ref, cwa_ref, cwb_ref, cba_ref, cbb_ref, o_ref):
        xh = h_ref[...]
        pos = lax.broadcasted_iota(jnp.int32, (tm, tn), 0) % seq
        is_first = pos == 0
        is_last = pos == seq - 1

        def conv(u, cw_ref, cb_ref):
            cw = cw_ref[...]
            prev = jnp.where(is_first, 0.0, pltpu.roll(u, 1, 0))
            nxt = jnp.where(is_last, 0.0, pltpu.roll(u, tm - 1, 0))
            return prev * cw[0:1] + u * cw[1:2] + nxt * cw[2:3] + cb_ref[...]

        a = conv(_dot(xh, wa_ref[...].astype(BF16)), cwa_ref, cba_ref)
        b = conv(_dot(xh, wb_ref[...].astype(BF16)), cwb_ref, cbb_ref)
        o_ref[...] = (_silu(a) * b).astype(o_ref.dtype)

    in_specs = [
        pl.BlockSpec((tm, d), lambda i, j: (i, 0), pipeline_mode=pl.Buffered(1)),
        pl.BlockSpec((d, tn), lambda i, j: (0, j)),
        pl.BlockSpec((d, tn), lambda i, j: (0, nj + j)),
        pl.BlockSpec((CONV_W, tn), lambda i, j: (0, j)),
        pl.BlockSpec((CONV_W, tn), lambda i, j: (0, nj + j)),
        pl.BlockSpec((1, tn), lambda i, j: (0, j)),
        pl.BlockSpec((1, tn), lambda i, j: (0, nj + j)),
    ]
    cb = conv_b.reshape(1, 2 * f)
    return _call(body, name="ffn_up", grid=(m // tm, nj), in_specs=in_specs,
                 out_specs=pl.BlockSpec((tm, tn), lambda i, j: (i, j)),
                 out_shape=jax.ShapeDtypeStruct((m, f), BF16),
                 args=[h, w_up, w_up, conv_w, conv_w, cb, cb],
                 temp_bytes=10 * _nbytes((tm, tn), F32), single_buffered=(0,))


def _residual_proj(name, lhs, rhs, x, mods, k_gate, *, tm, tn):
    m = x.shape[0]
    rows_per_cond = m // mods.shape[0]
    assert rows_per_cond % tm == 0
    extras = [(x, (tm, tn), lambda i, j: (i, j)),
              (mods, (None, N_MOD, tn), lambda i, j: ((i * tm) // rows_per_cond, 0, j))]
    return _proj(name, lhs, rhs, _ep_residual(k_gate), (F32,), tm=tm, tn=tn, extras=extras)[0]


class _AttnWeights(NamedTuple):
    qa: jax.Array
    ka: jax.Array
    va: jax.Array
    cq: jax.Array
    ckv: jax.Array
    kr: jax.Array
    g_cq: jax.Array
    g_ckv: jax.Array
    q_up: jax.Array
    kn: jax.Array
    vb: jax.Array
    o_a: jax.Array
    o_b: jax.Array
    lam_vecs: jax.Array


def _prep_attn_weights(w_in, g_cq, w_q_up, g_ckv, w_kv_up, w_o, lam_vecs):
    a_q = A_HEADS * A_HEAD_DIM
    a_v = A_HEADS * A_V_DIM
    q_lora = g_cq.shape[0]
    kv_lora = g_ckv.shape[0]
    o0, o1, o2 = a_q, 2 * a_q, 2 * a_q + a_v
    o3, o4 = o2 + q_lora, o2 + q_lora + kv_lora
    kr = jnp.pad(w_in[:, o4:], ((0, 0), (0, MLA_KR_PAD - ROPE_DIM)))
    q_up = w_q_up.reshape(q_lora, B_HEADS, NOPE_DIM + ROPE_DIM)
    q_up = jnp.pad(q_up, ((0, 0), (0, 0), (0, MLA_Q_PAD - NOPE_DIM - ROPE_DIM)))
    kv = w_kv_up.reshape(kv_lora, B_HEADS, NOPE_DIM + B_V_DIM)
    c = lambda w: w.astype(BF16)
    return _AttnWeights(
        qa=c(w_in[:, :o0]), ka=c(w_in[:, o0:o1]), va=c(w_in[:, o1:o2]), cq=c(w_in[:, o2:o3]),
        ckv=c(w_in[:, o3:o4]), kr=c(kr), g_cq=g_cq.reshape(1, q_lora), g_ckv=g_ckv.reshape(1, kv_lora),
        q_up=c(q_up.reshape(q_lora, B_HEADS * MLA_Q_PAD)),
        kn=c(kv[:, :, :NOPE_DIM].reshape(kv_lora, B_HEADS * NOPE_DIM)),
        vb=c(kv[:, :, NOPE_DIM:].reshape(kv_lora, B_HEADS * B_V_DIM)),
        o_a=c(w_o[:a_v]), o_b=c(w_o[a_v:]), lam_vecs=lam_vecs)


def _attn_mixer(h, x, mods, w, grp, lam_init, cache):
    m, d = h.shape
    tm = _tile(grp.seq if grp.rope else m, 1024, V7X_SUBLANES_BF16)
    tn = 512
    hps = 4
    qk_scale = A_QK_HALF ** -0.5
    mla_scale = (NOPE_DIM + ROPE_DIM) ** -0.5
    q_lora = w.cq.shape[1]
    kv_lora = w.ckv.shape[1]

    def plain(name, lhs, rhs, dtype, scale=1.0, tn_=tn):
        return _proj(name, [lhs], [rhs], _ep_scale(scale), (dtype,), tm=tm, tn=min(tn_, rhs.shape[1]))[0]

    def roped(name, lhs, rhs, tables, dtype, scale, with_plain, tn_):
        tiles_per_seq = grp.seq // tm
        extras = [(t, (tm, tn_), lambda i, j: (i % tiles_per_seq, 0)) for t in tables]
        outs = (dtype, dtype) if with_plain else (dtype,)
        return _proj(name, [lhs], [rhs], _ep_rope(scale, with_plain), outs, tm=tm, tn=tn_, extras=extras)

    cqn = _proj("cq_norm", [h], [w.cq], _ep_rmsnorm, (BF16,), tm=tm, tn=q_lora,
                extras=[(w.g_cq, (1, q_lora), lambda i, j: (0, 0))])[0]
    ckvn = _proj("ckv_norm", [h], [w.ckv], _ep_rmsnorm, (F32,), tm=tm, tn=kv_lora,
                 extras=[(w.g_ckv, (1, kv_lora), lambda i, j: (0, 0))])[0]
    kn = plain("k_nope", ckvn, w.kn, BF16)
    vb = plain("v_mla", ckvn, w.vb, BF16)

    if not grp.rope:
        qa = plain("q_diff", h, w.qa, BF16, qk_scale)
        ka = plain("k_diff", h, w.ka, F32)
        va = plain("v_diff", h, w.va, F32)
        kr = plain("k_rope", h, w.kr, F32)
        q = plain("q_mla", cqn, w.q_up, BF16, mla_scale)
        oa = _diff_attention(w.lam_vecs, lam_init, [qa], [ka], [va], grp, hps)
        ob = _mla_attention([q], [kn], [kr], [vb], grp, hps)
        state = (ka, va, ckvn, kr[:, :ROPE_DIM])
    else:
        cos, sin = _rope_tables(grp.seq)
        t_diff = _widen_tables(cos, sin, 0, 0, tn)
        t_mla = _widen_tables(cos, sin, NOPE_DIM, MLA_Q_PAD - NOPE_DIM - ROPE_DIM, tn)
        t_kr = _widen_tables(cos, sin, 0, MLA_KR_PAD - ROPE_DIM, MLA_KR_PAD)
        ka_c, va_c, ckv_c, kr_c = cache
        qa_u, qa_r = roped("q_diff", h, w.qa, t_diff, BF16, qk_scale, True, tn)
        ka_r, = roped("k_diff", h, w.ka, t_diff, BF16, 1.0, False, tn)
        va = plain("v_diff", h, w.va, BF16)
        kr_r, = roped("k_rope", h, w.kr, t_kr, BF16, 1.0, False, MLA_KR_PAD)
        q_u, q_r = roped("q_mla", cqn, w.q_up, t_mla, BF16, mla_scale, True, tn)
        tc = _tile(ckv_c.shape[0], 1024, V7X_SUBLANES_BF16)
        kn_c = _proj("k_nope_ctx", [ckv_c], [w.kn], _ep_scale(1.0), (BF16,), tm=tc, tn=tn)[0]
        vb_c = _proj("v_mla_ctx", [ckv_c], [w.vb], _ep_scale(1.0), (BF16,), tm=tc, tn=tn)[0]
        oa = _diff_attention(w.lam_vecs, lam_init, [qa_u, qa_r], [ka_c, ka_r], [va_c, va], grp, hps)
        ob = _mla_attention([q_u, q_r], [kn_c, kn], [kr_c, kr_r], [vb_c, vb], grp, hps)
        state = None
    x1 = _residual_proj("o_proj", [oa, ob], [w.o_a, w.o_b], x, mods, 2, tm=tm, tn=tn)
    return x1, state


def _conv_ffn(x, mods, g2, w_up, conv_w, conv_b, w_down, grp):
    h = _norm(x, g2, mods, 3, 4, BF16)
    act = _ffn_up(h, w_up, conv_w, conv_b, grp)
    return _residual_proj("ffn_down", [act], [w_down], x, mods, 5,
                          tm=_tile(x.shape[0] // mods.shape[0], 512, V7X_SUBLANES_BF16), tn=512)


def kernel(x_prompt, x_sample, cache_diff_k, cache_diff_v, cache_mla_ckv, cache_mla_krope, c, c_ctx,
           norm1_g, norm2_g, w_mod, b_mod, w_in, g_cq, w_q_up, g_ckv, w_kv_up,
           lambda_q1, lambda_k1, lambda_q2, lambda_k2, w_o, w_pool, pool_scale,
           w_up, conv_w, conv_b, w_down, g_final):
    batch, seq, d = x_prompt.shape
    dec_batch, dec_seq, _ = x_sample.shape
    depth = w_mod.shape[0]
    past = cache_diff_k.shape[2]
    groups = (_Group(batch, seq, False), _Group(dec_batch, dec_seq, True))

    n_cond = 1 + dec_batch
    cond8 = jnp.zeros((8, d), F32).at[0].set(c_ctx).at[1:n_cond].set(c)
    mods_all = _adaln(cond8, w_mod, b_mod).reshape(depth, 8, N_MOD, d)
    mods_g = (mods_all[:, 0:1], mods_all[:, 1:n_cond])

    attn_w = {}
    for i in range(0, depth, 2):
        j = i // 2
        lam_vecs = jnp.stack([lambda_q1[j], lambda_k1[j], lambda_q2[j], lambda_k2[j]])
        attn_w[i] = _prep_attn_weights(w_in[j], g_cq[j], w_q_up[j], g_ckv[j], w_kv_up[j], w_o[j], lam_vecs)
    w_pool_b = w_pool.astype(BF16)
    w_up_b = w_up.astype(BF16)
    w_down_b = w_down.astype(BF16)

    xs = [x_prompt.reshape(batch * seq, d), x_sample.reshape(dec_batch * dec_seq, d)]
    states = []
    for gi, grp in enumerate(groups):
        x = xs[gi]
        for i in range(depth):
            mods = mods_g[gi][i]
            h = _norm(x, norm1_g[i], mods, 0, 1, BF16)
            if i % 2 == 0:
                j = i // 2
                lam_init = 0.8 - 0.6 * math.exp(-0.3 * i)
                cache = None
                if grp.rope:
                    kr_c = jnp.pad(cache_mla_krope[:, j].reshape(dec_batch * past, ROPE_DIM),
                                   ((0, 0), (0, MLA_KR_PAD - ROPE_DIM)))
                    cache = (cache_diff_k[:, j].reshape(dec_batch * past, A_HEADS * A_HEAD_DIM),
                             cache_diff_v[:, j].reshape(dec_batch * past, A_HEADS * A_V_DIM),
                             cache_mla_ckv[:, j].reshape(dec_batch * past, -1), kr_c)
                x, st = _attn_mixer(h, x, mods, attn_w[i], grp, lam_init, cache)
                if st is not None:
                    states.append(st)
            else:
                x = _pool_mixer(h, x, mods, w_pool_b[i // 2], pool_scale[i // 2], grp, 2)
            x = _conv_ffn(x, mods, norm2_g[i], w_up_b[i], conv_w[i], conv_b[i], w_down_b[i], grp)
        xs[gi] = _norm(x, g_final, None, 0, 0, F32)

    y_prompt = xs[0].reshape(batch, seq, d)
    y_sample = xs[1].reshape(dec_batch, dec_seq, d)
    stack = lambda k, tail: jnp.stack([s[k].reshape((batch, seq) + tail) for s in states], axis=1)
    new_diff_k = stack(0, (A_HEADS, A_HEAD_DIM))
    new_diff_v = stack(1, (A_HEADS, A_V_DIM))
    new_mla_ckv = stack(2, (g_ckv.shape[-1],))
    new_mla_krope = stack(3, (ROPE_DIM,))
    return (y_prompt, y_sample, new_diff_k, new_diff_v, new_mla_ckv, new_mla_krope)
```

```python
import functools
import math
from typing import NamedTuple

import jax
import jax.numpy as jnp
from jax import lax
from jax.experimental import pallas as pl
from jax.experimental.pallas import tpu as pltpu

F32 = jnp.float32
BF16 = jnp.bfloat16

GRID_W = 64
ROPE_BASE = 10000.0
EPS = 1e-6
A_HEADS = 16
A_QK_HALF = 64
A_HEAD_DIM = 2 * A_QK_HALF
A_V_DIM = 128
B_HEADS = 16
NOPE_DIM = 128
ROPE_DIM = 64
B_V_DIM = 128
POOL_WINDOWS = (2, 4, 8, 16)
N_MOD = 6
CONV_W = 3

V7X_VMEM_BYTES = 64 * 1024 * 1024
V7X_LANES = 128
V7X_SUBLANES_BF16 = 16
COMPILER_SCRATCH_BYTES = 2 * 1024 * 1024

MLA_Q_PAD = 256
MLA_KR_PAD = MLA_Q_PAD - NOPE_DIM
ROPE_QUARTER = ROPE_DIM // 4
POOL_HALO = V7X_SUBLANES_BF16


class _Group(NamedTuple):
    n_seq: int
    seq: int
    rope: bool


def _tile(n, pref, mult=8):
    if n <= pref:
        return n
    t = (pref // mult) * mult
    while t >= mult:
        if n % t == 0:
            return t
        t -= mult
    return n


def _nbytes(shape, dtype):
    return math.prod(shape) * jnp.dtype(dtype).itemsize


def _call(body, *, name, grid, in_specs, out_specs, out_shape, args, temp_bytes=0, single_buffered=(),
          scratch=(), semantics=None, flags=None):
    multi = isinstance(out_shape, (list, tuple))
    outs = list(out_shape) if multi else [out_shape]
    ospecs = list(out_specs) if multi else [out_specs]
    total = temp_bytes + COMPILER_SCRATCH_BYTES + sum(_nbytes(s, dt) for s, dt in scratch)
    for k, (a, s) in enumerate(zip(args, in_specs)):
        blk = [1 if b is None else b for b in s.block_shape]
        total += _nbytes(blk, a.dtype) * (1 if k in single_buffered else 2)
    for o, s in zip(outs, ospecs):
        blk = [1 if b is None else b for b in s.block_shape]
        total += 2 * _nbytes(blk, o.dtype)
    limit = min(V7X_VMEM_BYTES - 4 * 1024 * 1024, max(total, 16 * 1024 * 1024))
    return pl.pallas_call(
        body, name=name, grid=grid, in_specs=in_specs, out_specs=out_specs, out_shape=out_shape,
        scratch_shapes=[pltpu.VMEM(s, dt) for s, dt in scratch],
        compiler_params=pltpu.CompilerParams(
            dimension_semantics=semantics or ("parallel",) * len(grid), vmem_limit_bytes=int(limit),
            flags=flags),
    )(*args)


def _dot(a, b):
    return jnp.dot(a, b, preferred_element_type=F32)


def _dot_nt(a, b):
    return lax.dot_general(a, b, (((1,), (1,)), ((), ())), preferred_element_type=F32)


def _silu(x):
    return x / (1.0 + jnp.exp(-x))


def _adaln(cond8, w_mod, b_mod):
    depth, d, n = w_mod.shape
    tn = next(t for t in (512, 256, 128) if n % (2 * t) == 0)

    def body(c_ref, wa_ref, wb_ref, b_ref, o_ref):
        s = _silu(c_ref[...]).astype(BF16)
        ya = _dot(s, wa_ref[...].astype(BF16))
        yb = _dot(s, wb_ref[...].astype(BF16))
        o_ref[...] = jnp.concatenate([ya, yb], axis=1) + b_ref[...]

    return _call(
        body, name="adaln", grid=(depth, n // (2 * tn)),
        in_specs=[pl.BlockSpec((8, d), lambda l, j: (0, 0)),
                  pl.BlockSpec((None, d, tn), lambda l, j: (l, 0, 2 * j)),
                  pl.BlockSpec((None, d, tn), lambda l, j: (l, 0, 2 * j + 1)),
                  pl.BlockSpec((None, 1, 2 * tn), lambda l, j: (l, 0, j))],
        out_specs=pl.BlockSpec((None, 8, 2 * tn), lambda l, j: (l, 0, j)),
        out_shape=jax.ShapeDtypeStruct((depth, 8, n), F32),
        args=(cond8, w_mod, w_mod, b_mod.reshape(depth, 1, n)),
        temp_bytes=2 * _nbytes((d, tn), BF16) + 2 * _nbytes((d, tn), F32))


def _norm(x, g, mods, k_shift, k_scale, out_dtype):
    m, d = x.shape
    modulated = mods is not None
    tm = _tile(m // mods.shape[0] if modulated else m, 512)

    def body(*refs):
        x_ref, g_ref = refs[0], refs[1]
        o_ref = refs[-1]
        xf = x_ref[...]
        y = xf * lax.rsqrt(jnp.mean(xf * xf, axis=-1, keepdims=True) + EPS) * g_ref[...]
        if modulated:
            mm = refs[2][...]
            y = y * (1.0 + mm[k_scale:k_scale + 1]) + mm[k_shift:k_shift + 1]
        o_ref[...] = y.astype(o_ref.dtype)

    in_specs = [pl.BlockSpec((tm, d), lambda i: (i, 0)), pl.BlockSpec((1, d), lambda i: (0, 0))]
    args = [x, g.reshape(1, d)]
    if modulated:
        rows_per_cond = m // mods.shape[0]
        in_specs.append(pl.BlockSpec((None, N_MOD, d), lambda i: ((i * tm) // rows_per_cond, 0, 0)))
        args.append(mods)
    return _call(body, name="norm", grid=(m // tm,), in_specs=in_specs,
                 out_specs=pl.BlockSpec((tm, d), lambda i: (i, 0)),
                 out_shape=jax.ShapeDtypeStruct((m, d), out_dtype), args=args,
                 temp_bytes=3 * _nbytes((tm, d), F32))


class _W(NamedTuple):
    arr: jax.Array
    layer: object
    k: int
    row_blk: int
    col0: int
    n: int

    def spec(self, tn, **mode):
        assert self.col0 % tn == 0 and self.n % tn == 0
        c0 = self.col0 // tn
        if self.layer is None:
            return pl.BlockSpec((self.k, tn), lambda i, j: (self.row_blk, c0 + j), **mode)
        return pl.BlockSpec((None, self.k, tn), lambda i, j: (self.layer, self.row_blk, c0 + j), **mode)


def _whole(arr):
    return _W(arr, None, arr.shape[0], 0, 0, arr.shape[1])


def _proj(name, lhs, rhs, epilogue, outs, *, tm, tn, extras=()):
    m = lhs[0].shape[0]
    n = rhs[0].n
    np_ = len(lhs)
    ne = len(extras)

    def body(*refs):
        acc = None
        for p in range(np_):
            part = _dot(refs[p][...].astype(BF16), refs[np_ + p][...].astype(BF16))
            acc = part if acc is None else acc + part
        epilogue(acc, refs[2 * np_:2 * np_ + ne], refs[2 * np_ + ne:])

    resident = n == tn
    rhs_mode = dict(pipeline_mode=pl.Buffered(1)) if resident else {}
    in_specs = ([pl.BlockSpec((tm, a.shape[1]), lambda i, j: (i, 0)) for a in lhs]
                + [w.spec(tn, **rhs_mode) for w in rhs]
                + [pl.BlockSpec(bs, im) for (_, bs, im) in extras])
    args = list(lhs) + [w.arr for w in rhs] + [e[0] for e in extras]
    out_shape = [jax.ShapeDtypeStruct((m, n), dt) for dt in outs]
    out_specs = [pl.BlockSpec((tm, tn), lambda i, j: (i, j)) for _ in outs]
    return _call(body, name=name, grid=(m // tm, n // tn), in_specs=in_specs, out_specs=out_specs,
                 out_shape=out_shape, args=args, temp_bytes=6 * _nbytes((tm, tn), F32),
                 single_buffered=tuple(range(np_, 2 * np_)) if resident else ())


def _ep_scale(scale):
    def ep(acc, ex, outs):
        outs[0][...] = (acc * scale).astype(outs[0].dtype)
    return ep


def _rotate(a, cos, sin):
    width = a.shape[1]
    lane = lax.broadcasted_iota(jnp.int32, a.shape, 1)
    low = (lane & (2 * ROPE_QUARTER - 1)) < ROPE_QUARTER
    partner = jnp.where(low, pltpu.roll(a, width - ROPE_QUARTER, 1), pltpu.roll(a, ROPE_QUARTER, 1))
    return a * cos + partner * sin


def _ep_rope(scale, with_plain):
    def ep(acc, ex, outs):
        a = acc * scale
        reps = a.shape[1] // ex[0].shape[1]
        cos = jnp.concatenate([ex[0][...]] * reps, axis=1)
        sin = jnp.concatenate([ex[1][...]] * reps, axis=1)
        r = _rotate(a, cos, sin)
        if with_plain:
            outs[0][...] = a.astype(outs[0].dtype)
        outs[-1][...] = r.astype(outs[-1].dtype)
    return ep


def _ep_rmsnorm(acc, ex, outs):
    y = acc * lax.rsqrt(jnp.mean(acc * acc, axis=-1, keepdims=True) + EPS) * ex[0][...]
    outs[0][...] = y.astype(outs[0].dtype)


def _ep_residual(k_gate):
    def ep(acc, ex, outs):
        gate = ex[1][...][k_gate:k_gate + 1]
        outs[0][...] = ex[0][...] + gate * acc
    return ep


def _rope_tables(seq):
    half = ROPE_DIM // 2
    inv = ROPE_BASE ** (-jnp.arange(0, half, 2, dtype=F32) / half)
    pos = jnp.arange(seq)
    ar = (pos // GRID_W).astype(F32)[:, None] * inv[None, :]
    ac = (pos % GRID_W).astype(F32)[:, None] * inv[None, :]
    cos = jnp.concatenate([jnp.cos(ar), jnp.cos(ar), jnp.cos(ac), jnp.cos(ac)], axis=1)
    sin = jnp.concatenate([-jnp.sin(ar), jnp.sin(ar), -jnp.sin(ac), jnp.sin(ac)], axis=1)
    return cos, sin


def _widen_tables(cos, sin, lead, trail, width):
    seq = cos.shape[0]
    c = jnp.concatenate([jnp.ones((seq, lead), F32), cos, jnp.ones((seq, trail), F32)], axis=1)
    s = jnp.concatenate([jnp.zeros((seq, lead), F32), sin, jnp.zeros((seq, trail), F32)], axis=1)
    reps = width // c.shape[1]
    assert reps * c.shape[1] == width
    return jnp.tile(c, (1, reps)), jnp.tile(s, (1, reps))


def _diff_attention(lam_vecs, lam_init, qs, ks, vs, grp, hps):
    m = qs[0].shape[0]
    nseg = len(qs)
    tq = _tile(grp.seq, 256)
    qt = grp.seq // tq
    width = hps * A_HEAD_DIM
    n_hg = A_HEADS // hps
    k_lens = [k.shape[0] // grp.n_seq for k in ks]

    def body(*refs):
        lam_ref = refs[0]
        q_refs = refs[1:1 + nseg]
        k_refs = refs[1 + nseg:1 + 2 * nseg]
        v_refs = refs[1 + 2 * nseg:1 + 3 * nseg]
        o_ref = refs[-1]
        lv = lam_ref[...]
        lam = (jnp.exp(jnp.sum(lv[0:1] * lv[1:2], axis=-1, keepdims=True))
               - jnp.exp(jnp.sum(lv[2:3] * lv[3:4], axis=-1, keepdims=True)) + lam_init)
        first = lax.broadcasted_iota(jnp.int32, (tq, A_HEAD_DIM), 1) < A_QK_HALF
        for g in range(hps):
            cols = slice(g * A_HEAD_DIM, (g + 1) * A_HEAD_DIM)
            scores = []
            for q_ref, k_ref in zip(q_refs, k_refs):
                q = q_ref[:, cols]
                zero = jnp.zeros_like(q)
                q2 = jnp.concatenate([jnp.where(first, q, zero), jnp.where(first, zero, q)], axis=0)
                scores.append(_dot_nt(q2, k_ref[:, cols].astype(BF16)))
            mx = functools.reduce(jnp.maximum, [jnp.max(s, axis=-1, keepdims=True) for s in scores])
            ps = [jnp.exp(s - mx) for s in scores]
            den = functools.reduce(jnp.add, [jnp.sum(p, axis=-1, keepdims=True) for p in ps])
            o2 = functools.reduce(
                jnp.add, [_dot(p.astype(BF16), v_ref[:, cols].astype(BF16)) for p, v_ref in zip(ps, v_refs)])
            o2 = o2 / den
            o = o2[:tq] - lam * o2[tq:]
            o = o * lax.rsqrt(jnp.mean(o * o, axis=-1, keepdims=True) + EPS) * (1.0 - lam_init)
            o_ref[:, cols] = o.astype(o_ref.dtype)

    q_spec = pl.BlockSpec((tq, width), lambda b, h, i: (b * qt + i, h))
    in_specs = ([pl.BlockSpec(lam_vecs.shape, lambda b, h, i: (0, 0))]
                + [q_spec] * nseg
                + [pl.BlockSpec((kl, width), lambda b, h, i: (b, h)) for kl in k_lens] * 2)
    nk = sum(k_lens)
    return _call(body, name="diff_attention", grid=(grp.n_seq, n_hg, qt), in_specs=in_specs,
                 out_specs=q_spec, out_shape=jax.ShapeDtypeStruct((m, A_HEADS * A_V_DIM), BF16),
                 args=[lam_vecs] + list(qs) + list(ks) + list(vs),
                 temp_bytes=4 * _nbytes((2 * tq, nk), F32))


def _mla_attention(qs, kns, krs, vs, grp, hps):
    m = qs[0].shape[0]
    nseg = len(qs)
    tq = _tile(grp.seq, 256)
    qt = grp.seq // tq
    n_hg = B_HEADS // hps
    k_lens = [k.shape[0] // grp.n_seq for k in kns]

    def body(*refs):
        q_refs = refs[:nseg]
        kn_refs = refs[nseg:2 * nseg]
        kr_refs = refs[2 * nseg:3 * nseg]
        v_refs = refs[3 * nseg:4 * nseg]
        o_ref = refs[-1]
        krs_v = [r[...].astype(BF16) for r in kr_refs]
        for g in range(hps):
            kcols = slice(g * NOPE_DIM, (g + 1) * NOPE_DIM)
            vcols = slice(g * B_V_DIM, (g + 1) * B_V_DIM)
            scores = []
            for q_ref, kn_ref, kr in zip(q_refs, kn_refs, krs_v):
                q = q_ref[:, g * MLA_Q_PAD:(g + 1) * MLA_Q_PAD]
                k = jnp.concatenate([kn_ref[:, kcols], kr], axis=1)
                scores.append(_dot_nt(q, k))
            mx = functools.reduce(jnp.maximum, [jnp.max(s, axis=-1, keepdims=True) for s in scores])
            ps = [jnp.exp(s - mx) for s in scores]
            den = functools.reduce(jnp.add, [jnp.sum(p, axis=-1, keepdims=True) for p in ps])
            o = functools.reduce(
                jnp.add, [_dot(p.astype(BF16), v_ref[:, vcols]) for p, v_ref in zip(ps, v_refs)])
            o_ref[:, vcols] = (o / den).astype(o_ref.dtype)

    q_spec = pl.BlockSpec((tq, hps * MLA_Q_PAD), lambda b, h, i: (b * qt + i, h))
    kv_specs = [pl.BlockSpec((kl, hps * NOPE_DIM), lambda b, h, i: (b, h)) for kl in k_lens]
    kr_specs = [pl.BlockSpec((kl, MLA_KR_PAD), lambda b, h, i: (b, 0)) for kl in k_lens]
    nk = sum(k_lens)
    return _call(body, name="mla_attention", grid=(grp.n_seq, n_hg, qt),
                 in_specs=[q_spec] * nseg + kv_specs + kr_specs + kv_specs,
                 out_specs=pl.BlockSpec((tq, hps * B_V_DIM), lambda b, h, i: (b * qt + i, h)),
                 out_shape=jax.ShapeDtypeStruct((m, B_HEADS * B_V_DIM), BF16),
                 args=list(qs) + list(kns) + list(krs) + list(vs),
                 temp_bytes=4 * _nbytes((tq, nk), F32) + 2 * _nbytes((nk, MLA_Q_PAD), BF16))


def _pool_mixer(h, x, mods, w_pool, layer, pool_scale, grp, k_gate):
    m, d = x.shape
    _, n_groups, c, _ = w_pool.shape
    assert max(POOL_WINDOWS) // 2 <= POOL_HALO and n_groups == len(POOL_WINDOWS)
    r = _tile(grp.seq, 512, POOL_HALO)
    nt = m // r
    halo_per_tile = r // POOL_HALO
    rows_per_cond = m // mods.shape[0]
    seq = grp.seq
    kc = r + 2 * POOL_HALO

    def body(hp_ref, hc_ref, hn_ref, w_ref, ps_ref, x_ref, m_ref, o_ref):
        g = pl.program_id(0)
        i = pl.program_id(1)
        back = functools.reduce(jnp.add, [jnp.where(g == k, w // 2, 0) for k, w in enumerate(POOL_WINDOWS)])
        fwd = functools.reduce(jnp.add, [jnp.where(g == k, w - w // 2, 0) for k, w in enumerate(POOL_WINDOWS)])
        seq_start = ((i * r) // seq) * seq
        row = i * r + lax.broadcasted_iota(jnp.int32, (r, kc), 0)
        col = i * r - POOL_HALO + lax.broadcasted_iota(jnp.int32, (r, kc), 1)
        lo = jnp.maximum(row - back, seq_start)
        hi = jnp.minimum(row + fwd, seq_start + seq)
        member = jnp.where((col >= lo) & (col < hi), 1.0, 0.0).astype(BF16)
        hc = hc_ref[...]
        hcat = jnp.concatenate([hp_ref[...], hc, hn_ref[...]], axis=0)
        win = _dot(member, hcat)
        row_c = i * r + lax.broadcasted_iota(jnp.int32, (r, c), 0)
        cnt = (jnp.minimum(row_c + fwd, seq_start + seq) - jnp.maximum(row_c - back, seq_start)).astype(F32)
        pooled = win / cnt - hc.astype(F32)
        y = _dot(pooled.astype(BF16), w_ref[...]) * ps_ref[...]
        gate = m_ref[...][k_gate:k_gate + 1]
        o_ref[...] = x_ref[...] + gate * y

    last_halo = m // POOL_HALO - 1
    in_specs = [
        pl.BlockSpec((POOL_HALO, c), lambda g, i: (jnp.maximum(i * halo_per_tile - 1, 0), g)),
        pl.BlockSpec((r, c), lambda g, i: (i, g)),
        pl.BlockSpec((POOL_HALO, c), lambda g, i: (jnp.minimum((i + 1) * halo_per_tile, last_halo), g)),
        pl.BlockSpec((None, None, c, c), lambda g, i: (layer, g, 0, 0)),
        pl.BlockSpec((1, c), lambda g, i: (0, g)),
        pl.BlockSpec((r, c), lambda g, i: (i, g)),
        pl.BlockSpec((None, N_MOD, c), lambda g, i: ((i * r) // rows_per_cond, 0, g)),
    ]
    return _call(body, name="pool_mixer", grid=(n_groups, nt), in_specs=in_specs,
                 out_specs=pl.BlockSpec((r, c), lambda g, i: (i, g)),
                 out_shape=jax.ShapeDtypeStruct((m, d), F32),
                 args=[h, h, h, w_pool, pool_scale.reshape(1, d), x, mods],
                 temp_bytes=4 * _nbytes((r, kc), F32) + 4 * _nbytes((r, c), F32))


def _ffn_up(h, w_up, conv_w, conv_b, layer, grp):
    m, d = h.shape
    f = w_up.shape[2] // 2
    tm = _tile(m, max(grp.seq, 2048), grp.seq)
    tn = _tile(f, 256, V7X_LANES)
    nj = f // tn
    seq = grp.seq

    rc = _tile(seq, 256)
    pad = 8
    rs = _tile(tm, 512, V7X_SUBLANES_BF16)

    def body(h_ref, wa_ref, wb_ref, cwa_ref, cwb_ref, cba_ref, cbb_ref, o_ref, acc_ref):
        row8 = lax.broadcasted_iota(jnp.int32, (8, tn), 0)

        def matmul(s):
            xs = h_ref[s * rs:(s + 1) * rs, :]
            for k, w_ref in enumerate((wa_ref, wb_ref)):
                acc_ref[k, pad + s * rs:pad + (s + 1) * rs] = _dot(xs, w_ref[...].astype(BF16))

        def piece(p0, p1):
            n = p1 - p0

            def conv(k, cw_ref, cb_ref):
                cw = cw_ref[...]
                u = acc_ref[k, pad + p0:pad + p1]
                prev = acc_ref[k, pad + p0 - 1:pad + p1 - 1]
                nxt = acc_ref[k, pad + p0 + 1:pad + p1 + 1]
                if p0 % seq == 0:
                    top = jnp.where(row8 == 0, 0.0, prev[:8])
                    prev = top if n == 8 else jnp.concatenate([top, prev[8:]], axis=0)
                if p1 % seq == 0:
                    bot = jnp.where(row8 == 7, 0.0, nxt[n - 8:])
                    nxt = bot if n == 8 else jnp.concatenate([nxt[:n - 8], bot], axis=0)
                return prev * cw[0:1] + u * cw[1:2] + nxt * cw[2:3] + cb_ref[...]

            a = conv(0, cwa_ref, cba_ref)
            b = conv(1, cwb_ref, cbb_ref)
            o_ref[p0:p1] = (_silu(a) * b).astype(o_ref.dtype)

        def epilogue(r0, r1):
            cuts = sorted({r0, r1} | {c for c in range(0, tm + 1, rc) if r0 < c < r1})
            for p0, p1 in zip(cuts[:-1], cuts[1:]):
                piece(p0, p1)

        n_stage = tm // rs
        for k in range(2):
            acc_ref[k, 0:pad] = jnp.zeros((pad, tn), F32)
            acc_ref[k, pad + tm:2 * pad + tm] = jnp.zeros((pad, tn), F32)
        for s in range(n_stage + 1):
            if s > 0:
                epilogue(max((s - 1) * rs - pad, 0), tm if s == n_stage else s * rs - pad)
            if s < n_stage:
                matmul(s)

    in_specs = [
        pl.BlockSpec((tm, d), lambda i, j: (i, 0), pipeline_mode=pl.Buffered(1)),
        pl.BlockSpec((None, d, tn), lambda i, j: (layer, 0, j)),
        pl.BlockSpec((None, d, tn), lambda i, j: (layer, 0, nj + j)),
        pl.BlockSpec((None, CONV_W, tn), lambda i, j: (layer, 0, j)),
        pl.BlockSpec((None, CONV_W, tn), lambda i, j: (layer, 0, nj + j)),
        pl.BlockSpec((None, 1, tn), lambda i, j: (layer, 0, j)),
        pl.BlockSpec((None, 1, tn), lambda i, j: (layer, 0, nj + j)),
    ]
    cb = conv_b.reshape(conv_b.shape[0], 1, 2 * f)
    return _call(body, name="ffn_up", grid=(m // tm, nj), in_specs=in_specs,
                 out_specs=pl.BlockSpec((tm, tn), lambda i, j: (i, j)),
                 out_shape=jax.ShapeDtypeStruct((m, f), BF16),
                 args=[h, w_up, w_up, conv_w, conv_w, cb, cb],
                 temp_bytes=10 * _nbytes((rc, tn), F32) + 4 * _nbytes((tm, tn), F32), single_buffered=(0,),
                 scratch=[((2, tm + 2 * pad, tn), F32)])


def _residual_proj(name, lhs, rhs, x, mods, k_gate, *, tm, tn):
    m = x.shape[0]
    rows_per_cond = m // mods.shape[0]
    assert rows_per_cond % tm == 0
    extras = [(x, (tm, tn), lambda i, j: (i, j)),
              (mods, (None, N_MOD, tn), lambda i, j: ((i * tm) // rows_per_cond, 0, j))]
    return _proj(name, lhs, rhs, _ep_residual(k_gate), (F32,), tm=tm, tn=tn, extras=extras)[0]


class _AttnWeights(NamedTuple):
    qa: _W
    ka: _W
    va: _W
    cq: _W
    ckv: _W
    kr: _W
    g_cq: jax.Array
    g_ckv: jax.Array
    q_up: _W
    kn: _W
    vb: _W
    o_a: _W
    o_b: _W
    lam_vecs: jax.Array


def _prep_attn_weights(j, w_in_b, w_in, g_cq, w_q_up, g_ckv, w_kv_up, w_o_b, lam_vecs):
    d = w_in.shape[0]
    a_q = A_HEADS * A_HEAD_DIM
    a_v = A_HEADS * A_V_DIM
    b_out = B_HEADS * B_V_DIM
    assert a_v == b_out
    q_lora = g_cq.shape[0]
    kv_lora = g_ckv.shape[0]
    o0, o1, o2 = a_q, 2 * a_q, 2 * a_q + a_v
    o3, o4 = o2 + q_lora, o2 + q_lora + kv_lora
    kr = jnp.pad(w_in[:, o4:], ((0, 0), (0, MLA_KR_PAD - ROPE_DIM)))
    q_up = w_q_up.reshape(q_lora, B_HEADS, NOPE_DIM + ROPE_DIM)
    q_up = jnp.pad(q_up, ((0, 0), (0, 0), (0, MLA_Q_PAD - NOPE_DIM - ROPE_DIM)))
    kv = w_kv_up.reshape(kv_lora, B_HEADS, NOPE_DIM + B_V_DIM)
    c = lambda w: _whole(w.astype(BF16))
    win = lambda col0, n: _W(w_in_b, j, d, 0, col0, n)
    return _AttnWeights(
        qa=win(0, a_q), ka=win(o0, a_q), va=win(o1, a_v), cq=win(o2, q_lora), ckv=win(o3, kv_lora),
        kr=c(kr), g_cq=g_cq.reshape(1, q_lora), g_ckv=g_ckv.reshape(1, kv_lora),
        q_up=c(q_up.reshape(q_lora, B_HEADS * MLA_Q_PAD)),
        kn=c(kv[:, :, :NOPE_DIM].reshape(kv_lora, B_HEADS * NOPE_DIM)),
        vb=c(kv[:, :, NOPE_DIM:].reshape(kv_lora, B_HEADS * B_V_DIM)),
        o_a=_W(w_o_b, j, a_v, 0, 0, w_o_b.shape[-1]), o_b=_W(w_o_b, j, b_out, 1, 0, w_o_b.shape[-1]),
        lam_vecs=lam_vecs)


def _attn_mixer(h, x, mods, w, grp, lam_init, cache):
    m, d = h.shape
    hps = 4
    qk_scale = A_QK_HALF ** -0.5
    mla_scale = (NOPE_DIM + ROPE_DIM) ** -0.5
    rows_unit = grp.seq if grp.rope else m

    def tiles(lhs, rhs, heavy_epilogue):
        k, n = rhs.k, rhs.n
        pref = 1024 if k < 1024 else (256 if heavy_epilogue else 512)
        return _tile(min(rows_unit, lhs.shape[0]), pref, V7X_SUBLANES_BF16), min(n, 2048)

    def plain(name, lhs, rhs, dtype, scale=1.0):
        tm_, tn_ = tiles(lhs, rhs, False)
        return _proj(name, [lhs], [rhs], _ep_scale(scale), (dtype,), tm=tm_, tn=tn_)[0]

    def roped(name, lhs, rhs, tables, dtype, scale, with_plain):
        tm_, tn_ = tiles(lhs, rhs, True)
        tiles_per_seq = grp.seq // tm_
        extras = [(t, (tm_, t.shape[1]), lambda i, j: (i % tiles_per_seq, 0)) for t in tables]
        outs = (dtype, dtype) if with_plain else (dtype,)
        return _proj(name, [lhs], [rhs], _ep_rope(scale, with_plain), outs, tm=tm_, tn=tn_, extras=extras)

    def normed(name, rhs, gain, dtype):
        tm_, _ = tiles(h, rhs, False)
        n = rhs.n
        return _proj(name, [h], [rhs], _ep_rmsnorm, (dtype,), tm=tm_, tn=n,
                     extras=[(gain, (1, n), lambda i, j: (0, 0))])[0]

    cqn = normed("cq_norm", w.cq, w.g_cq, BF16)
    ckvn = normed("ckv_norm", w.ckv, w.g_ckv, F32)
    kn = plain("k_nope", ckvn, w.kn, BF16)
    vb = plain("v_mla", ckvn, w.vb, BF16)

    if not grp.rope:
        qa = plain("q_diff", h, w.qa, BF16, qk_scale)
        ka = plain("k_diff", h, w.ka, F32)
        va = plain("v_diff", h, w.va, F32)
        kr = plain("k_rope", h, w.kr, F32)
        q = plain("q_mla", cqn, w.q_up, BF16, mla_scale)
        oa = _diff_attention(w.lam_vecs, lam_init, [qa], [ka], [va], grp, hps)
        ob = _mla_attention([q], [kn], [kr], [vb], grp, hps)
        state = (ka, va, ckvn, kr[:, :ROPE_DIM])
    else:
        cos, sin = _rope_tables(grp.seq)
        t_diff = _widen_tables(cos, sin, 0, 0, A_HEAD_DIM)
        t_mla = _widen_tables(cos, sin, NOPE_DIM, MLA_Q_PAD - NOPE_DIM - ROPE_DIM, MLA_Q_PAD)
        t_kr = _widen_tables(cos, sin, 0, MLA_KR_PAD - ROPE_DIM, MLA_KR_PAD)
        ka_c, va_c, ckv_c, kr_c = cache
        qa_u, qa_r = roped("q_diff", h, w.qa, t_diff, BF16, qk_scale, True)
        ka_r, = roped("k_diff", h, w.ka, t_diff, BF16, 1.0, False)
        va = plain("v_diff", h, w.va, BF16)
        kr_r, = roped("k_rope", h, w.kr, t_kr, BF16, 1.0, False)
        q_u, q_r = roped("q_mla", cqn, w.q_up, t_mla, BF16, mla_scale, True)
        kn_c = plain("k_nope_ctx", ckv_c, w.kn, BF16)
        vb_c = plain("v_mla_ctx", ckv_c, w.vb, BF16)
        oa = _diff_attention(w.lam_vecs, lam_init, [qa_u, qa_r], [ka_c, ka_r], [va_c, va], grp, hps)
        ob = _mla_attention([q_u, q_r], [kn_c, kn], [kr_c, kr_r], [vb_c, vb], grp, hps)
        state = None
    x1 = _residual_proj("o_proj", [oa, ob], [w.o_a, w.o_b], x, mods, 2,
                        tm=_tile(rows_unit, 1024, V7X_SUBLANES_BF16), tn=512)
    return x1, state


def _conv_ffn(x, mods, g2, w_up, conv_w, conv_b, w_down, layer, grp):
    h = _norm(x, g2, mods, 3, 4, BF16)
    act = _ffn_up(h, w_up, conv_w, conv_b, layer, grp)
    w_dn = _W(w_down, layer, w_down.shape[1], 0, 0, w_down.shape[2])
    return _residual_proj("ffn_down", [act], [w_dn], x, mods, 5,
                          tm=_tile(x.shape[0] // mods.shape[0], 512, V7X_SUBLANES_BF16), tn=512)


def kernel(x_prompt, x_sample, cache_diff_k, cache_diff_v, cache_mla_ckv, cache_mla_krope, c, c_ctx,
           norm1_g, norm2_g, w_mod, b_mod, w_in, g_cq, w_q_up, g_ckv, w_kv_up,
           lambda_q1, lambda_k1, lambda_q2, lambda_k2, w_o, w_pool, pool_scale,
           w_up, conv_w, conv_b, w_down, g_final):
    batch, seq, d = x_prompt.shape
    dec_batch, dec_seq, _ = x_sample.shape
    depth = w_mod.shape[0]
    past = cache_diff_k.shape[2]
    groups = (_Group(batch, seq, False), _Group(dec_batch, dec_seq, True))

    n_cond = 1 + dec_batch
    cond8 = jnp.zeros((8, d), F32).at[0].set(c_ctx).at[1:n_cond].set(c)
    mods_all = _adaln(cond8, w_mod, b_mod).reshape(depth, 8, N_MOD, d)
    mods_g = (mods_all[:, 0:1], mods_all[:, 1:n_cond])

    w_in_b = w_in.astype(BF16)
    w_o_b = w_o.astype(BF16)
    attn_w = {}
    for i in range(0, depth, 2):
        j = i // 2
        lam_vecs = jnp.stack([lambda_q1[j], lambda_k1[j], lambda_q2[j], lambda_k2[j]])
        attn_w[i] = _prep_attn_weights(j, w_in_b, w_in[j], g_cq[j], w_q_up[j], g_ckv[j], w_kv_up[j],
                                       w_o_b, lam_vecs)
    w_pool_b = w_pool.astype(BF16)
    w_up_b = w_up.astype(BF16)
    w_down_b = w_down.astype(BF16)

    xs = [x_prompt.reshape(batch * seq, d), x_sample.reshape(dec_batch * dec_seq, d)]
    states = []
    for gi, grp in enumerate(groups):
        x = xs[gi]
        for i in range(depth):
            mods = mods_g[gi][i]
            h = _norm(x, norm1_g[i], mods, 0, 1, BF16)
            if i % 2 == 0:
                j = i // 2
                lam_init = 0.8 - 0.6 * math.exp(-0.3 * i)
                cache = None
                if grp.rope:
                    kr_c = jnp.pad(cache_mla_krope[:, j].reshape(dec_batch * past, ROPE_DIM),
                                   ((0, 0), (0, MLA_KR_PAD - ROPE_DIM)))
                    cache = (cache_diff_k[:, j].reshape(dec_batch * past, A_HEADS * A_HEAD_DIM),
                             cache_diff_v[:, j].reshape(dec_batch * past, A_HEADS * A_V_DIM),
                             cache_mla_ckv[:, j].reshape(dec_batch * past, -1), kr_c)
                x, st = _attn_mixer(h, x, mods, attn_w[i], grp, lam_init, cache)
                if st is not None:
                    states.append(st)
            else:
                x = _pool_mixer(h, x, mods, w_pool_b, i // 2, pool_scale[i // 2], grp, 2)
            x = _conv_ffn(x, mods, norm2_g[i], w_up_b, conv_w, conv_b, w_down_b, i, grp)
        xs[gi] = _norm(x, g_final, None, 0, 0, F32)

    y_prompt = xs[0].reshape(batch, seq, d)
    y_sample = xs[1].reshape(dec_batch, dec_seq, d)
    stack = lambda k, tail: jnp.stack([s[k].reshape((batch, seq) + tail) for s in states], axis=1)
    new_diff_k = stack(0, (A_HEADS, A_HEAD_DIM))
    new_diff_v = stack(1, (A_HEADS, A_V_DIM))
    new_mla_ckv = stack(2, (g_ckv.shape[-1],))
    new_mla_krope = stack(3, (ROPE_DIM,))
    return (y_prompt, y_sample, new_diff_k, new_diff_v, new_mla_ckv, new_mla_krope)
```

```python
import functools
import math
from typing import NamedTuple

import jax
import jax.numpy as jnp
from jax import lax
from jax.experimental import pallas as pl
from jax.experimental.pallas import tpu as pltpu

F32 = jnp.float32
BF16 = jnp.bfloat16

GRID_W = 64
ROPE_BASE = 10000.0
EPS = 1e-6
A_HEADS = 16
A_QK_HALF = 64
A_HEAD_DIM = 2 * A_QK_HALF
A_V_DIM = 128
B_HEADS = 16
NOPE_DIM = 128
ROPE_DIM = 64
B_V_DIM = 128
POOL_WINDOWS = (2, 4, 8, 16)
N_MOD = 6
CONV_W = 3

V7X_VMEM_BYTES = 64 * 1024 * 1024
V7X_LANES = 128
V7X_SUBLANES_BF16 = 16
COMPILER_SCRATCH_BYTES = 2 * 1024 * 1024

MLA_Q_PAD = 256
MLA_KR_PAD = MLA_Q_PAD - NOPE_DIM
ROPE_QUARTER = ROPE_DIM // 4
POOL_HALO = V7X_SUBLANES_BF16


class _Group(NamedTuple):
    n_seq: int
    seq: int
    rope: bool


def _tile(n, pref, mult=8):
    if n <= pref:
        return n
    t = (pref // mult) * mult
    while t >= mult:
        if n % t == 0:
            return t
        t -= mult
    return n


def _nbytes(shape, dtype):
    return math.prod(shape) * jnp.dtype(dtype).itemsize


def _call(body, *, name, grid, in_specs, out_specs, out_shape, args, temp_bytes=0, single_buffered=(),
          scratch=(), semantics=None, flags=None):
    multi = isinstance(out_shape, (list, tuple))
    outs = list(out_shape) if multi else [out_shape]
    ospecs = list(out_specs) if multi else [out_specs]
    total = temp_bytes + COMPILER_SCRATCH_BYTES + sum(_nbytes(s, dt) for s, dt in scratch)
    for k, (a, s) in enumerate(zip(args, in_specs)):
        blk = [1 if b is None else b for b in s.block_shape]
        total += _nbytes(blk, a.dtype) * (1 if k in single_buffered else 2)
    for o, s in zip(outs, ospecs):
        blk = [1 if b is None else b for b in s.block_shape]
        total += 2 * _nbytes(blk, o.dtype)
    limit = min(V7X_VMEM_BYTES - 4 * 1024 * 1024, max(total, 16 * 1024 * 1024))
    return pl.pallas_call(
        body, name=name, grid=grid, in_specs=in_specs, out_specs=out_specs, out_shape=out_shape,
        scratch_shapes=[pltpu.VMEM(s, dt) for s, dt in scratch],
        compiler_params=pltpu.CompilerParams(
            dimension_semantics=semantics or ("parallel",) * len(grid), vmem_limit_bytes=int(limit),
            flags=flags),
    )(*args)


def _dot(a, b):
    return jnp.dot(a, b, preferred_element_type=F32)


def _dot_nt(a, b):
    return lax.dot_general(a, b, (((1,), (1,)), ((), ())), preferred_element_type=F32)


def _silu(x):
    return x / (1.0 + jnp.exp(-x))


def _adaln(cond8, w_mod, b_mod):
    depth, d, n = w_mod.shape
    tn = next(t for t in (512, 256, 128) if n % (2 * t) == 0)

    def body(c_ref, wa_ref, wb_ref, b_ref, o_ref):
        s = _silu(c_ref[...]).astype(BF16)
        ya = _dot(s, wa_ref[...].astype(BF16))
        yb = _dot(s, wb_ref[...].astype(BF16))
        o_ref[...] = jnp.concatenate([ya, yb], axis=1) + b_ref[...]

    return _call(
        body, name="adaln", grid=(depth, n // (2 * tn)),
        in_specs=[pl.BlockSpec((8, d), lambda l, j: (0, 0)),
                  pl.BlockSpec((None, d, tn), lambda l, j: (l, 0, 2 * j)),
                  pl.BlockSpec((None, d, tn), lambda l, j: (l, 0, 2 * j + 1)),
                  pl.BlockSpec((None, 1, 2 * tn), lambda l, j: (l, 0, j))],
        out_specs=pl.BlockSpec((None, 8, 2 * tn), lambda l, j: (l, 0, j)),
        out_shape=jax.ShapeDtypeStruct((depth, 8, n), F32),
        args=(cond8, w_mod, w_mod, b_mod.reshape(depth, 1, n)),
        temp_bytes=2 * _nbytes((d, tn), BF16) + 2 * _nbytes((d, tn), F32))


def _norm(x, g, mods, k_shift, k_scale, out_dtype):
    m, d = x.shape
    modulated = mods is not None
    tm = _tile(m // mods.shape[0] if modulated else m, 512)

    def body(*refs):
        x_ref, g_ref = refs[0], refs[1]
        o_ref = refs[-1]
        xf = x_ref[...]
        y = xf * lax.rsqrt(jnp.mean(xf * xf, axis=-1, keepdims=True) + EPS) * g_ref[...]
        if modulated:
            mm = refs[2][...]
            y = y * (1.0 + mm[k_scale:k_scale + 1]) + mm[k_shift:k_shift + 1]
        o_ref[...] = y.astype(o_ref.dtype)

    in_specs = [pl.BlockSpec((tm, d), lambda i: (i, 0)), pl.BlockSpec((1, d), lambda i: (0, 0))]
    args = [x, g.reshape(1, d)]
    if modulated:
        rows_per_cond = m // mods.shape[0]
        in_specs.append(pl.BlockSpec((None, N_MOD, d), lambda i: ((i * tm) // rows_per_cond, 0, 0)))
        args.append(mods)
    return _call(body, name="norm", grid=(m // tm,), in_specs=in_specs,
                 out_specs=pl.BlockSpec((tm, d), lambda i: (i, 0)),
                 out_shape=jax.ShapeDtypeStruct((m, d), out_dtype), args=args,
                 temp_bytes=3 * _nbytes((tm, d), F32))


class _W(NamedTuple):
    arr: jax.Array
    layer: object
    k: int
    row_blk: int
    col0: int
    n: int

    def spec(self, tn, **mode):
        assert self.col0 % tn == 0 and self.n % tn == 0
        c0 = self.col0 // tn
        if self.layer is None:
            return pl.BlockSpec((self.k, tn), lambda i, j: (self.row_blk, c0 + j), **mode)
        return pl.BlockSpec((None, self.k, tn), lambda i, j: (self.layer, self.row_blk, c0 + j), **mode)


def _whole(arr):
    return _W(arr, None, arr.shape[0], 0, 0, arr.shape[1])


def _proj(name, lhs, rhs, epilogue, outs, *, tm, tn, extras=()):
    m = lhs[0].shape[0]
    n = rhs[0].n
    np_ = len(lhs)
    ne = len(extras)

    def body(*refs):
        acc = None
        for p in range(np_):
            part = _dot(refs[p][...].astype(BF16), refs[np_ + p][...].astype(BF16))
            acc = part if acc is None else acc + part
        epilogue(acc, refs[2 * np_:2 * np_ + ne], refs[2 * np_ + ne:])

    resident = n == tn
    rhs_mode = dict(pipeline_mode=pl.Buffered(1)) if resident else {}
    in_specs = ([pl.BlockSpec((tm, a.shape[1]), lambda i, j: (i, 0)) for a in lhs]
                + [w.spec(tn, **rhs_mode) for w in rhs]
                + [pl.BlockSpec(bs, im) for (_, bs, im) in extras])
    args = list(lhs) + [w.arr for w in rhs] + [e[0] for e in extras]
    out_shape = [jax.ShapeDtypeStruct((m, n), dt) for dt in outs]
    out_specs = [pl.BlockSpec((tm, tn), lambda i, j: (i, j)) for _ in outs]
    return _call(body, name=name, grid=(m // tm, n // tn), in_specs=in_specs, out_specs=out_specs,
                 out_shape=out_shape, args=args, temp_bytes=6 * _nbytes((tm, tn), F32),
                 single_buffered=tuple(range(np_, 2 * np_)) if resident else ())


def _ep_scale(scale):
    def ep(acc, ex, outs):
        outs[0][...] = (acc * scale).astype(outs[0].dtype)
    return ep


def _rotate(a, cos, sin):
    width = a.shape[1]
    lane = lax.broadcasted_iota(jnp.int32, a.shape, 1)
    low = (lane & (2 * ROPE_QUARTER - 1)) < ROPE_QUARTER
    partner = jnp.where(low, pltpu.roll(a, width - ROPE_QUARTER, 1), pltpu.roll(a, ROPE_QUARTER, 1))
    return a * cos + partner * sin


def _ep_rope(scale, with_plain):
    def ep(acc, ex, outs):
        a = acc * scale
        reps = a.shape[1] // ex[0].shape[1]
        cos = jnp.concatenate([ex[0][...]] * reps, axis=1)
        sin = jnp.concatenate([ex[1][...]] * reps, axis=1)
        r = _rotate(a, cos, sin)
        if with_plain:
            outs[0][...] = a.astype(outs[0].dtype)
        outs[-1][...] = r.astype(outs[-1].dtype)
    return ep


def _ep_rmsnorm(acc, ex, outs):
    y = acc * lax.rsqrt(jnp.mean(acc * acc, axis=-1, keepdims=True) + EPS) * ex[0][...]
    outs[0][...] = y.astype(outs[0].dtype)


def _ep_residual(k_gate):
    def ep(acc, ex, outs):
        gate = ex[1][...][k_gate:k_gate + 1]
        outs[0][...] = ex[0][...] + gate * acc
    return ep


def _rope_tables(seq):
    half = ROPE_DIM // 2
    inv = ROPE_BASE ** (-jnp.arange(0, half, 2, dtype=F32) / half)
    pos = jnp.arange(seq)
    ar = (pos // GRID_W).astype(F32)[:, None] * inv[None, :]
    ac = (pos % GRID_W).astype(F32)[:, None] * inv[None, :]
    cos = jnp.concatenate([jnp.cos(ar), jnp.cos(ar), jnp.cos(ac), jnp.cos(ac)], axis=1)
    sin = jnp.concatenate([-jnp.sin(ar), jnp.sin(ar), -jnp.sin(ac), jnp.sin(ac)], axis=1)
    return cos, sin


def _widen_tables(cos, sin, lead, trail, width):
    seq = cos.shape[0]
    c = jnp.concatenate([jnp.ones((seq, lead), F32), cos, jnp.ones((seq, trail), F32)], axis=1)
    s = jnp.concatenate([jnp.zeros((seq, lead), F32), sin, jnp.zeros((seq, trail), F32)], axis=1)
    reps = width // c.shape[1]
    assert reps * c.shape[1] == width
    return jnp.tile(c, (1, reps)), jnp.tile(s, (1, reps))


def _diff_attention(lam_vecs, lam_init, qs, ks, vs, grp, hps):
    m = qs[0].shape[0]
    nseg = len(qs)
    tq = _tile(grp.seq, 256)
    qt = grp.seq // tq
    width = hps * A_HEAD_DIM
    n_hg = A_HEADS // hps
    k_lens = [k.shape[0] // grp.n_seq for k in ks]

    def body(*refs):
        lam_ref = refs[0]
        q_refs = refs[1:1 + nseg]
        k_refs = refs[1 + nseg:1 + 2 * nseg]
        v_refs = refs[1 + 2 * nseg:1 + 3 * nseg]
        o_ref = refs[-1]
        lv = lam_ref[...]
        lam = (jnp.exp(jnp.sum(lv[0:1] * lv[1:2], axis=-1, keepdims=True))
               - jnp.exp(jnp.sum(lv[2:3] * lv[3:4], axis=-1, keepdims=True)) + lam_init)
        first = lax.broadcasted_iota(jnp.int32, (tq, A_HEAD_DIM), 1) < A_QK_HALF
        for g in range(hps):
            cols = slice(g * A_HEAD_DIM, (g + 1) * A_HEAD_DIM)
            scores = []
            for q_ref, k_ref in zip(q_refs, k_refs):
                q = q_ref[:, cols]
                zero = jnp.zeros_like(q)
                q2 = jnp.concatenate([jnp.where(first, q, zero), jnp.where(first, zero, q)], axis=0)
                scores.append(_dot_nt(q2, k_ref[:, cols].astype(BF16)))
            mx = functools.reduce(jnp.maximum, [jnp.max(s, axis=-1, keepdims=True) for s in scores])
            ps = [jnp.exp(s - mx) for s in scores]
            den = functools.reduce(jnp.add, [jnp.sum(p, axis=-1, keepdims=True) for p in ps])
            o2 = functools.reduce(
                jnp.add, [_dot(p.astype(BF16), v_ref[:, cols].astype(BF16)) for p, v_ref in zip(ps, v_refs)])
            o2 = o2 / den
            o = o2[:tq] - lam * o2[tq:]
            o = o * lax.rsqrt(jnp.mean(o * o, axis=-1, keepdims=True) + EPS) * (1.0 - lam_init)
            o_ref[:, cols] = o.astype(o_ref.dtype)

    q_spec = pl.BlockSpec((tq, width), lambda b, h, i: (b * qt + i, h))
    in_specs = ([pl.BlockSpec(lam_vecs.shape, lambda b, h, i: (0, 0))]
                + [q_spec] * nseg
                + [pl.BlockSpec((kl, width), lambda b, h, i: (b, h)) for kl in k_lens] * 2)
    nk = sum(k_lens)
    return _call(body, name="diff_attention", grid=(grp.n_seq, n_hg, qt), in_specs=in_specs,
                 out_specs=q_spec, out_shape=jax.ShapeDtypeStruct((m, A_HEADS * A_V_DIM), BF16),
                 args=[lam_vecs] + list(qs) + list(ks) + list(vs),
                 temp_bytes=4 * _nbytes((2 * tq, nk), F32))


def _mla_attention(qs, kns, krs, vs, grp, hps):
    m = qs[0].shape[0]
    nseg = len(qs)
    tq = _tile(grp.seq, 256)
    qt = grp.seq // tq
    n_hg = B_HEADS // hps
    k_lens = [k.shape[0] // grp.n_seq for k in kns]

    def body(*refs):
        q_refs = refs[:nseg]
        kn_refs = refs[nseg:2 * nseg]
        kr_refs = refs[2 * nseg:3 * nseg]
        v_refs = refs[3 * nseg:4 * nseg]
        o_ref = refs[-1]
        krs_v = [r[...].astype(BF16) for r in kr_refs]
        for g in range(hps):
            kcols = slice(g * NOPE_DIM, (g + 1) * NOPE_DIM)
            vcols = slice(g * B_V_DIM, (g + 1) * B_V_DIM)
            scores = []
            for q_ref, kn_ref, kr in zip(q_refs, kn_refs, krs_v):
                q = q_ref[:, g * MLA_Q_PAD:(g + 1) * MLA_Q_PAD]
                k = jnp.concatenate([kn_ref[:, kcols], kr], axis=1)
                scores.append(_dot_nt(q, k))
            mx = functools.reduce(jnp.maximum, [jnp.max(s, axis=-1, keepdims=True) for s in scores])
            ps = [jnp.exp(s - mx) for s in scores]
            den = functools.reduce(jnp.add, [jnp.sum(p, axis=-1, keepdims=True) for p in ps])
            o = functools.reduce(
                jnp.add, [_dot(p.astype(BF16), v_ref[:, vcols]) for p, v_ref in zip(ps, v_refs)])
            o_ref[:, vcols] = (o / den).astype(o_ref.dtype)

    q_spec = pl.BlockSpec((tq, hps * MLA_Q_PAD), lambda b, h, i: (b * qt + i, h))
    kv_specs = [pl.BlockSpec((kl, hps * NOPE_DIM), lambda b, h, i: (b, h)) for kl in k_lens]
    kr_specs = [pl.BlockSpec((kl, MLA_KR_PAD), lambda b, h, i: (b, 0)) for kl in k_lens]
    nk = sum(k_lens)
    return _call(body, name="mla_attention", grid=(grp.n_seq, n_hg, qt),
                 in_specs=[q_spec] * nseg + kv_specs + kr_specs + kv_specs,
                 out_specs=pl.BlockSpec((tq, hps * B_V_DIM), lambda b, h, i: (b * qt + i, h)),
                 out_shape=jax.ShapeDtypeStruct((m, B_HEADS * B_V_DIM), BF16),
                 args=list(qs) + list(kns) + list(krs) + list(vs),
                 temp_bytes=4 * _nbytes((tq, nk), F32) + 2 * _nbytes((nk, MLA_Q_PAD), BF16))


def _pool_mixer(h, x, mods, w_pool, layer, pool_scale, grp, k_gate):
    m, d = x.shape
    _, n_groups, c, _ = w_pool.shape
    assert max(POOL_WINDOWS) // 2 <= POOL_HALO and n_groups == len(POOL_WINDOWS)
    r = _tile(grp.seq, 512, POOL_HALO)
    nt = m // r
    halo_per_tile = r // POOL_HALO
    rows_per_cond = m // mods.shape[0]
    seq = grp.seq
    kc = r + 2 * POOL_HALO

    def body(hp_ref, hc_ref, hn_ref, w_ref, ps_ref, x_ref, m_ref, o_ref):
        g = pl.program_id(0)
        i = pl.program_id(1)
        back = functools.reduce(jnp.add, [jnp.where(g == k, w // 2, 0) for k, w in enumerate(POOL_WINDOWS)])
        fwd = functools.reduce(jnp.add, [jnp.where(g == k, w - w // 2, 0) for k, w in enumerate(POOL_WINDOWS)])
        seq_start = ((i * r) // seq) * seq
        row = i * r + lax.broadcasted_iota(jnp.int32, (r, kc), 0)
        col = i * r - POOL_HALO + lax.broadcasted_iota(jnp.int32, (r, kc), 1)
        lo = jnp.maximum(row - back, seq_start)
        hi = jnp.minimum(row + fwd, seq_start + seq)
        member = jnp.where((col >= lo) & (col < hi), 1.0, 0.0).astype(BF16)
        hc = hc_ref[...]
        hcat = jnp.concatenate([hp_ref[...], hc, hn_ref[...]], axis=0)
        win = _dot(member, hcat)
        row_c = i * r + lax.broadcasted_iota(jnp.int32, (r, c), 0)
        cnt = (jnp.minimum(row_c + fwd, seq_start + seq) - jnp.maximum(row_c - back, seq_start)).astype(F32)
        pooled = win / cnt - hc.astype(F32)
        y = _dot(pooled.astype(BF16), w_ref[...]) * ps_ref[...]
        gate = m_ref[...][k_gate:k_gate + 1]
        o_ref[...] = x_ref[...] + gate * y

    last_halo = m // POOL_HALO - 1
    in_specs = [
        pl.BlockSpec((POOL_HALO, c), lambda g, i: (jnp.maximum(i * halo_per_tile - 1, 0), g)),
        pl.BlockSpec((r, c), lambda g, i: (i, g)),
        pl.BlockSpec((POOL_HALO, c), lambda g, i: (jnp.minimum((i + 1) * halo_per_tile, last_halo), g)),
        pl.BlockSpec((None, None, c, c), lambda g, i: (layer, g, 0, 0)),
        pl.BlockSpec((1, c), lambda g, i: (0, g)),
        pl.BlockSpec((r, c), lambda g, i: (i, g)),
        pl.BlockSpec((None, N_MOD, c), lambda g, i: ((i * r) // rows_per_cond, 0, g)),
    ]
    return _call(body, name="pool_mixer", grid=(n_groups, nt), in_specs=in_specs,
                 out_specs=pl.BlockSpec((r, c), lambda g, i: (i, g)),
                 out_shape=jax.ShapeDtypeStruct((m, d), F32),
                 args=[h, h, h, w_pool, pool_scale.reshape(1, d), x, mods],
                 temp_bytes=4 * _nbytes((r, kc), F32) + 4 * _nbytes((r, c), F32))


def _ffn_up(h, w_up, conv_w, conv_b, layer, grp):
    m, d = h.shape
    f = w_up.shape[2] // 2
    tm = _tile(m, max(grp.seq, 2048), grp.seq)
    tn = _tile(f, 256, V7X_LANES)
    nj = f // tn
    seq = grp.seq

    rc = _tile(seq, 256)
    pad = 8
    rs = _tile(tm, 512, V7X_SUBLANES_BF16)

    def body(h_ref, wa_ref, wb_ref, cwa_ref, cwb_ref, cba_ref, cbb_ref, o_ref, acc_ref):
        row8 = lax.broadcasted_iota(jnp.int32, (8, tn), 0)

        def matmul(s):
            xs = h_ref[s * rs:(s + 1) * rs, :]
            for k, w_ref in enumerate((wa_ref, wb_ref)):
                acc_ref[k, pad + s * rs:pad + (s + 1) * rs] = _dot(xs, w_ref[...].astype(BF16))

        def piece(p0, p1):
            n = p1 - p0

            def conv(k, cw_ref, cb_ref):
                cw = cw_ref[...]
                u = acc_ref[k, pad + p0:pad + p1]
                prev = acc_ref[k, pad + p0 - 1:pad + p1 - 1]
                nxt = acc_ref[k, pad + p0 + 1:pad + p1 + 1]
                if p0 % seq == 0:
                    top = jnp.where(row8 == 0, 0.0, prev[:8])
                    prev = top if n == 8 else jnp.concatenate([top, prev[8:]], axis=0)
                if p1 % seq == 0:
                    bot = jnp.where(row8 == 7, 0.0, nxt[n - 8:])
                    nxt = bot if n == 8 else jnp.concatenate([nxt[:n - 8], bot], axis=0)
                return prev * cw[0:1] + u * cw[1:2] + nxt * cw[2:3] + cb_ref[...]

            a = conv(0, cwa_ref, cba_ref)
            b = conv(1, cwb_ref, cbb_ref)
            o_ref[p0:p1] = (_silu(a) * b).astype(o_ref.dtype)

        def epilogue(r0, r1):
            cuts = sorted({r0, r1} | {c for c in range(0, tm + 1, rc) if r0 < c < r1})
            for p0, p1 in zip(cuts[:-1], cuts[1:]):
                piece(p0, p1)

        n_stage = tm // rs
        for k in range(2):
            acc_ref[k, 0:pad] = jnp.zeros((pad, tn), F32)
            acc_ref[k, pad + tm:2 * pad + tm] = jnp.zeros((pad, tn), F32)
        for s in range(n_stage + 1):
            if s > 0:
                epilogue(max((s - 1) * rs - pad, 0), tm if s == n_stage else s * rs - pad)
            if s < n_stage:
                matmul(s)

    in_specs = [
        pl.BlockSpec((tm, d), lambda i, j: (i, 0), pipeline_mode=pl.Buffered(1)),
        pl.BlockSpec((None, d, tn), lambda i, j: (layer, 0, j)),
        pl.BlockSpec((None, d, tn), lambda i, j: (layer, 0, nj + j)),
        pl.BlockSpec((None, CONV_W, tn), lambda i, j: (layer, 0, j)),
        pl.BlockSpec((None, CONV_W, tn), lambda i, j: (layer, 0, nj + j)),
        pl.BlockSpec((None, 1, tn), lambda i, j: (layer, 0, j)),
        pl.BlockSpec((None, 1, tn), lambda i, j: (layer, 0, nj + j)),
    ]
    cb = conv_b.reshape(conv_b.shape[0], 1, 2 * f)
    return _call(body, name="ffn_up", grid=(m // tm, nj), in_specs=in_specs,
                 out_specs=pl.BlockSpec((tm, tn), lambda i, j: (i, j)),
                 out_shape=jax.ShapeDtypeStruct((m, f), BF16),
                 args=[h, w_up, w_up, conv_w, conv_w, cb, cb],
                 temp_bytes=10 * _nbytes((rc, tn), F32) + 4 * _nbytes((tm, tn), F32), single_buffered=(0,),
                 scratch=[((2, tm + 2 * pad, tn), F32)])


def _residual_proj(name, lhs, rhs, x, mods, k_gate, *, tm, tn):
    m = x.shape[0]
    rows_per_cond = m // mods.shape[0]
    assert rows_per_cond % tm == 0
    extras = [(x, (tm, tn), lambda i, j: (i, j)),
              (mods, (None, N_MOD, tn), lambda i, j: ((i * tm) // rows_per_cond, 0, j))]
    return _proj(name, lhs, rhs, _ep_residual(k_gate), (F32,), tm=tm, tn=tn, extras=extras)[0]


class _AttnWeights(NamedTuple):
    qa: _W
    ka: _W
    va: _W
    cq: _W
    ckv: _W
    kr: _W
    g_cq: jax.Array
    g_ckv: jax.Array
    q_up: _W
    kn: _W
    vb: _W
    o_a: _W
    o_b: _W
    lam_vecs: jax.Array


def _prep_attn_weights(j, w_in_b, w_in, g_cq, w_q_up, g_ckv, w_kv_up, w_o_b, lam_vecs):
    d = w_in.shape[0]
    a_q = A_HEADS * A_HEAD_DIM
    a_v = A_HEADS * A_V_DIM
    b_out = B_HEADS * B_V_DIM
    assert a_v == b_out
    q_lora = g_cq.shape[0]
    kv_lora = g_ckv.shape[0]
    o0, o1, o2 = a_q, 2 * a_q, 2 * a_q + a_v
    o3, o4 = o2 + q_lora, o2 + q_lora + kv_lora
    kr = jnp.pad(w_in[:, o4:], ((0, 0), (0, MLA_KR_PAD - ROPE_DIM)))
    q_up = w_q_up.reshape(q_lora, B_HEADS, NOPE_DIM + ROPE_DIM)
    q_up = jnp.pad(q_up, ((0, 0), (0, 0), (0, MLA_Q_PAD - NOPE_DIM - ROPE_DIM)))
    kv = w_kv_up.reshape(kv_lora, B_HEADS, NOPE_DIM + B_V_DIM)
    c = lambda w: _whole(w.astype(BF16))
    win = lambda col0, n: _W(w_in_b, j, d, 0, col0, n)
    return _AttnWeights(
        qa=win(0, a_q), ka=win(o0, a_q), va=win(o1, a_v), cq=win(o2, q_lora), ckv=win(o3, kv_lora),
        kr=c(kr), g_cq=g_cq.reshape(1, q_lora), g_ckv=g_ckv.reshape(1, kv_lora),
        q_up=c(q_up.reshape(q_lora, B_HEADS * MLA_Q_PAD)),
        kn=c(kv[:, :, :NOPE_DIM].reshape(kv_lora, B_HEADS * NOPE_DIM)),
        vb=c(kv[:, :, NOPE_DIM:].reshape(kv_lora, B_HEADS * B_V_DIM)),
        o_a=_W(w_o_b, j, a_v, 0, 0, w_o_b.shape[-1]), o_b=_W(w_o_b, j, b_out, 1, 0, w_o_b.shape[-1]),
        lam_vecs=lam_vecs)


def _attn_mixer(h, x, mods, w, grp, lam_init, cache):
    m, d = h.shape
    hps = A_HEADS if grp.seq <= 256 else 4
    qk_scale = A_QK_HALF ** -0.5
    mla_scale = (NOPE_DIM + ROPE_DIM) ** -0.5
    rows_unit = grp.seq if grp.rope else m

    def tiles(lhs, rhs, heavy_epilogue):
        k, n = rhs.k, rhs.n
        pref = 1024 if k < 1024 else (256 if heavy_epilogue else 512)
        return _tile(min(rows_unit, lhs.shape[0]), pref, V7X_SUBLANES_BF16), min(n, 2048)

    def plain(name, lhs, rhs, dtype, scale=1.0):
        tm_, tn_ = tiles(lhs, rhs, False)
        return _proj(name, [lhs], [rhs], _ep_scale(scale), (dtype,), tm=tm_, tn=tn_)[0]

    def roped(name, lhs, rhs, tables, dtype, scale, with_plain):
        tm_, tn_ = tiles(lhs, rhs, True)
        tiles_per_seq = grp.seq // tm_
        extras = [(t, (tm_, t.shape[1]), lambda i, j: (i % tiles_per_seq, 0)) for t in tables]
        outs = (dtype, dtype) if with_plain else (dtype,)
        return _proj(name, [lhs], [rhs], _ep_rope(scale, with_plain), outs, tm=tm_, tn=tn_, extras=extras)

    def normed(name, rhs, gain, dtype):
        tm_, _ = tiles(h, rhs, False)
        n = rhs.n
        return _proj(name, [h], [rhs], _ep_rmsnorm, (dtype,), tm=tm_, tn=n,
                     extras=[(gain, (1, n), lambda i, j: (0, 0))])[0]

    cqn = normed("cq_norm", w.cq, w.g_cq, BF16)
    ckvn = normed("ckv_norm", w.ckv, w.g_ckv, F32)
    kn = plain("k_nope", ckvn, w.kn, BF16)
    vb = plain("v_mla", ckvn, w.vb, BF16)

    if not grp.rope:
        qa = plain("q_diff", h, w.qa, BF16, qk_scale)
        ka = plain("k_diff", h, w.ka, F32)
        va = plain("v_diff", h, w.va, F32)
        kr = plain("k_rope", h, w.kr, F32)
        q = plain("q_mla", cqn, w.q_up, BF16, mla_scale)
        oa = _diff_attention(w.lam_vecs, lam_init, [qa], [ka], [va], grp, hps)
        ob = _mla_attention([q], [kn], [kr], [vb], grp, hps)
        state = (ka, va, ckvn, kr[:, :ROPE_DIM])
    else:
        cos, sin = _rope_tables(grp.seq)
        t_diff = _widen_tables(cos, sin, 0, 0, A_HEAD_DIM)
        t_mla = _widen_tables(cos, sin, NOPE_DIM, MLA_Q_PAD - NOPE_DIM - ROPE_DIM, MLA_Q_PAD)
        t_kr = _widen_tables(cos, sin, 0, MLA_KR_PAD - ROPE_DIM, MLA_KR_PAD)
        ka_c, va_c, ckv_c, kr_c = cache
        qa_u, qa_r = roped("q_diff", h, w.qa, t_diff, BF16, qk_scale, True)
        ka_r, = roped("k_diff", h, w.ka, t_diff, BF16, 1.0, False)
        va = plain("v_diff", h, w.va, BF16)
        kr_r, = roped("k_rope", h, w.kr, t_kr, BF16, 1.0, False)
        q_u, q_r = roped("q_mla", cqn, w.q_up, t_mla, BF16, mla_scale, True)
        kn_c = plain("k_nope_ctx", ckv_c, w.kn, BF16)
        vb_c = plain("v_mla_ctx", ckv_c, w.vb, BF16)
        oa = _diff_attention(w.lam_vecs, lam_init, [qa_u, qa_r], [ka_c, ka_r], [va_c, va], grp, hps)
        ob = _mla_attention([q_u, q_r], [kn_c, kn], [kr_c, kr_r], [vb_c, vb], grp, hps)
        state = None
    x1 = _residual_proj("o_proj", [oa, ob], [w.o_a, w.o_b], x, mods, 2,
                        tm=_tile(rows_unit, 1024, V7X_SUBLANES_BF16), tn=512)
    return x1, state


def _conv_ffn(x, mods, g2, w_up, conv_w, conv_b, w_down, layer, grp):
    h = _norm(x, g2, mods, 3, 4, BF16)
    act = _ffn_up(h, w_up, conv_w, conv_b, layer, grp)
    w_dn = _W(w_down, layer, w_down.shape[1], 0, 0, w_down.shape[2])
    return _residual_proj("ffn_down", [act], [w_dn], x, mods, 5,
                          tm=_tile(x.shape[0] // mods.shape[0], 512, V7X_SUBLANES_BF16), tn=512)


def kernel(x_prompt, x_sample, cache_diff_k, cache_diff_v, cache_mla_ckv, cache_mla_krope, c, c_ctx,
           norm1_g, norm2_g, w_mod, b_mod, w_in, g_cq, w_q_up, g_ckv, w_kv_up,
           lambda_q1, lambda_k1, lambda_q2, lambda_k2, w_o, w_pool, pool_scale,
           w_up, conv_w, conv_b, w_down, g_final):
    batch, seq, d = x_prompt.shape
    dec_batch, dec_seq, _ = x_sample.shape
    depth = w_mod.shape[0]
    past = cache_diff_k.shape[2]
    groups = (_Group(batch, seq, False), _Group(dec_batch, dec_seq, True))

    n_cond = 1 + dec_batch
    cond8 = jnp.zeros((8, d), F32).at[0].set(c_ctx).at[1:n_cond].set(c)
    mods_all = _adaln(cond8, w_mod, b_mod).reshape(depth, 8, N_MOD, d)
    mods_g = (mods_all[:, 0:1], mods_all[:, 1:n_cond])

    w_in_b = w_in.astype(BF16)
    w_o_b = w_o.astype(BF16)
    attn_w = {}
    for i in range(0, depth, 2):
        j = i // 2
        lam_vecs = jnp.stack([lambda_q1[j], lambda_k1[j], lambda_q2[j], lambda_k2[j]])
        attn_w[i] = _prep_attn_weights(j, w_in_b, w_in[j], g_cq[j], w_q_up[j], g_ckv[j], w_kv_up[j],
                                       w_o_b, lam_vecs)
    w_pool_b = w_pool.astype(BF16)
    w_down_b = w_down.astype(BF16)

    xs = [x_prompt.reshape(batch * seq, d), x_sample.reshape(dec_batch * dec_seq, d)]
    states = []
    for gi, grp in enumerate(groups):
        x = xs[gi]
        for i in range(depth):
            mods = mods_g[gi][i]
            h = _norm(x, norm1_g[i], mods, 0, 1, BF16)
            if i % 2 == 0:
                j = i // 2
                lam_init = 0.8 - 0.6 * math.exp(-0.3 * i)
                cache = None
                if grp.rope:
                    kr_c = jnp.pad(cache_mla_krope[:, j].reshape(dec_batch * past, ROPE_DIM),
                                   ((0, 0), (0, MLA_KR_PAD - ROPE_DIM)))
                    cache = (cache_diff_k[:, j].reshape(dec_batch * past, A_HEADS * A_HEAD_DIM),
                             cache_diff_v[:, j].reshape(dec_batch * past, A_HEADS * A_V_DIM),
                             cache_mla_ckv[:, j].reshape(dec_batch * past, -1), kr_c)
                x, st = _attn_mixer(h, x, mods, attn_w[i], grp, lam_init, cache)
                if st is not None:
                    states.append(st)
            else:
                x = _pool_mixer(h, x, mods, w_pool_b, i // 2, pool_scale[i // 2], grp, 2)
            x = _conv_ffn(x, mods, norm2_g[i], w_up, conv_w, conv_b, w_down_b, i, grp)
        xs[gi] = _norm(x, g_final, None, 0, 0, F32)

    y_prompt = xs[0].reshape(batch, seq, d)
    y_sample = xs[1].reshape(dec_batch, dec_seq, d)
    stack = lambda k, tail: jnp.stack([s[k].reshape((batch, seq) + tail) for s in states], axis=1)
    new_diff_k = stack(0, (A_HEADS, A_HEAD_DIM))
    new_diff_v = stack(1, (A_HEADS, A_V_DIM))
    new_mla_ckv = stack(2, (g_ckv.shape[-1],))
    new_mla_krope = stack(3, (ROPE_DIM,))
    return (y_prompt, y_sample, new_diff_k, new_diff_v, new_mla_ckv, new_mla_krope)
```

```python
import functools
import math
from typing import NamedTuple

import jax
import jax.numpy as jnp
from jax import lax
from jax.experimental import pallas as pl
from jax.experimental.pallas import tpu as pltpu

F32 = jnp.float32
BF16 = jnp.bfloat16

GRID_W = 64
ROPE_BASE = 10000.0
EPS = 1e-6
A_HEADS = 16
A_QK_HALF = 64
A_HEAD_DIM = 2 * A_QK_HALF
A_V_DIM = 128
B_HEADS = 16
NOPE_DIM = 128
ROPE_DIM = 64
B_V_DIM = 128
POOL_WINDOWS = (2, 4, 8, 16)
N_MOD = 6
CONV_W = 3

V7X_VMEM_BYTES = 64 * 1024 * 1024
V7X_LANES = 128
V7X_SUBLANES_BF16 = 16
COMPILER_SCRATCH_BYTES = 2 * 1024 * 1024

MLA_Q_PAD = 256
MLA_KR_PAD = MLA_Q_PAD - NOPE_DIM
ROPE_QUARTER = ROPE_DIM // 4
POOL_HALO = V7X_SUBLANES_BF16


class _Group(NamedTuple):
    n_seq: int
    seq: int
    rope: bool


def _tile(n, pref, mult=8):
    if n <= pref:
        return n
    t = (pref // mult) * mult
    while t >= mult:
        if n % t == 0:
            return t
        t -= mult
    return n


def _nbytes(shape, dtype):
    return math.prod(shape) * jnp.dtype(dtype).itemsize


def _call(body, *, name, grid, in_specs, out_specs, out_shape, args, temp_bytes=0, single_buffered=(),
          scratch=(), semantics=None, flags=None):
    multi = isinstance(out_shape, (list, tuple))
    outs = list(out_shape) if multi else [out_shape]
    ospecs = list(out_specs) if multi else [out_specs]
    total = temp_bytes + COMPILER_SCRATCH_BYTES + sum(_nbytes(s, dt) for s, dt in scratch)
    for k, (a, s) in enumerate(zip(args, in_specs)):
        blk = [1 if b is None else b for b in s.block_shape]
        total += _nbytes(blk, a.dtype) * (1 if k in single_buffered else 2)
    for o, s in zip(outs, ospecs):
        blk = [1 if b is None else b for b in s.block_shape]
        total += 2 * _nbytes(blk, o.dtype)
    limit = min(V7X_VMEM_BYTES - 4 * 1024 * 1024, max(total, 16 * 1024 * 1024))
    return pl.pallas_call(
        body, name=name, grid=grid, in_specs=in_specs, out_specs=out_specs, out_shape=out_shape,
        scratch_shapes=[pltpu.VMEM(s, dt) for s, dt in scratch],
        compiler_params=pltpu.CompilerParams(
            dimension_semantics=semantics or ("parallel",) * len(grid), vmem_limit_bytes=int(limit),
            flags=flags),
    )(*args)


def _dot(a, b):
    return jnp.dot(a, b, preferred_element_type=F32)


def _dot_nt(a, b):
    return lax.dot_general(a, b, (((1,), (1,)), ((), ())), preferred_element_type=F32)


def _silu(x):
    return x / (1.0 + jnp.exp(-x))


def _adaln(cond8, w_mod, b_mod):
    depth, d, n = w_mod.shape
    tn = next(t for t in (512, 256, 128) if n % (2 * t) == 0)

    def body(c_ref, wa_ref, wb_ref, b_ref, o_ref):
        s = _silu(c_ref[...]).astype(BF16)
        ya = _dot(s, wa_ref[...].astype(BF16))
        yb = _dot(s, wb_ref[...].astype(BF16))
        o_ref[...] = jnp.concatenate([ya, yb], axis=1) + b_ref[...]

    return _call(
        body, name="adaln", grid=(depth, n // (2 * tn)),
        in_specs=[pl.BlockSpec((8, d), lambda l, j: (0, 0)),
                  pl.BlockSpec((None, d, tn), lambda l, j: (l, 0, 2 * j)),
                  pl.BlockSpec((None, d, tn), lambda l, j: (l, 0, 2 * j + 1)),
                  pl.BlockSpec((None, 1, 2 * tn), lambda l, j: (l, 0, j))],
        out_specs=pl.BlockSpec((None, 8, 2 * tn), lambda l, j: (l, 0, j)),
        out_shape=jax.ShapeDtypeStruct((depth, 8, n), F32),
        args=(cond8, w_mod, w_mod, b_mod.reshape(depth, 1, n)),
        temp_bytes=2 * _nbytes((d, tn), BF16) + 2 * _nbytes((d, tn), F32))


def _norm(x, g, mods, k_shift, k_scale, out_dtype):
    m, d = x.shape
    modulated = mods is not None
    tm = _tile(m // mods.shape[0] if modulated else m, 512)

    def body(*refs):
        x_ref, g_ref = refs[0], refs[1]
        o_ref = refs[-1]
        xf = x_ref[...]
        y = xf * lax.rsqrt(jnp.mean(xf * xf, axis=-1, keepdims=True) + EPS) * g_ref[...]
        if modulated:
            mm = refs[2][...]
            y = y * (1.0 + mm[k_scale:k_scale + 1]) + mm[k_shift:k_shift + 1]
        o_ref[...] = y.astype(o_ref.dtype)

    in_specs = [pl.BlockSpec((tm, d), lambda i: (i, 0)), pl.BlockSpec((1, d), lambda i: (0, 0))]
    args = [x, g.reshape(1, d)]
    if modulated:
        rows_per_cond = m // mods.shape[0]
        in_specs.append(pl.BlockSpec((None, N_MOD, d), lambda i: ((i * tm) // rows_per_cond, 0, 0)))
        args.append(mods)
    return _call(body, name="norm", grid=(m // tm,), in_specs=in_specs,
                 out_specs=pl.BlockSpec((tm, d), lambda i: (i, 0)),
                 out_shape=jax.ShapeDtypeStruct((m, d), out_dtype), args=args,
                 temp_bytes=3 * _nbytes((tm, d), F32))


class _W(NamedTuple):
    arr: jax.Array
    layer: object
    k: int
    row_blk: int
    col0: int
    n: int

    def spec(self, tn, col=lambda i, j: j, **mode):
        assert self.col0 % tn == 0 and self.n % tn == 0
        c0 = self.col0 // tn
        if self.layer is None:
            return pl.BlockSpec((self.k, tn), lambda i, j: (self.row_blk, c0 + col(i, j)), **mode)
        return pl.BlockSpec((None, self.k, tn),
                            lambda i, j: (self.layer, self.row_blk, c0 + col(i, j)), **mode)


def _whole(arr):
    return _W(arr, None, arr.shape[0], 0, 0, arr.shape[1])


def _proj(name, lhs, rhs, epilogue, outs, *, tm, tn, extras=()):
    m = lhs[0].shape[0]
    n = rhs[0].n
    np_ = len(lhs)
    ne = len(extras)

    def body(*refs):
        acc = None
        for p in range(np_):
            part = _dot(refs[p][...].astype(BF16), refs[np_ + p][...].astype(BF16))
            acc = part if acc is None else acc + part
        epilogue(acc, refs[2 * np_:2 * np_ + ne], refs[2 * np_ + ne:])

    resident = n == tn
    rhs_mode = dict(pipeline_mode=pl.Buffered(1)) if resident else {}
    nj = n // tn
    col = (lambda i, j: j) if nj == 1 else (lambda i, j: jnp.where(i % 2 == 0, j, nj - 1 - j))
    in_specs = ([pl.BlockSpec((tm, a.shape[1]), lambda i, j: (i, 0)) for a in lhs]
                + [w.spec(tn, col, **rhs_mode) for w in rhs]
                + [pl.BlockSpec(bs, lambda i, j, im=im: im(i, col(i, j))) for (_, bs, im) in extras])
    args = list(lhs) + [w.arr for w in rhs] + [e[0] for e in extras]
    out_shape = [jax.ShapeDtypeStruct((m, n), dt) for dt in outs]
    out_specs = [pl.BlockSpec((tm, tn), lambda i, j: (i, col(i, j))) for _ in outs]
    return _call(body, name=name, grid=(m // tm, nj), in_specs=in_specs, out_specs=out_specs,
                 out_shape=out_shape, args=args, temp_bytes=6 * _nbytes((tm, tn), F32),
                 single_buffered=tuple(range(np_, 2 * np_)) if resident else ())


def _ep_scale(scale):
    def ep(acc, ex, outs):
        outs[0][...] = (acc * scale).astype(outs[0].dtype)
    return ep


def _rotate(a, cos, sin):
    width = a.shape[1]
    lane = lax.broadcasted_iota(jnp.int32, a.shape, 1)
    low = (lane & (2 * ROPE_QUARTER - 1)) < ROPE_QUARTER
    partner = jnp.where(low, pltpu.roll(a, width - ROPE_QUARTER, 1), pltpu.roll(a, ROPE_QUARTER, 1))
    return a * cos + partner * sin


def _ep_rope(scale, with_plain):
    def ep(acc, ex, outs):
        a = acc * scale
        reps = a.shape[1] // ex[0].shape[1]
        cos = jnp.concatenate([ex[0][...]] * reps, axis=1)
        sin = jnp.concatenate([ex[1][...]] * reps, axis=1)
        r = _rotate(a, cos, sin)
        if with_plain:
            outs[0][...] = a.astype(outs[0].dtype)
        outs[-1][...] = r.astype(outs[-1].dtype)
    return ep


def _ep_rmsnorm(acc, ex, outs):
    y = acc * lax.rsqrt(jnp.mean(acc * acc, axis=-1, keepdims=True) + EPS) * ex[0][...]
    outs[0][...] = y.astype(outs[0].dtype)


def _ep_residual(k_gate):
    def ep(acc, ex, outs):
        gate = ex[1][...][k_gate:k_gate + 1]
        outs[0][...] = ex[0][...] + gate * acc
    return ep


def _rope_tables(seq):
    half = ROPE_DIM // 2
    inv = ROPE_BASE ** (-jnp.arange(0, half, 2, dtype=F32) / half)
    pos = jnp.arange(seq)
    ar = (pos // GRID_W).astype(F32)[:, None] * inv[None, :]
    ac = (pos % GRID_W).astype(F32)[:, None] * inv[None, :]
    cos = jnp.concatenate([jnp.cos(ar), jnp.cos(ar), jnp.cos(ac), jnp.cos(ac)], axis=1)
    sin = jnp.concatenate([-jnp.sin(ar), jnp.sin(ar), -jnp.sin(ac), jnp.sin(ac)], axis=1)
    return cos, sin


def _widen_tables(cos, sin, lead, trail, width):
    seq = cos.shape[0]
    c = jnp.concatenate([jnp.ones((seq, lead), F32), cos, jnp.ones((seq, trail), F32)], axis=1)
    s = jnp.concatenate([jnp.zeros((seq, lead), F32), sin, jnp.zeros((seq, trail), F32)], axis=1)
    reps = width // c.shape[1]
    assert reps * c.shape[1] == width
    return jnp.tile(c, (1, reps)), jnp.tile(s, (1, reps))


def _diff_attention(lam_vecs, lam_init, qs, ks, vs, grp, hps):
    m = qs[0].shape[0]
    nseg = len(qs)
    tq = _tile(grp.seq, 256)
    qt = grp.seq // tq
    width = hps * A_HEAD_DIM
    n_hg = A_HEADS // hps
    k_lens = [k.shape[0] // grp.n_seq for k in ks]

    def body(*refs):
        lam_ref = refs[0]
        q_refs = refs[1:1 + nseg]
        k_refs = refs[1 + nseg:1 + 2 * nseg]
        v_refs = refs[1 + 2 * nseg:1 + 3 * nseg]
        o_ref = refs[-1]
        lv = lam_ref[...]
        lam = (jnp.exp(jnp.sum(lv[0:1] * lv[1:2], axis=-1, keepdims=True))
               - jnp.exp(jnp.sum(lv[2:3] * lv[3:4], axis=-1, keepdims=True)) + lam_init)
        first = lax.broadcasted_iota(jnp.int32, (tq, A_HEAD_DIM), 1) < A_QK_HALF
        for g in range(hps):
            cols = slice(g * A_HEAD_DIM, (g + 1) * A_HEAD_DIM)
            scores = []
            for q_ref, k_ref in zip(q_refs, k_refs):
                q = q_ref[:, cols]
                zero = jnp.zeros_like(q)
                q2 = jnp.concatenate([jnp.where(first, q, zero), jnp.where(first, zero, q)], axis=0)
                scores.append(_dot_nt(q2, k_ref[:, cols].astype(BF16)))
            mx = functools.reduce(jnp.maximum, [jnp.max(s, axis=-1, keepdims=True) for s in scores])
            ps = [jnp.exp(s - mx) for s in scores]
            den = functools.reduce(jnp.add, [jnp.sum(p, axis=-1, keepdims=True) for p in ps])
            o2 = functools.reduce(
                jnp.add, [_dot(p.astype(BF16), v_ref[:, cols].astype(BF16)) for p, v_ref in zip(ps, v_refs)])
            o2 = o2 / den
            o = o2[:tq] - lam * o2[tq:]
            o = o * lax.rsqrt(jnp.mean(o * o, axis=-1, keepdims=True) + EPS) * (1.0 - lam_init)
            o_ref[:, cols] = o.astype(o_ref.dtype)

    q_spec = pl.BlockSpec((tq, width), lambda b, h, i: (b * qt + i, h))
    in_specs = ([pl.BlockSpec(lam_vecs.shape, lambda b, h, i: (0, 0))]
                + [q_spec] * nseg
                + [pl.BlockSpec((kl, width), lambda b, h, i: (b, h)) for kl in k_lens] * 2)
    nk = sum(k_lens)
    return _call(body, name="diff_attention", grid=(grp.n_seq, n_hg, qt), in_specs=in_specs,
                 out_specs=q_spec, out_shape=jax.ShapeDtypeStruct((m, A_HEADS * A_V_DIM), BF16),
                 args=[lam_vecs] + list(qs) + list(ks) + list(vs),
                 temp_bytes=4 * _nbytes((2 * tq, nk), F32))


def _mla_attention(qs, kns, krs, vs, grp, hps):
    m = qs[0].shape[0]
    nseg = len(qs)
    tq = _tile(grp.seq, 256)
    qt = grp.seq // tq
    n_hg = B_HEADS // hps
    k_lens = [k.shape[0] // grp.n_seq for k in kns]

    def body(*refs):
        q_refs = refs[:nseg]
        kn_refs = refs[nseg:2 * nseg]
        kr_refs = refs[2 * nseg:3 * nseg]
        v_refs = refs[3 * nseg:4 * nseg]
        o_ref = refs[-1]
        krs_v = [r[...].astype(BF16) for r in kr_refs]
        for g in range(hps):
            kcols = slice(g * NOPE_DIM, (g + 1) * NOPE_DIM)
            vcols = slice(g * B_V_DIM, (g + 1) * B_V_DIM)
            scores = []
            for q_ref, kn_ref, kr in zip(q_refs, kn_refs, krs_v):
                q = q_ref[:, g * MLA_Q_PAD:(g + 1) * MLA_Q_PAD]
                k = jnp.concatenate([kn_ref[:, kcols], kr], axis=1)
                scores.append(_dot_nt(q, k))
            mx = functools.reduce(jnp.maximum, [jnp.max(s, axis=-1, keepdims=True) for s in scores])
            ps = [jnp.exp(s - mx) for s in scores]
            den = functools.reduce(jnp.add, [jnp.sum(p, axis=-1, keepdims=True) for p in ps])
            o = functools.reduce(
                jnp.add, [_dot(p.astype(BF16), v_ref[:, vcols]) for p, v_ref in zip(ps, v_refs)])
            o_ref[:, vcols] = (o / den).astype(o_ref.dtype)

    q_spec = pl.BlockSpec((tq, hps * MLA_Q_PAD), lambda b, h, i: (b * qt + i, h))
    kv_specs = [pl.BlockSpec((kl, hps * NOPE_DIM), lambda b, h, i: (b, h)) for kl in k_lens]
    kr_specs = [pl.BlockSpec((kl, MLA_KR_PAD), lambda b, h, i: (b, 0)) for kl in k_lens]
    nk = sum(k_lens)
    return _call(body, name="mla_attention", grid=(grp.n_seq, n_hg, qt),
                 in_specs=[q_spec] * nseg + kv_specs + kr_specs + kv_specs,
                 out_specs=pl.BlockSpec((tq, hps * B_V_DIM), lambda b, h, i: (b * qt + i, h)),
                 out_shape=jax.ShapeDtypeStruct((m, B_HEADS * B_V_DIM), BF16),
                 args=list(qs) + list(kns) + list(krs) + list(vs),
                 temp_bytes=4 * _nbytes((tq, nk), F32) + 2 * _nbytes((nk, MLA_Q_PAD), BF16))


def _pool_mixer(h, x, mods, w_pool, layer, pool_scale, grp, k_gate):
    m, d = x.shape
    _, n_groups, c, _ = w_pool.shape
    assert max(POOL_WINDOWS) // 2 <= POOL_HALO and n_groups == len(POOL_WINDOWS)
    r = _tile(grp.seq, 512, POOL_HALO)
    nt = m // r
    halo_per_tile = r // POOL_HALO
    rows_per_cond = m // mods.shape[0]
    seq = grp.seq
    kc = r + 2 * POOL_HALO

    def body(hp_ref, hc_ref, hn_ref, w_ref, ps_ref, x_ref, m_ref, o_ref):
        g = pl.program_id(0)
        i = pl.program_id(1)
        back = functools.reduce(jnp.add, [jnp.where(g == k, w // 2, 0) for k, w in enumerate(POOL_WINDOWS)])
        fwd = functools.reduce(jnp.add, [jnp.where(g == k, w - w // 2, 0) for k, w in enumerate(POOL_WINDOWS)])
        seq_start = ((i * r) // seq) * seq
        row = i * r + lax.broadcasted_iota(jnp.int32, (r, kc), 0)
        col = i * r - POOL_HALO + lax.broadcasted_iota(jnp.int32, (r, kc), 1)
        lo = jnp.maximum(row - back, seq_start)
        hi = jnp.minimum(row + fwd, seq_start + seq)
        member = jnp.where((col >= lo) & (col < hi), 1.0, 0.0).astype(BF16)
        hc = hc_ref[...]
        hcat = jnp.concatenate([hp_ref[...], hc, hn_ref[...]], axis=0)
        win = _dot(member, hcat)
        row_c = i * r + lax.broadcasted_iota(jnp.int32, (r, c), 0)
        cnt = (jnp.minimum(row_c + fwd, seq_start + seq) - jnp.maximum(row_c - back, seq_start)).astype(F32)
        pooled = win / cnt - hc.astype(F32)
        y = _dot(pooled.astype(BF16), w_ref[...]) * ps_ref[...]
        gate = m_ref[...][k_gate:k_gate + 1]
        o_ref[...] = x_ref[...] + gate * y

    last_halo = m // POOL_HALO - 1
    in_specs = [
        pl.BlockSpec((POOL_HALO, c), lambda g, i: (jnp.maximum(i * halo_per_tile - 1, 0), g)),
        pl.BlockSpec((r, c), lambda g, i: (i, g)),
        pl.BlockSpec((POOL_HALO, c), lambda g, i: (jnp.minimum((i + 1) * halo_per_tile, last_halo), g)),
        pl.BlockSpec((None, None, c, c), lambda g, i: (layer, g, 0, 0)),
        pl.BlockSpec((1, c), lambda g, i: (0, g)),
        pl.BlockSpec((r, c), lambda g, i: (i, g)),
        pl.BlockSpec((None, N_MOD, c), lambda g, i: ((i * r) // rows_per_cond, 0, g)),
    ]
    return _call(body, name="pool_mixer", grid=(n_groups, nt), in_specs=in_specs,
                 out_specs=pl.BlockSpec((r, c), lambda g, i: (i, g)),
                 out_shape=jax.ShapeDtypeStruct((m, d), F32),
                 args=[h, h, h, w_pool, pool_scale.reshape(1, d), x, mods],
                 temp_bytes=4 * _nbytes((r, kc), F32) + 4 * _nbytes((r, c), F32))


def _ffn_up(h, w_up, conv_w, conv_b, layer, grp, w_down=None):
    m, d = h.shape
    f = w_up.shape[2] // 2
    tm = _tile(m, max(grp.seq, 2048), grp.seq)
    tn = _tile(f, 256, V7X_LANES)
    nj = f // tn
    seq = grp.seq
    steps = (m // tm) * nj
    fuse_cast = w_down is not None and w_down.shape[1] % (steps * V7X_SUBLANES_BF16) == 0
    slab_rows = w_down.shape[1] // steps if fuse_cast else 0

    rc = _tile(seq, 256)
    pad = 8
    rs = _tile(tm, 512, V7X_SUBLANES_BF16)

    def body(*refs):
        h_ref, wa_ref, wb_ref, cwa_ref, cwb_ref, cba_ref, cbb_ref = refs[:7]
        if fuse_cast:
            wd_ref, o_ref, wdo_ref, acc_ref = refs[7:]
            wdo_ref[...] = wd_ref[...].astype(BF16)
        else:
            o_ref, acc_ref = refs[7:]
        row8 = lax.broadcasted_iota(jnp.int32, (8, tn), 0)

        def matmul(s):
            xs = h_ref[s * rs:(s + 1) * rs, :]
            for k, w_ref in enumerate((wa_ref, wb_ref)):
                acc_ref[k, pad + s * rs:pad + (s + 1) * rs] = _dot(xs, w_ref[...].astype(BF16))

        def piece(p0, p1):
            n = p1 - p0

            def conv(k, cw_ref, cb_ref):
                cw = cw_ref[...]
                u = acc_ref[k, pad + p0:pad + p1]
                prev = acc_ref[k, pad + p0 - 1:pad + p1 - 1]
                nxt = acc_ref[k, pad + p0 + 1:pad + p1 + 1]
                if p0 % seq == 0:
                    top = jnp.where(row8 == 0, 0.0, prev[:8])
                    prev = top if n == 8 else jnp.concatenate([top, prev[8:]], axis=0)
                if p1 % seq == 0:
                    bot = jnp.where(row8 == 7, 0.0, nxt[n - 8:])
                    nxt = bot if n == 8 else jnp.concatenate([nxt[:n - 8], bot], axis=0)
                return prev * cw[0:1] + u * cw[1:2] + nxt * cw[2:3] + cb_ref[...]

            a = conv(0, cwa_ref, cba_ref)
            b = conv(1, cwb_ref, cbb_ref)
            o_ref[p0:p1] = (_silu(a) * b).astype(o_ref.dtype)

        def epilogue(r0, r1):
            cuts = sorted({r0, r1} | {c for c in range(0, tm + 1, rc) if r0 < c < r1})
            for p0, p1 in zip(cuts[:-1], cuts[1:]):
                piece(p0, p1)

        n_stage = tm // rs
        for k in range(2):
            acc_ref[k, 0:pad] = jnp.zeros((pad, tn), F32)
            acc_ref[k, pad + tm:2 * pad + tm] = jnp.zeros((pad, tn), F32)
        for s in range(n_stage + 1):
            if s > 0:
                epilogue(max((s - 1) * rs - pad, 0), tm if s == n_stage else s * rs - pad)
            if s < n_stage:
                matmul(s)

    col = lambda i, j: jnp.where(i % 2 == 0, j, nj - 1 - j)
    in_specs = [
        pl.BlockSpec((tm, d), lambda i, j: (i, 0)),
        pl.BlockSpec((None, d, tn), lambda i, j: (layer, 0, col(i, j))),
        pl.BlockSpec((None, d, tn), lambda i, j: (layer, 0, nj + col(i, j))),
        pl.BlockSpec((None, CONV_W, tn), lambda i, j: (layer, 0, col(i, j))),
        pl.BlockSpec((None, CONV_W, tn), lambda i, j: (layer, 0, nj + col(i, j))),
        pl.BlockSpec((None, 1, tn), lambda i, j: (layer, 0, col(i, j))),
        pl.BlockSpec((None, 1, tn), lambda i, j: (layer, 0, nj + col(i, j))),
    ]
    cb = conv_b.reshape(conv_b.shape[0], 1, 2 * f)
    args = [h, w_up, w_up, conv_w, conv_w, cb, cb]
    out_specs = [pl.BlockSpec((tm, tn), lambda i, j: (i, col(i, j)))]
    out_shape = [jax.ShapeDtypeStruct((m, f), BF16)]
    if fuse_cast:
        slab = (slab_rows, w_down.shape[2])
        in_specs.append(pl.BlockSpec((None,) + slab, lambda i, j: (layer, i * nj + j, 0)))
        args.append(w_down)
        out_specs.append(pl.BlockSpec(slab, lambda i, j: (i * nj + j, 0)))
        out_shape.append(jax.ShapeDtypeStruct(w_down.shape[1:], BF16))
    res = _call(body, name="ffn_up", grid=(m // tm, nj), in_specs=in_specs, out_specs=out_specs,
                out_shape=out_shape, args=args,
                temp_bytes=10 * _nbytes((rc, tn), F32) + 4 * _nbytes((rs, tn), F32) + 2 * _nbytes((d, tn), BF16),
                scratch=[((2, tm + 2 * pad, tn), F32)])
    return res[0], (res[1] if fuse_cast else None)


def _residual_proj(name, lhs, rhs, x, mods, k_gate, *, tm, tn):
    m = x.shape[0]
    rows_per_cond = m // mods.shape[0]
    assert rows_per_cond % tm == 0
    extras = [(x, (tm, tn), lambda i, j: (i, j)),
              (mods, (None, N_MOD, tn), lambda i, j: ((i * tm) // rows_per_cond, 0, j))]
    return _proj(name, lhs, rhs, _ep_residual(k_gate), (F32,), tm=tm, tn=tn, extras=extras)[0]


class _AttnWeights(NamedTuple):
    qa: _W
    ka: _W
    va: _W
    cq: _W
    ckv: _W
    kr: _W
    g_cq: jax.Array
    g_ckv: jax.Array
    q_up: _W
    kn: _W
    vb: _W
    o_a: _W
    o_b: _W
    lam_vecs: jax.Array


def _prep_attn_weights(j, w_in_b, w_in, g_cq, w_q_up, g_ckv, w_kv_up, w_o_b, lam_vecs):
    d = w_in.shape[0]
    a_q = A_HEADS * A_HEAD_DIM
    a_v = A_HEADS * A_V_DIM
    b_out = B_HEADS * B_V_DIM
    assert a_v == b_out
    q_lora = g_cq.shape[0]
    kv_lora = g_ckv.shape[0]
    o0, o1, o2 = a_q, 2 * a_q, 2 * a_q + a_v
    o3, o4 = o2 + q_lora, o2 + q_lora + kv_lora
    kr = jnp.pad(w_in[:, o4:], ((0, 0), (0, MLA_KR_PAD - ROPE_DIM)))
    q_up = w_q_up.reshape(q_lora, B_HEADS, NOPE_DIM + ROPE_DIM)
    q_up = jnp.pad(q_up, ((0, 0), (0, 0), (0, MLA_Q_PAD - NOPE_DIM - ROPE_DIM)))
    kv = w_kv_up.reshape(kv_lora, B_HEADS, NOPE_DIM + B_V_DIM)
    c = lambda w: _whole(w.astype(BF16))
    win = lambda col0, n: _W(w_in_b, j, d, 0, col0, n)
    return _AttnWeights(
        qa=win(0, a_q), ka=win(o0, a_q), va=win(o1, a_v), cq=win(o2, q_lora), ckv=win(o3, kv_lora),
        kr=c(kr), g_cq=g_cq.reshape(1, q_lora), g_ckv=g_ckv.reshape(1, kv_lora),
        q_up=c(q_up.reshape(q_lora, B_HEADS * MLA_Q_PAD)),
        kn=c(kv[:, :, :NOPE_DIM].reshape(kv_lora, B_HEADS * NOPE_DIM)),
        vb=c(kv[:, :, NOPE_DIM:].reshape(kv_lora, B_HEADS * B_V_DIM)),
        o_a=_W(w_o_b, j, a_v, 0, 0, w_o_b.shape[-1]), o_b=_W(w_o_b, j, b_out, 1, 0, w_o_b.shape[-1]),
        lam_vecs=lam_vecs)


def _attn_mixer(h, x, mods, w, grp, lam_init, cache):
    m, d = h.shape
    hps = A_HEADS if grp.seq <= 256 else 4
    qk_scale = A_QK_HALF ** -0.5
    mla_scale = (NOPE_DIM + ROPE_DIM) ** -0.5
    rows_unit = grp.seq if grp.rope else m

    def tiles(lhs, rhs, heavy_epilogue):
        k, n = rhs.k, rhs.n
        pref = 1024 if k < 1024 else (256 if heavy_epilogue else 512)
        return _tile(min(rows_unit, lhs.shape[0]), pref, V7X_SUBLANES_BF16), min(n, 2048)

    def plain(name, lhs, rhs, dtype, scale=1.0):
        tm_, tn_ = tiles(lhs, rhs, False)
        return _proj(name, [lhs], [rhs], _ep_scale(scale), (dtype,), tm=tm_, tn=tn_)[0]

    def roped(name, lhs, rhs, tables, dtype, scale, with_plain):
        tm_, tn_ = tiles(lhs, rhs, True)
        tiles_per_seq = grp.seq // tm_
        extras = [(t, (tm_, t.shape[1]), lambda i, j: (i % tiles_per_seq, 0)) for t in tables]
        outs = (dtype, dtype) if with_plain else (dtype,)
        return _proj(name, [lhs], [rhs], _ep_rope(scale, with_plain), outs, tm=tm_, tn=tn_, extras=extras)

    def normed(name, rhs, gain, dtype):
        tm_, _ = tiles(h, rhs, False)
        n = rhs.n
        return _proj(name, [h], [rhs], _ep_rmsnorm, (dtype,), tm=tm_, tn=n,
                     extras=[(gain, (1, n), lambda i, j: (0, 0))])[0]

    cqn = normed("cq_norm", w.cq, w.g_cq, BF16)
    ckvn = normed("ckv_norm", w.ckv, w.g_ckv, F32)
    kn = plain("k_nope", ckvn, w.kn, BF16)
    vb = plain("v_mla", ckvn, w.vb, BF16)

    if not grp.rope:
        qa = plain("q_diff", h, w.qa, BF16, qk_scale)
        ka = plain("k_diff", h, w.ka, F32)
        va = plain("v_diff", h, w.va, F32)
        kr = plain("k_rope", h, w.kr, F32)
        q = plain("q_mla", cqn, w.q_up, BF16, mla_scale)
        oa = _diff_attention(w.lam_vecs, lam_init, [qa], [ka], [va], grp, hps)
        ob = _mla_attention([q], [kn], [kr], [vb], grp, hps)
        state = (ka, va, ckvn, kr[:, :ROPE_DIM])
    else:
        cos, sin = _rope_tables(grp.seq)
        t_diff = _widen_tables(cos, sin, 0, 0, A_HEAD_DIM)
        t_mla = _widen_tables(cos, sin, NOPE_DIM, MLA_Q_PAD - NOPE_DIM - ROPE_DIM, MLA_Q_PAD)
        t_kr = _widen_tables(cos, sin, 0, MLA_KR_PAD - ROPE_DIM, MLA_KR_PAD)
        ka_c, va_c, ckv_c, kr_c = cache
        qa_u, qa_r = roped("q_diff", h, w.qa, t_diff, BF16, qk_scale, True)
        ka_r, = roped("k_diff", h, w.ka, t_diff, BF16, 1.0, False)
        va = plain("v_diff", h, w.va, BF16)
        kr_r, = roped("k_rope", h, w.kr, t_kr, BF16, 1.0, False)
        q_u, q_r = roped("q_mla", cqn, w.q_up, t_mla, BF16, mla_scale, True)
        kn_c = plain("k_nope_ctx", ckv_c, w.kn, BF16)
        vb_c = plain("v_mla_ctx", ckv_c, w.vb, BF16)
        oa = _diff_attention(w.lam_vecs, lam_init, [qa_u, qa_r], [ka_c, ka_r], [va_c, va], grp, hps)
        ob = _mla_attention([q_u, q_r], [kn_c, kn], [kr_c, kr_r], [vb_c, vb], grp, hps)
        state = None
    x1 = _residual_proj("o_proj", [oa, ob], [w.o_a, w.o_b], x, mods, 2,
                        tm=_tile(rows_unit, 1024, V7X_SUBLANES_BF16), tn=512)
    return x1, state


def _conv_ffn(x, mods, g2, w_up, conv_w, conv_b, w_down, w_down_b, layer, grp):
    h = _norm(x, g2, mods, 3, 4, BF16)
    act, made = _ffn_up(h, w_up, conv_w, conv_b, layer, grp, None if w_down_b is not None else w_down)
    if w_down_b is None:
        w_down_b = made if made is not None else w_down[layer].astype(BF16)
    out = _residual_proj("ffn_down", [act], [_whole(w_down_b)], x, mods, 5,
                         tm=_tile(x.shape[0] // mods.shape[0], 512, V7X_SUBLANES_BF16), tn=512)
    return out, w_down_b


def kernel(x_prompt, x_sample, cache_diff_k, cache_diff_v, cache_mla_ckv, cache_mla_krope, c, c_ctx,
           norm1_g, norm2_g, w_mod, b_mod, w_in, g_cq, w_q_up, g_ckv, w_kv_up,
           lambda_q1, lambda_k1, lambda_q2, lambda_k2, w_o, w_pool, pool_scale,
           w_up, conv_w, conv_b, w_down, g_final):
    batch, seq, d = x_prompt.shape
    dec_batch, dec_seq, _ = x_sample.shape
    depth = w_mod.shape[0]
    past = cache_diff_k.shape[2]
    groups = (_Group(batch, seq, False), _Group(dec_batch, dec_seq, True))

    n_cond = 1 + dec_batch
    cond8 = jnp.zeros((8, d), F32).at[0].set(c_ctx).at[1:n_cond].set(c)
    mods_all = _adaln(cond8, w_mod, b_mod).reshape(depth, 8, N_MOD, d)
    mods_g = (mods_all[:, 0:1], mods_all[:, 1:n_cond])

    w_in_b = w_in.astype(BF16)
    w_o_b = w_o.astype(BF16)
    attn_w = {}
    for i in range(0, depth, 2):
        j = i // 2
        lam_vecs = jnp.stack([lambda_q1[j], lambda_k1[j], lambda_q2[j], lambda_k2[j]])
        attn_w[i] = _prep_attn_weights(j, w_in_b, w_in[j], g_cq[j], w_q_up[j], g_ckv[j], w_kv_up[j],
                                       w_o_b, lam_vecs)
    w_pool_b = w_pool.astype(BF16)
    w_down_b = [None] * depth

    xs = [x_prompt.reshape(batch * seq, d), x_sample.reshape(dec_batch * dec_seq, d)]
    states = []
    for gi, grp in enumerate(groups):
        x = xs[gi]
        for i in range(depth):
            mods = mods_g[gi][i]
            h = _norm(x, norm1_g[i], mods, 0, 1, BF16)
            if i % 2 == 0:
                j = i // 2
                lam_init = 0.8 - 0.6 * math.exp(-0.3 * i)
                cache = None
                if grp.rope:
                    kr_c = jnp.pad(cache_mla_krope[:, j].reshape(dec_batch * past, ROPE_DIM),
                                   ((0, 0), (0, MLA_KR_PAD - ROPE_DIM)))
                    cache = (cache_diff_k[:, j].reshape(dec_batch * past, A_HEADS * A_HEAD_DIM),
                             cache_diff_v[:, j].reshape(dec_batch * past, A_HEADS * A_V_DIM),
                             cache_mla_ckv[:, j].reshape(dec_batch * past, -1), kr_c)
                x, st = _attn_mixer(h, x, mods, attn_w[i], grp, lam_init, cache)
                if st is not None:
                    states.append(st)
            else:
                x = _pool_mixer(h, x, mods, w_pool_b, i // 2, pool_scale[i // 2], grp, 2)
            x, w_down_b[i] = _conv_ffn(x, mods, norm2_g[i], w_up, conv_w, conv_b, w_down, w_down_b[i], i, grp)
        xs[gi] = _norm(x, g_final, None, 0, 0, F32)

    y_prompt = xs[0].reshape(batch, seq, d)
    y_sample = xs[1].reshape(dec_batch, dec_seq, d)
    stack = lambda k, tail: jnp.stack([s[k].reshape((batch, seq) + tail) for s in states], axis=1)
    new_diff_k = stack(0, (A_HEADS, A_HEAD_DIM))
    new_diff_v = stack(1, (A_HEADS, A_V_DIM))
    new_mla_ckv = stack(2, (g_ckv.shape[-1],))
    new_mla_krope = stack(3, (ROPE_DIM,))
    return (y_prompt, y_sample, new_diff_k, new_diff_v, new_mla_ckv, new_mla_krope)
```

```python
import functools
import math
from typing import NamedTuple

import jax
import jax.numpy as jnp
from jax import lax
from jax.experimental import pallas as pl
from jax.experimental.pallas import tpu as pltpu

F32 = jnp.float32
BF16 = jnp.bfloat16

GRID_W = 64
ROPE_BASE = 10000.0
EPS = 1e-6
A_HEADS = 16
A_QK_HALF = 64
A_HEAD_DIM = 2 * A_QK_HALF
A_V_DIM = 128
B_HEADS = 16
NOPE_DIM = 128
ROPE_DIM = 64
B_V_DIM = 128
POOL_WINDOWS = (2, 4, 8, 16)
N_MOD = 6
CONV_W = 3

V7X_VMEM_BYTES = 64 * 1024 * 1024
V7X_LANES = 128
V7X_SUBLANES_BF16 = 16
COMPILER_SCRATCH_BYTES = 2 * 1024 * 1024

MLA_Q_PAD = 256
MLA_KR_PAD = MLA_Q_PAD - NOPE_DIM
ROPE_QUARTER = ROPE_DIM // 4
POOL_HALO = V7X_SUBLANES_BF16


class _Group(NamedTuple):
    n_seq: int
    seq: int
    rope: bool
    n_cond: int


def _tile(n, pref, mult=8):
    if n <= pref:
        return n
    t = (pref // mult) * mult
    while t >= mult:
        if n % t == 0:
            return t
        t -= mult
    return n


def _nbytes(shape, dtype):
    return math.prod(shape) * jnp.dtype(dtype).itemsize


def _call(body, *, name, grid, in_specs, out_specs, out_shape, args, temp_bytes=0, single_buffered=(),
          scratch=(), semantics=None, flags=None):
    multi = isinstance(out_shape, (list, tuple))
    outs = list(out_shape) if multi else [out_shape]
    ospecs = list(out_specs) if multi else [out_specs]
    total = temp_bytes + COMPILER_SCRATCH_BYTES + sum(_nbytes(s, dt) for s, dt in scratch)
    for k, (a, s) in enumerate(zip(args, in_specs)):
        blk = [1 if b is None else b for b in s.block_shape]
        total += _nbytes(blk, a.dtype) * (1 if k in single_buffered else 2)
    for o, s in zip(outs, ospecs):
        blk = [1 if b is None else b for b in s.block_shape]
        total += 2 * _nbytes(blk, o.dtype)
    limit = min(V7X_VMEM_BYTES - 4 * 1024 * 1024, max(total, 16 * 1024 * 1024))
    return pl.pallas_call(
        body, name=name, grid=grid, in_specs=in_specs, out_specs=out_specs, out_shape=out_shape,
        scratch_shapes=[pltpu.VMEM(s, dt) for s, dt in scratch],
        compiler_params=pltpu.CompilerParams(
            dimension_semantics=semantics or ("parallel",) * len(grid), vmem_limit_bytes=int(limit),
            flags=flags),
    )(*args)


def _dot(a, b):
    return jnp.dot(a, b, preferred_element_type=F32)


def _dot_nt(a, b):
    return lax.dot_general(a, b, (((1,), (1,)), ((), ())), preferred_element_type=F32)


def _silu(x):
    return x / (1.0 + jnp.exp(-x))


def _adaln(cond8, w_mod, b_mod):
    depth, d, n = w_mod.shape
    tn = next(t for t in (512, 256, 128) if n % (2 * t) == 0)

    def body(c_ref, wa_ref, wb_ref, b_ref, o_ref):
        s = _silu(c_ref[...]).astype(BF16)
        ya = _dot(s, wa_ref[...].astype(BF16))
        yb = _dot(s, wb_ref[...].astype(BF16))
        o_ref[...] = jnp.concatenate([ya, yb], axis=1) + b_ref[...]

    return _call(
        body, name="adaln", grid=(depth, n // (2 * tn)),
        in_specs=[pl.BlockSpec((8, d), lambda l, j: (0, 0)),
                  pl.BlockSpec((None, d, tn), lambda l, j: (l, 0, 2 * j)),
                  pl.BlockSpec((None, d, tn), lambda l, j: (l, 0, 2 * j + 1)),
                  pl.BlockSpec((None, 1, 2 * tn), lambda l, j: (l, 0, j))],
        out_specs=pl.BlockSpec((None, 8, 2 * tn), lambda l, j: (l, 0, j)),
        out_shape=jax.ShapeDtypeStruct((depth, 8, n), F32),
        args=(cond8, w_mod, w_mod, b_mod.reshape(depth, 1, n)),
        temp_bytes=2 * _nbytes((d, tn), BF16) + 2 * _nbytes((d, tn), F32))


def _norm(x, g, mods, k_shift, k_scale, out_dtype):
    m, d = x.shape
    modulated = mods is not None
    tm = _tile(m // mods.shape[0] if modulated else m, 512)

    def body(*refs):
        x_ref, g_ref = refs[0], refs[1]
        o_ref = refs[-1]
        xf = x_ref[...]
        y = xf * lax.rsqrt(jnp.mean(xf * xf, axis=-1, keepdims=True) + EPS) * g_ref[...]
        if modulated:
            mm = refs[2][...]
            y = y * (1.0 + mm[k_scale:k_scale + 1]) + mm[k_shift:k_shift + 1]
        o_ref[...] = y.astype(o_ref.dtype)

    in_specs = [pl.BlockSpec((tm, d), lambda i: (i, 0)), pl.BlockSpec((1, d), lambda i: (0, 0))]
    args = [x, g.reshape(1, d)]
    if modulated:
        rows_per_cond = m // mods.shape[0]
        in_specs.append(pl.BlockSpec((None, N_MOD, d), lambda i: ((i * tm) // rows_per_cond, 0, 0)))
        args.append(mods)
    return _call(body, name="norm", grid=(m // tm,), in_specs=in_specs,
                 out_specs=pl.BlockSpec((tm, d), lambda i: (i, 0)),
                 out_shape=jax.ShapeDtypeStruct((m, d), out_dtype), args=args,
                 temp_bytes=3 * _nbytes((tm, d), F32))


FFN_SEGS = 8


def _norm_interleaved(x, g, mods, k_shift, k_scale, tile):
    m, d = x.shape
    seg = tile // FFN_SEGS
    tv = _tile(seg, 64, V7X_SUBLANES_BF16)
    rows_per_cond = m // mods.shape[0]
    assert m % tile == 0 and rows_per_cond % tile == 0

    def body(x_ref, g_ref, m_ref, o_ref):
        mm = m_ref[...]
        for s in range(FFN_SEGS):
            xf = x_ref[s]
            y = xf * lax.rsqrt(jnp.mean(xf * xf, axis=-1, keepdims=True) + EPS) * g_ref[...]
            y = y * (1.0 + mm[k_scale:k_scale + 1]) + mm[k_shift:k_shift + 1]
            o_ref[:, s * d:(s + 1) * d] = y.astype(o_ref.dtype)

    out = _call(
        body, name="norm_interleaved", grid=(m // tile, seg // tv),
        in_specs=[pl.BlockSpec((None, FFN_SEGS, tv, d), lambda t, v: (t, 0, v, 0)),
                  pl.BlockSpec((1, d), lambda t, v: (0, 0)),
                  pl.BlockSpec((None, N_MOD, d), lambda t, v: ((t * tile) // rows_per_cond, 0, 0))],
        out_specs=pl.BlockSpec((None, tv, FFN_SEGS * d), lambda t, v: (t, v, 0)),
        out_shape=jax.ShapeDtypeStruct((m // tile, seg, FFN_SEGS * d), BF16),
        args=[x.reshape(m // tile, FFN_SEGS, seg, d), g.reshape(1, d), mods],
        temp_bytes=3 * _nbytes((tv, d), F32))
    return out.reshape(m, d)


class _Rows(NamedTuple):
    parts: list
    k: int


class _W(NamedTuple):
    arr: jax.Array
    layer: object
    k: int
    row_blk: int
    col0: int
    n: int

    def spec(self, tn, col=lambda i, j: j, **mode):
        assert self.col0 % tn == 0 and self.n % tn == 0
        c0 = self.col0 // tn
        if self.layer is None:
            return pl.BlockSpec((self.k, tn), lambda i, j: (self.row_blk, c0 + col(i, j)), **mode)
        return pl.BlockSpec((None, self.k, tn),
                            lambda i, j: (self.layer, self.row_blk, c0 + col(i, j)), **mode)


def _whole(arr):
    return _W(arr, None, arr.shape[0], 0, 0, arr.shape[1])


def _proj(name, lhs, rhs, epilogue, outs, *, tm, tn, extras=(), m=None):
    m = m if m is not None else next(a.shape[0] for a in lhs if not isinstance(a, _Rows))
    n = rhs[0].n
    np_ = len(lhs)
    ne = len(extras)
    lhs_parts = [a.parts if isinstance(a, _Rows) else [(a, (tm, a.shape[1]), lambda i, j: (i, 0))] for a in lhs]
    n_parts = [len(p) for p in lhs_parts]
    n_lhs = sum(n_parts)

    def body(*refs):
        acc = None
        at = 0
        for p in range(np_):
            blocks = [refs[at + q][...].astype(BF16) for q in range(n_parts[p])]
            at += n_parts[p]
            left = blocks[0] if len(blocks) == 1 else jnp.concatenate(blocks, axis=0)
            part = _dot(left, refs[n_lhs + p][...].astype(BF16))
            acc = part if acc is None else acc + part
        epilogue(acc, refs[n_lhs + np_:n_lhs + np_ + ne], refs[n_lhs + np_ + ne:])

    resident = n == tn
    rhs_mode = dict(pipeline_mode=pl.Buffered(1)) if resident else {}
    nj = n // tn
    col = (lambda i, j: j) if nj == 1 else (lambda i, j: jnp.where(i % 2 == 0, j, nj - 1 - j))
    flat = [part for parts in lhs_parts for part in parts]
    in_specs = ([pl.BlockSpec(bs, im) for (_, bs, im) in flat]
                + [w.spec(tn, col, **rhs_mode) for w in rhs]
                + [pl.BlockSpec(bs, lambda i, j, im=im: im(i, col(i, j))) for (_, bs, im) in extras])
    args = [a for (a, _, _) in flat] + [w.arr for w in rhs] + [e[0] for e in extras]
    out_shape = [jax.ShapeDtypeStruct((m, n), dt) for dt in outs]
    out_specs = [pl.BlockSpec((tm, tn), lambda i, j: (i, col(i, j))) for _ in outs]
    return _call(body, name=name, grid=(m // tm, nj), in_specs=in_specs, out_specs=out_specs,
                 out_shape=out_shape, args=args, temp_bytes=6 * _nbytes((tm, tn), F32),
                 single_buffered=tuple(range(n_lhs, n_lhs + np_)) if resident else ())


def _ep_scale(scale):
    def ep(acc, ex, outs):
        outs[0][...] = (acc * scale).astype(outs[0].dtype)
    return ep


def _rotate(a, cos, sin):
    width = a.shape[1]
    lane = lax.broadcasted_iota(jnp.int32, a.shape, 1)
    low = (lane & (2 * ROPE_QUARTER - 1)) < ROPE_QUARTER
    partner = jnp.where(low, pltpu.roll(a, width - ROPE_QUARTER, 1), pltpu.roll(a, ROPE_QUARTER, 1))
    return a * cos + partner * sin


def _ep_rope(scale, with_plain):
    def ep(acc, ex, outs):
        a = acc * scale
        reps = a.shape[1] // ex[0].shape[1]
        cos = jnp.concatenate([ex[0][...]] * reps, axis=1)
        sin = jnp.concatenate([ex[1][...]] * reps, axis=1)
        r = _rotate(a, cos, sin)
        if with_plain:
            outs[0][...] = a.astype(outs[0].dtype)
        outs[-1][...] = r.astype(outs[-1].dtype)
    return ep


def _ep_rmsnorm(acc, ex, outs):
    y = acc * lax.rsqrt(jnp.mean(acc * acc, axis=-1, keepdims=True) + EPS) * ex[0][...]
    outs[0][...] = y.astype(outs[0].dtype)


def _ep_residual(k_gate):
    def ep(acc, ex, outs):
        gate = ex[1][...][k_gate:k_gate + 1]
        outs[0][...] = ex[0][...] + gate * acc
    return ep


def _rope_tables(seq):
    half = ROPE_DIM // 2
    inv = ROPE_BASE ** (-jnp.arange(0, half, 2, dtype=F32) / half)
    pos = jnp.arange(seq)
    ar = (pos // GRID_W).astype(F32)[:, None] * inv[None, :]
    ac = (pos % GRID_W).astype(F32)[:, None] * inv[None, :]
    cos = jnp.concatenate([jnp.cos(ar), jnp.cos(ar), jnp.cos(ac), jnp.cos(ac)], axis=1)
    sin = jnp.concatenate([-jnp.sin(ar), jnp.sin(ar), -jnp.sin(ac), jnp.sin(ac)], axis=1)
    return cos, sin


def _widen_tables(cos, sin, lead, trail, width):
    seq = cos.shape[0]
    c = jnp.concatenate([jnp.ones((seq, lead), F32), cos, jnp.ones((seq, trail), F32)], axis=1)
    s = jnp.concatenate([jnp.zeros((seq, lead), F32), sin, jnp.zeros((seq, trail), F32)], axis=1)
    reps = width // c.shape[1]
    assert reps * c.shape[1] == width
    return jnp.tile(c, (1, reps)), jnp.tile(s, (1, reps))


def _diff_attention(lam_vecs, lam_init, qs, ks, vs, grp, hps):
    m = qs[0].shape[0]
    nseg = len(qs)
    tq = _tile(grp.seq, 256)
    qt = grp.seq // tq
    width = hps * A_HEAD_DIM
    n_hg = A_HEADS // hps
    k_lens = [k.shape[0] // grp.n_seq for k in ks]

    def body(*refs):
        lam_ref = refs[0]
        q_refs = refs[1:1 + nseg]
        k_refs = refs[1 + nseg:1 + 2 * nseg]
        v_refs = refs[1 + 2 * nseg:1 + 3 * nseg]
        o_ref = refs[-1]
        lv = lam_ref[...]
        lam = (jnp.exp(jnp.sum(lv[0:1] * lv[1:2], axis=-1, keepdims=True))
               - jnp.exp(jnp.sum(lv[2:3] * lv[3:4], axis=-1, keepdims=True)) + lam_init)
        first = lax.broadcasted_iota(jnp.int32, (tq, A_HEAD_DIM), 1) < A_QK_HALF
        for g in range(hps):
            cols = slice(g * A_HEAD_DIM, (g + 1) * A_HEAD_DIM)
            scores = []
            for q_ref, k_ref in zip(q_refs, k_refs):
                q = q_ref[:, cols]
                zero = jnp.zeros_like(q)
                q2 = jnp.concatenate([jnp.where(first, q, zero), jnp.where(first, zero, q)], axis=0)
                scores.append(_dot_nt(q2, k_ref[:, cols].astype(BF16)))
            mx = functools.reduce(jnp.maximum, [jnp.max(s, axis=-1, keepdims=True) for s in scores])
            ps = [jnp.exp(s - mx) for s in scores]
            den = functools.reduce(jnp.add, [jnp.sum(p, axis=-1, keepdims=True) for p in ps])
            o2 = functools.reduce(
                jnp.add, [_dot(p.astype(BF16), v_ref[:, cols].astype(BF16)) for p, v_ref in zip(ps, v_refs)])
            o2 = o2 / den
            o = o2[:tq] - lam * o2[tq:]
            o = o * lax.rsqrt(jnp.mean(o * o, axis=-1, keepdims=True) + EPS) * (1.0 - lam_init)
            o_ref[:, cols] = o.astype(o_ref.dtype)

    q_spec = pl.BlockSpec((tq, width), lambda b, h, i: (b * qt + i, h))
    in_specs = ([pl.BlockSpec(lam_vecs.shape, lambda b, h, i: (0, 0))]
                + [q_spec] * nseg
                + [pl.BlockSpec((kl, width), lambda b, h, i: (b, h)) for kl in k_lens] * 2)
    nk = sum(k_lens)
    return _call(body, name="diff_attention", grid=(grp.n_seq, n_hg, qt), in_specs=in_specs,
                 out_specs=q_spec, out_shape=jax.ShapeDtypeStruct((m, A_HEADS * A_V_DIM), BF16),
                 args=[lam_vecs] + list(qs) + list(ks) + list(vs),
                 temp_bytes=4 * _nbytes((2 * tq, nk), F32))


def _mla_attention(qs, kns, krs, vs, grp, hps):
    m = qs[0].shape[0]
    nseg = len(qs)
    tq = _tile(grp.seq, 256)
    qt = grp.seq // tq
    n_hg = B_HEADS // hps
    k_lens = [k.shape[0] // grp.n_seq for k in kns]

    def body(*refs):
        q_refs = refs[:nseg]
        kn_refs = refs[nseg:2 * nseg]
        kr_refs = refs[2 * nseg:3 * nseg]
        v_refs = refs[3 * nseg:4 * nseg]
        o_ref = refs[-1]
        krs_v = [r[...].astype(BF16) for r in kr_refs]
        for g in range(hps):
            kcols = slice(g * NOPE_DIM, (g + 1) * NOPE_DIM)
            vcols = slice(g * B_V_DIM, (g + 1) * B_V_DIM)
            scores = []
            for q_ref, kn_ref, kr in zip(q_refs, kn_refs, krs_v):
                q = q_ref[:, g * MLA_Q_PAD:(g + 1) * MLA_Q_PAD]
                k = jnp.concatenate([kn_ref[:, kcols], kr], axis=1)
                scores.append(_dot_nt(q, k))
            mx = functools.reduce(jnp.maximum, [jnp.max(s, axis=-1, keepdims=True) for s in scores])
            ps = [jnp.exp(s - mx) for s in scores]
            den = functools.reduce(jnp.add, [jnp.sum(p, axis=-1, keepdims=True) for p in ps])
            o = functools.reduce(
                jnp.add, [_dot(p.astype(BF16), v_ref[:, vcols]) for p, v_ref in zip(ps, v_refs)])
            o_ref[:, vcols] = (o / den).astype(o_ref.dtype)

    q_spec = pl.BlockSpec((tq, hps * MLA_Q_PAD), lambda b, h, i: (b * qt + i, h))
    kv_specs = [pl.BlockSpec((kl, hps * NOPE_DIM), lambda b, h, i: (b, h)) for kl in k_lens]
    kr_specs = [pl.BlockSpec((kl, MLA_KR_PAD), lambda b, h, i: (b, 0)) for kl in k_lens]
    nk = sum(k_lens)
    return _call(body, name="mla_attention", grid=(grp.n_seq, n_hg, qt),
                 in_specs=[q_spec] * nseg + kv_specs + kr_specs + kv_specs,
                 out_specs=pl.BlockSpec((tq, hps * B_V_DIM), lambda b, h, i: (b * qt + i, h)),
                 out_shape=jax.ShapeDtypeStruct((m, B_HEADS * B_V_DIM), BF16),
                 args=list(qs) + list(kns) + list(krs) + list(vs),
                 temp_bytes=4 * _nbytes((tq, nk), F32) + 2 * _nbytes((nk, MLA_Q_PAD), BF16))


def _pool_mixer(h, x, mods, w_pool, layer, pool_scale, grp, k_gate):
    m, d = x.shape
    _, n_groups, c, _ = w_pool.shape
    assert max(POOL_WINDOWS) // 2 <= POOL_HALO and n_groups == len(POOL_WINDOWS)
    r = _tile(grp.seq, 512, POOL_HALO)
    nt = m // r
    halo_per_tile = r // POOL_HALO
    rows_per_cond = m // mods.shape[0]
    seq = grp.seq
    kc = r + 2 * POOL_HALO

    def body(hp_ref, hc_ref, hn_ref, w_ref, ps_ref, x_ref, m_ref, o_ref):
        g = pl.program_id(0)
        i = pl.program_id(1)
        back = functools.reduce(jnp.add, [jnp.where(g == k, w // 2, 0) for k, w in enumerate(POOL_WINDOWS)])
        fwd = functools.reduce(jnp.add, [jnp.where(g == k, w - w // 2, 0) for k, w in enumerate(POOL_WINDOWS)])
        seq_start = ((i * r) // seq) * seq
        row = i * r + lax.broadcasted_iota(jnp.int32, (r, kc), 0)
        col = i * r - POOL_HALO + lax.broadcasted_iota(jnp.int32, (r, kc), 1)
        lo = jnp.maximum(row - back, seq_start)
        hi = jnp.minimum(row + fwd, seq_start + seq)
        member = jnp.where((col >= lo) & (col < hi), 1.0, 0.0).astype(BF16)
        hc = hc_ref[...]
        hcat = jnp.concatenate([hp_ref[...], hc, hn_ref[...]], axis=0)
        win = _dot(member, hcat)
        row_c = i * r + lax.broadcasted_iota(jnp.int32, (r, c), 0)
        cnt = (jnp.minimum(row_c + fwd, seq_start + seq) - jnp.maximum(row_c - back, seq_start)).astype(F32)
        pooled = win / cnt - hc.astype(F32)
        y = _dot(pooled.astype(BF16), w_ref[...]) * ps_ref[...]
        gate = m_ref[...][k_gate:k_gate + 1]
        o_ref[...] = x_ref[...] + gate * y

    last_halo = m // POOL_HALO - 1
    in_specs = [
        pl.BlockSpec((POOL_HALO, c), lambda g, i: (jnp.maximum(i * halo_per_tile - 1, 0), g)),
        pl.BlockSpec((r, c), lambda g, i: (i, g)),
        pl.BlockSpec((POOL_HALO, c), lambda g, i: (jnp.minimum((i + 1) * halo_per_tile, last_halo), g)),
        pl.BlockSpec((None, None, c, c), lambda g, i: (layer, g, 0, 0)),
        pl.BlockSpec((1, c), lambda g, i: (0, g)),
        pl.BlockSpec((r, c), lambda g, i: (i, g)),
        pl.BlockSpec((None, N_MOD, c), lambda g, i: ((i * r) // rows_per_cond, 0, g)),
    ]
    return _call(body, name="pool_mixer", grid=(n_groups, nt), in_specs=in_specs,
                 out_specs=pl.BlockSpec((r, c), lambda g, i: (i, g)),
                 out_shape=jax.ShapeDtypeStruct((m, d), F32),
                 args=[h, h, h, w_pool, pool_scale.reshape(1, d), x, mods],
                 temp_bytes=4 * _nbytes((r, kc), F32) + 4 * _nbytes((r, c), F32))


def _ffn_tile(grp):
    rows_per_cond = grp.n_seq * grp.seq // grp.n_cond
    return _tile(rows_per_cond, min(2048, FFN_SEGS * grp.seq), grp.seq)


def _ffn_up(h, w_up, conv_w, conv_b, layer, grp, w_down=None):
    m, d = h.shape
    f = w_up.shape[2] // 2
    tm = _ffn_tile(grp)
    tn = _tile(f, 256, V7X_LANES)
    nj = f // tn
    seq = grp.seq
    steps = (m // tm) * nj
    fuse_cast = w_down is not None and w_down.shape[1] % (steps * V7X_SUBLANES_BF16) == 0
    slab_rows = w_down.shape[1] // steps if fuse_cast else 0

    rc = 256
    rs = _tile(tm, 512, V7X_SUBLANES_BF16)
    seg = tm // FFN_SEGS
    assert tm % (FFN_SEGS * V7X_SUBLANES_BF16) == 0 and seq % seg == 0
    starts = tuple((s * seg) % seq == 0 for s in range(FFN_SEGS))
    ends = tuple(((s + 1) * seg) % seq == 0 for s in range(FFN_SEGS))

    def body(*refs):
        h_ref, wa_ref, wb_ref, cwa_ref, cwb_ref, cba_ref, cbb_ref = refs[:7]
        if fuse_cast:
            wd_ref, o_ref, wdo_ref, acc_ref = refs[7:]
            wdo_ref[...] = wd_ref[...].astype(BF16)
        else:
            o_ref, acc_ref = refs[7:]
        nseg = FFN_SEGS
        sub = lax.broadcasted_iota(jnp.int32, (nseg, tn), 0)
        cws = (cwa_ref[...], cwb_ref[...])
        cbs = (cba_ref[...], cbb_ref[...])

        def matmul(s):
            xs = h_ref[s * rs:(s + 1) * rs, :]
            for k, w_ref in enumerate((wa_ref, wb_ref)):
                acc_ref[k, s * rs:(s + 1) * rs] = _dot(xs, w_ref[...].astype(BF16))

        def gate(p0, p1, prevs, nxts):
            a, b = (prevs[k] * cws[k][0:1] + acc_ref[k, p0:p1] * cws[k][1:2] + nxts[k] * cws[k][2:3] + cbs[k]
                    for k in range(2))
            o_ref[p0:p1] = (_silu(a) * b).astype(o_ref.dtype)

        def interior(r0, r1):
            for p0 in range(r0, r1, rc):
                p1 = min(p0 + rc, r1)
                gate(p0, p1, [acc_ref[k, p0 - nseg:p1 - nseg] for k in range(2)],
                     [acc_ref[k, p0 + nseg:p1 + nseg] for k in range(2)])

        def select_rows(flags):
            return functools.reduce(jnp.logical_or, [sub == s for s in range(nseg) if flags[s]],
                                    jnp.zeros((nseg, tn), jnp.bool_))

        def edges():
            prevs = [jnp.where(select_rows(starts), 0.0, pltpu.roll(acc_ref[k, tm - nseg:tm], 1, 0))
                     for k in range(2)]
            gate(0, nseg, prevs, [acc_ref[k, nseg:2 * nseg] for k in range(2)])
            nxts = [jnp.where(select_rows(ends), 0.0, pltpu.roll(acc_ref[k, 0:nseg], nseg - 1, 0))
                    for k in range(2)]
            gate(tm - nseg, tm, [acc_ref[k, tm - 2 * nseg:tm - nseg] for k in range(2)], nxts)

        n_stage = tm // rs
        for s in range(n_stage + 1):
            if s > 0:
                interior(max((s - 1) * rs - nseg, nseg), (tm if s == n_stage else s * rs) - nseg)
            if s == n_stage:
                edges()
            if s < n_stage:
                matmul(s)

    col = lambda i, j: jnp.where(i % 2 == 0, j, nj - 1 - j)
    in_specs = [
        pl.BlockSpec((tm, d), lambda i, j: (i, 0)),
        pl.BlockSpec((None, d, tn), lambda i, j: (layer, 0, col(i, j))),
        pl.BlockSpec((None, d, tn), lambda i, j: (layer, 0, nj + col(i, j))),
        pl.BlockSpec((None, CONV_W, tn), lambda i, j: (layer, 0, col(i, j))),
        pl.BlockSpec((None, CONV_W, tn), lambda i, j: (layer, 0, nj + col(i, j))),
        pl.BlockSpec((None, 1, tn), lambda i, j: (layer, 0, col(i, j))),
        pl.BlockSpec((None, 1, tn), lambda i, j: (layer, 0, nj + col(i, j))),
    ]
    cb = conv_b.reshape(conv_b.shape[0], 1, 2 * f)
    args = [h, w_up, w_up, conv_w, conv_w, cb, cb]
    out_specs = [pl.BlockSpec((tm, tn), lambda i, j: (i, col(i, j)))]
    out_shape = [jax.ShapeDtypeStruct((m, f), BF16)]
    if fuse_cast:
        slab = (slab_rows, w_down.shape[2])
        in_specs.append(pl.BlockSpec((None,) + slab, lambda i, j: (layer, i * nj + j, 0)))
        args.append(w_down)
        out_specs.append(pl.BlockSpec(slab, lambda i, j: (i * nj + j, 0)))
        out_shape.append(jax.ShapeDtypeStruct(w_down.shape[1:], BF16))
    res = _call(body, name="ffn_up", grid=(m // tm, nj), in_specs=in_specs, out_specs=out_specs,
                out_shape=out_shape, args=args,
                temp_bytes=10 * _nbytes((rc, tn), F32) + 4 * _nbytes((rs, tn), F32) + 2 * _nbytes((d, tn), BF16),
                scratch=[((2, tm, tn), F32)])
    return res[0], (res[1] if fuse_cast else None)


def _residual_proj(name, lhs, rhs, x, mods, k_gate, *, tm, tn):
    m = x.shape[0]
    rows_per_cond = m // mods.shape[0]
    assert rows_per_cond % tm == 0
    extras = [(x, (tm, tn), lambda i, j: (i, j)),
              (mods, (None, N_MOD, tn), lambda i, j: ((i * tm) // rows_per_cond, 0, j))]
    return _proj(name, lhs, rhs, _ep_residual(k_gate), (F32,), tm=tm, tn=tn, extras=extras, m=m)[0]


class _AttnWeights(NamedTuple):
    qa: _W
    ka: _W
    va: _W
    cq: _W
    ckv: _W
    kr: _W
    g_cq: jax.Array
    g_ckv: jax.Array
    q_up: _W
    kn: _W
    vb: _W
    o_a: _W
    o_b: _W
    lam_vecs: jax.Array


def _prep_attn_weights(j, w_in_b, w_in, g_cq, w_q_up, g_ckv, w_kv_up, w_o_b, lam_vecs):
    d = w_in.shape[0]
    a_q = A_HEADS * A_HEAD_DIM
    a_v = A_HEADS * A_V_DIM
    b_out = B_HEADS * B_V_DIM
    assert a_v == b_out
    q_lora = g_cq.shape[0]
    kv_lora = g_ckv.shape[0]
    o0, o1, o2 = a_q, 2 * a_q, 2 * a_q + a_v
    o3, o4 = o2 + q_lora, o2 + q_lora + kv_lora
    kr = jnp.pad(w_in[:, o4:], ((0, 0), (0, MLA_KR_PAD - ROPE_DIM)))
    q_up = w_q_up.reshape(q_lora, B_HEADS, NOPE_DIM + ROPE_DIM)
    q_up = jnp.pad(q_up, ((0, 0), (0, 0), (0, MLA_Q_PAD - NOPE_DIM - ROPE_DIM)))
    kv = w_kv_up.reshape(kv_lora, B_HEADS, NOPE_DIM + B_V_DIM)
    c = lambda w: _whole(w.astype(BF16))
    win = lambda col0, n: _W(w_in_b, j, d, 0, col0, n)
    return _AttnWeights(
        qa=win(0, a_q), ka=win(o0, a_q), va=win(o1, a_v), cq=win(o2, q_lora), ckv=win(o3, kv_lora),
        kr=c(kr), g_cq=g_cq.reshape(1, q_lora), g_ckv=g_ckv.reshape(1, kv_lora),
        q_up=c(q_up.reshape(q_lora, B_HEADS * MLA_Q_PAD)),
        kn=c(kv[:, :, :NOPE_DIM].reshape(kv_lora, B_HEADS * NOPE_DIM)),
        vb=c(kv[:, :, NOPE_DIM:].reshape(kv_lora, B_HEADS * B_V_DIM)),
        o_a=_W(w_o_b, j, a_v, 0, 0, w_o_b.shape[-1]), o_b=_W(w_o_b, j, b_out, 1, 0, w_o_b.shape[-1]),
        lam_vecs=lam_vecs)


def _attn_mixer(h, x, mods, w, grp, lam_init, cache):
    m, d = h.shape
    hps = A_HEADS if grp.seq <= 256 else 4
    qk_scale = A_QK_HALF ** -0.5
    mla_scale = (NOPE_DIM + ROPE_DIM) ** -0.5
    rows_unit = grp.seq if grp.rope else m

    def tiles(lhs, rhs, heavy_epilogue):
        k, n = rhs.k, rhs.n
        pref = 1024 if k < 1024 else (256 if heavy_epilogue else 512)
        return _tile(min(rows_unit, lhs.shape[0]), pref, V7X_SUBLANES_BF16), min(n, 2048)

    def plain(name, lhs, rhs, dtype, scale=1.0):
        tm_, tn_ = tiles(lhs, rhs, False)
        return _proj(name, [lhs], [rhs], _ep_scale(scale), (dtype,), tm=tm_, tn=tn_)[0]

    def roped(name, lhs, rhs, tables, dtype, scale, with_plain):
        tm_, tn_ = tiles(lhs, rhs, True)
        tiles_per_seq = grp.seq // tm_
        extras = [(t, (tm_, t.shape[1]), lambda i, j: (i % tiles_per_seq, 0)) for t in tables]
        outs = (dtype, dtype) if with_plain else (dtype,)
        return _proj(name, [lhs], [rhs], _ep_rope(scale, with_plain), outs, tm=tm_, tn=tn_, extras=extras)

    def normed(name, rhs, gain, dtype):
        tm_, _ = tiles(h, rhs, False)
        n = rhs.n
        return _proj(name, [h], [rhs], _ep_rmsnorm, (dtype,), tm=tm_, tn=n,
                     extras=[(gain, (1, n), lambda i, j: (0, 0))])[0]

    cqn = normed("cq_norm", w.cq, w.g_cq, BF16)
    ckvn = normed("ckv_norm", w.ckv, w.g_ckv, F32)
    kn = plain("k_nope", ckvn, w.kn, BF16)
    vb = plain("v_mla", ckvn, w.vb, BF16)

    if not grp.rope:
        qa = plain("q_diff", h, w.qa, BF16, qk_scale)
        ka = plain("k_diff", h, w.ka, F32)
        va = plain("v_diff", h, w.va, F32)
        kr = plain("k_rope", h, w.kr, F32)
        q = plain("q_mla", cqn, w.q_up, BF16, mla_scale)
        oa = _diff_attention(w.lam_vecs, lam_init, [qa], [ka], [va], grp, hps)
        ob = _mla_attention([q], [kn], [kr], [vb], grp, hps)
        state = (ka, va, ckvn, kr[:, :ROPE_DIM])
    else:
        cos, sin = _rope_tables(grp.seq)
        t_diff = _widen_tables(cos, sin, 0, 0, A_HEAD_DIM)
        t_mla = _widen_tables(cos, sin, NOPE_DIM, MLA_Q_PAD - NOPE_DIM - ROPE_DIM, MLA_Q_PAD)
        t_kr = _widen_tables(cos, sin, 0, MLA_KR_PAD - ROPE_DIM, MLA_KR_PAD)
        ka_c, va_c, ckv_c, kr_c = cache
        qa_u, qa_r = roped("q_diff", h, w.qa, t_diff, BF16, qk_scale, True)
        ka_r, = roped("k_diff", h, w.ka, t_diff, BF16, 1.0, False)
        va = plain("v_diff", h, w.va, BF16)
        kr_r, = roped("k_rope", h, w.kr, t_kr, BF16, 1.0, False)
        q_u, q_r = roped("q_mla", cqn, w.q_up, t_mla, BF16, mla_scale, True)
        kn_c = plain("k_nope_ctx", ckv_c, w.kn, BF16)
        vb_c = plain("v_mla_ctx", ckv_c, w.vb, BF16)
        oa = _diff_attention(w.lam_vecs, lam_init, [qa_u, qa_r], [ka_c, ka_r], [va_c, va], grp, hps)
        ob = _mla_attention([q_u, q_r], [kn_c, kn], [kr_c, kr_r], [vb_c, vb], grp, hps)
        state = None
    x1 = _residual_proj("o_proj", [oa, ob], [w.o_a, w.o_b], x, mods, 2,
                        tm=_tile(rows_unit, 1024, V7X_SUBLANES_BF16), tn=512)
    return x1, state


def _conv_ffn(x, mods, g2, w_up, conv_w, conv_b, w_down, w_down_b, layer, grp):
    m = x.shape[0]
    tile = _ffn_tile(grp)
    seg = tile // FFN_SEGS
    h = _norm_interleaved(x, g2, mods, 3, 4, tile)
    act, made = _ffn_up(h, w_up, conv_w, conv_b, layer, grp, None if w_down_b is not None else w_down)
    if w_down_b is None:
        w_down_b = made if made is not None else w_down[layer].astype(BF16)
    f = act.shape[1]
    tm = _tile(min(m // mods.shape[0], tile), 512, seg)
    tiles_per_group, segs_per_tile = tile // tm, tm // seg
    act_v = act.reshape(m // FFN_SEGS, FFN_SEGS * f)
    lhs = _Rows([(act_v, (seg, f), lambda i, j, q=q: (i // tiles_per_group,
                                                        (i % tiles_per_group) * segs_per_tile + q))
                 for q in range(segs_per_tile)], f)
    out = _residual_proj("ffn_down", [lhs], [_whole(w_down_b)], x, mods, 5, tm=tm, tn=512)
    return out, w_down_b


def kernel(x_prompt, x_sample, cache_diff_k, cache_diff_v, cache_mla_ckv, cache_mla_krope, c, c_ctx,
           norm1_g, norm2_g, w_mod, b_mod, w_in, g_cq, w_q_up, g_ckv, w_kv_up,
           lambda_q1, lambda_k1, lambda_q2, lambda_k2, w_o, w_pool, pool_scale,
           w_up, conv_w, conv_b, w_down, g_final):
    batch, seq, d = x_prompt.shape
    dec_batch, dec_seq, _ = x_sample.shape
    depth = w_mod.shape[0]
    past = cache_diff_k.shape[2]
    groups = (_Group(batch, seq, False, 1), _Group(dec_batch, dec_seq, True, dec_batch))

    n_cond = 1 + dec_batch
    cond8 = jnp.zeros((8, d), F32).at[0].set(c_ctx).at[1:n_cond].set(c)
    mods_all = _adaln(cond8, w_mod, b_mod).reshape(depth, 8, N_MOD, d)
    mods_g = (mods_all[:, 0:1], mods_all[:, 1:n_cond])

    w_in_b = w_in.astype(BF16)
    w_o_b = w_o.astype(BF16)
    attn_w = {}
    for i in range(0, depth, 2):
        j = i // 2
        lam_vecs = jnp.stack([lambda_q1[j], lambda_k1[j], lambda_q2[j], lambda_k2[j]])
        attn_w[i] = _prep_attn_weights(j, w_in_b, w_in[j], g_cq[j], w_q_up[j], g_ckv[j], w_kv_up[j],
                                       w_o_b, lam_vecs)
    w_pool_b = w_pool.astype(BF16)
    w_down_b = [None] * depth

    xs = [x_prompt.reshape(batch * seq, d), x_sample.reshape(dec_batch * dec_seq, d)]
    states = []
    for gi, grp in enumerate(groups):
        x = xs[gi]
        for i in range(depth):
            mods = mods_g[gi][i]
            h = _norm(x, norm1_g[i], mods, 0, 1, BF16)
            if i % 2 == 0:
                j = i // 2
                lam_init = 0.8 - 0.6 * math.exp(-0.3 * i)
                cache = None
                if grp.rope:
                    kr_c = jnp.pad(cache_mla_krope[:, j].reshape(dec_batch * past, ROPE_DIM),
                                   ((0, 0), (0, MLA_KR_PAD - ROPE_DIM)))
                    cache = (cache_diff_k[:, j].reshape(dec_batch * past, A_HEADS * A_HEAD_DIM),
                             cache_diff_v[:, j].reshape(dec_batch * past, A_HEADS * A_V_DIM),
                             cache_mla_ckv[:, j].reshape(dec_batch * past, -1), kr_c)
                x, st = _attn_mixer(h, x, mods, attn_w[i], grp, lam_init, cache)
                if st is not None:
                    states.append(st)
            else:
                x = _pool_mixer(h, x, mods, w_pool_b, i // 2, pool_scale[i // 2], grp, 2)
            x, w_down_b[i] = _conv_ffn(x, mods, norm2_g[i], w_up, conv_w, conv_b, w_down, w_down_b[i], i, grp)
        xs[gi] = _norm(x, g_final, None, 0, 0, F32)

    y_prompt = xs[0].reshape(batch, seq, d)
    y_sample = xs[1].reshape(dec_batch, dec_seq, d)
    stack = lambda k, tail: jnp.stack([s[k].reshape((batch, seq) + tail) for s in states], axis=1)
    new_diff_k = stack(0, (A_HEADS, A_HEAD_DIM))
    new_diff_v = stack(1, (A_HEADS, A_V_DIM))
    new_mla_ckv = stack(2, (g_ckv.shape[-1],))
    new_mla_krope = stack(3, (ROPE_DIM,))
    return (y_prompt, y_sample, new_diff_k, new_diff_v, new_mla_ckv, new_mla_krope)
```

```python
import functools
import math
from typing import NamedTuple

import jax
import jax.numpy as jnp
from jax import lax
from jax.experimental import pallas as pl
from jax.experimental.pallas import tpu as pltpu

F32 = jnp.float32
BF16 = jnp.bfloat16

GRID_W = 64
ROPE_BASE = 10000.0
EPS = 1e-6
A_HEADS = 16
A_QK_HALF = 64
A_HEAD_DIM = 2 * A_QK_HALF
A_V_DIM = 128
B_HEADS = 16
NOPE_DIM = 128
ROPE_DIM = 64
B_V_DIM = 128
POOL_WINDOWS = (2, 4, 8, 16)
N_MOD = 6
CONV_W = 3
LOG2_E = math.log2(math.e)

V7X_VMEM_BYTES = 64 * 1024 * 1024
V7X_LANES = 128
V7X_SUBLANES_BF16 = 16
COMPILER_SCRATCH_BYTES = 2 * 1024 * 1024

MLA_Q_PAD = 256
MLA_KR_PAD = MLA_Q_PAD - NOPE_DIM
ROPE_QUARTER = ROPE_DIM // 4
POOL_HALO = V7X_SUBLANES_BF16


class _Group(NamedTuple):
    n_seq: int
    seq: int
    rope: bool


def _tile(n, pref, mult=8):
    if n <= pref:
        return n
    t = (pref // mult) * mult
    while t >= mult:
        if n % t == 0:
            return t
        t -= mult
    return n


def _nbytes(shape, dtype):
    return math.prod(shape) * jnp.dtype(dtype).itemsize


def _call(body, *, name, grid, in_specs, out_specs, out_shape, args, temp_bytes=0, single_buffered=(),
          scratch=(), semantics=None, flags=None):
    multi = isinstance(out_shape, (list, tuple))
    outs = list(out_shape) if multi else [out_shape]
    ospecs = list(out_specs) if multi else [out_specs]
    total = temp_bytes + COMPILER_SCRATCH_BYTES + sum(_nbytes(s, dt) for s, dt in scratch)
    for k, (a, s) in enumerate(zip(args, in_specs)):
        blk = [1 if b is None else b for b in s.block_shape]
        total += _nbytes(blk, a.dtype) * (1 if k in single_buffered else 2)
    for o, s in zip(outs, ospecs):
        blk = [1 if b is None else b for b in s.block_shape]
        total += 2 * _nbytes(blk, o.dtype)
    limit = min(V7X_VMEM_BYTES - 4 * 1024 * 1024, max(total, 16 * 1024 * 1024))
    return pl.pallas_call(
        body, name=name, grid=grid, in_specs=in_specs, out_specs=out_specs, out_shape=out_shape,
        scratch_shapes=[pltpu.VMEM(s, dt) for s, dt in scratch],
        compiler_params=pltpu.CompilerParams(
            dimension_semantics=semantics or ("parallel",) * len(grid), vmem_limit_bytes=int(limit),
            flags=flags),
    )(*args)


def _dot(a, b):
    return jnp.dot(a, b, preferred_element_type=F32)


def _dot_nt(a, b):
    return lax.dot_general(a, b, (((1,), (1,)), ((), ())), preferred_element_type=F32)


def _silu(x):
    return x / (1.0 + jnp.exp(-x))


def _adaln(cond8, w_mod, b_mod):
    depth, d, n = w_mod.shape
    tn = next(t for t in (512, 256, 128) if n % (2 * t) == 0)

    def body(c_ref, wa_ref, wb_ref, b_ref, o_ref):
        s = _silu(c_ref[...]).astype(BF16)
        ya = _dot(s, wa_ref[...].astype(BF16))
        yb = _dot(s, wb_ref[...].astype(BF16))
        o_ref[...] = jnp.concatenate([ya, yb], axis=1) + b_ref[...]

    return _call(
        body, name="adaln", grid=(depth, n // (2 * tn)),
        in_specs=[pl.BlockSpec((8, d), lambda l, j: (0, 0)),
                  pl.BlockSpec((None, d, tn), lambda l, j: (l, 0, 2 * j)),
                  pl.BlockSpec((None, d, tn), lambda l, j: (l, 0, 2 * j + 1)),
                  pl.BlockSpec((None, 1, 2 * tn), lambda l, j: (l, 0, j))],
        out_specs=pl.BlockSpec((None, 8, 2 * tn), lambda l, j: (l, 0, j)),
        out_shape=jax.ShapeDtypeStruct((depth, 8, n), F32),
        args=(cond8, w_mod, w_mod, b_mod.reshape(depth, 1, n)),
        temp_bytes=2 * _nbytes((d, tn), BF16) + 2 * _nbytes((d, tn), F32))


def _norm(x, g, mods, k_shift, k_scale, out_dtype):
    m, d = x.shape
    modulated = mods is not None
    tm = _tile(m // mods.shape[0] if modulated else m, 512)

    def body(*refs):
        x_ref, g_ref = refs[0], refs[1]
        o_ref = refs[-1]
        xf = x_ref[...]
        y = xf * lax.rsqrt(jnp.mean(xf * xf, axis=-1, keepdims=True) + EPS) * g_ref[...]
        if modulated:
            mm = refs[2][...]
            y = y * (1.0 + mm[k_scale:k_scale + 1]) + mm[k_shift:k_shift + 1]
        o_ref[...] = y.astype(o_ref.dtype)

    in_specs = [pl.BlockSpec((tm, d), lambda i: (i, 0)), pl.BlockSpec((1, d), lambda i: (0, 0))]
    args = [x, g.reshape(1, d)]
    if modulated:
        rows_per_cond = m // mods.shape[0]
        in_specs.append(pl.BlockSpec((None, N_MOD, d), lambda i: ((i * tm) // rows_per_cond, 0, 0)))
        args.append(mods)
    return _call(body, name="norm", grid=(m // tm,), in_specs=in_specs,
                 out_specs=pl.BlockSpec((tm, d), lambda i: (i, 0)),
                 out_shape=jax.ShapeDtypeStruct((m, d), out_dtype), args=args,
                 temp_bytes=3 * _nbytes((tm, d), F32))


class _W(NamedTuple):
    arr: jax.Array
    layer: object
    k: int
    row_blk: int
    col0: int
    n: int

    def spec(self, tn, col=lambda i, j: j, **mode):
        assert self.col0 % tn == 0 and self.n % tn == 0
        c0 = self.col0 // tn
        if self.layer is None:
            return pl.BlockSpec((self.k, tn), lambda i, j: (self.row_blk, c0 + col(i, j)), **mode)
        return pl.BlockSpec((None, self.k, tn),
                            lambda i, j: (self.layer, self.row_blk, c0 + col(i, j)), **mode)


def _whole(arr):
    return _W(arr, None, arr.shape[0], 0, 0, arr.shape[1])


def _proj(name, lhs, rhs, epilogue, outs, *, tm, tn, extras=()):
    m = lhs[0].shape[0]
    n = rhs[0].n
    np_ = len(lhs)
    ne = len(extras)

    def body(*refs):
        acc = None
        for p in range(np_):
            part = _dot(refs[p][...].astype(BF16), refs[np_ + p][...].astype(BF16))
            acc = part if acc is None else acc + part
        epilogue(acc, refs[2 * np_:2 * np_ + ne], refs[2 * np_ + ne:])

    resident = n == tn
    rhs_mode = dict(pipeline_mode=pl.Buffered(1)) if resident else {}
    nj = n // tn
    col = (lambda i, j: j) if nj == 1 else (lambda i, j: jnp.where(i % 2 == 0, j, nj - 1 - j))
    in_specs = ([pl.BlockSpec((tm, a.shape[1]), lambda i, j: (i, 0)) for a in lhs]
                + [w.spec(tn, col, **rhs_mode) for w in rhs]
                + [pl.BlockSpec(bs, lambda i, j, im=im: im(i, col(i, j))) for (_, bs, im) in extras])
    args = list(lhs) + [w.arr for w in rhs] + [e[0] for e in extras]
    out_shape = [jax.ShapeDtypeStruct((m, n), dt) for dt in outs]
    out_specs = [pl.BlockSpec((tm, tn), lambda i, j: (i, col(i, j))) for _ in outs]
    return _call(body, name=name, grid=(m // tm, nj), in_specs=in_specs, out_specs=out_specs,
                 out_shape=out_shape, args=args, temp_bytes=6 * _nbytes((tm, tn), F32),
                 single_buffered=tuple(range(np_, 2 * np_)) if resident else ())


def _ep_scale(scale):
    def ep(acc, ex, outs):
        outs[0][...] = (acc * scale).astype(outs[0].dtype)
    return ep


def _rotate(a, cos, sin):
    width = a.shape[1]
    lane = lax.broadcasted_iota(jnp.int32, a.shape, 1)
    low = (lane & (2 * ROPE_QUARTER - 1)) < ROPE_QUARTER
    partner = jnp.where(low, pltpu.roll(a, width - ROPE_QUARTER, 1), pltpu.roll(a, ROPE_QUARTER, 1))
    return a * cos + partner * sin


def _ep_rope(scale, with_plain):
    def ep(acc, ex, outs):
        a = acc * scale
        reps = a.shape[1] // ex[0].shape[1]
        cos = jnp.concatenate([ex[0][...]] * reps, axis=1)
        sin = jnp.concatenate([ex[1][...]] * reps, axis=1)
        r = _rotate(a, cos, sin)
        if with_plain:
            outs[0][...] = a.astype(outs[0].dtype)
        outs[-1][...] = r.astype(outs[-1].dtype)
    return ep


def _ep_rmsnorm(acc, ex, outs):
    y = acc * lax.rsqrt(jnp.mean(acc * acc, axis=-1, keepdims=True) + EPS) * ex[0][...]
    outs[0][...] = y.astype(outs[0].dtype)


def _ep_residual(k_gate):
    def ep(acc, ex, outs):
        gate = ex[1][...][k_gate:k_gate + 1]
        outs[0][...] = ex[0][...] + gate * acc
    return ep


def _rope_tables(seq):
    half = ROPE_DIM // 2
    inv = ROPE_BASE ** (-jnp.arange(0, half, 2, dtype=F32) / half)
    pos = jnp.arange(seq)
    ar = (pos // GRID_W).astype(F32)[:, None] * inv[None, :]
    ac = (pos % GRID_W).astype(F32)[:, None] * inv[None, :]
    cos = jnp.concatenate([jnp.cos(ar), jnp.cos(ar), jnp.cos(ac), jnp.cos(ac)], axis=1)
    sin = jnp.concatenate([-jnp.sin(ar), jnp.sin(ar), -jnp.sin(ac), jnp.sin(ac)], axis=1)
    return cos, sin


def _widen_tables(cos, sin, lead, trail, width):
    seq = cos.shape[0]
    c = jnp.concatenate([jnp.ones((seq, lead), F32), cos, jnp.ones((seq, trail), F32)], axis=1)
    s = jnp.concatenate([jnp.zeros((seq, lead), F32), sin, jnp.zeros((seq, trail), F32)], axis=1)
    reps = width // c.shape[1]
    assert reps * c.shape[1] == width
    return jnp.tile(c, (1, reps)), jnp.tile(s, (1, reps))


def _diff_attention(lam_vecs, lam_init, qs, ks, vs, grp, hps):
    m = qs[0].shape[0]
    nseg = len(qs)
    tq = _tile(grp.seq, 256)
    qt = grp.seq // tq
    width = hps * A_HEAD_DIM
    n_hg = A_HEADS // hps
    k_lens = [k.shape[0] // grp.n_seq for k in ks]

    def body(*refs):
        lam_ref = refs[0]
        q_refs = refs[1:1 + nseg]
        k_refs = refs[1 + nseg:1 + 2 * nseg]
        v_refs = refs[1 + 2 * nseg:1 + 3 * nseg]
        o_ref = refs[-1]
        lv = lam_ref[...]
        lam = (jnp.exp(jnp.sum(lv[0:1] * lv[1:2], axis=-1, keepdims=True))
               - jnp.exp(jnp.sum(lv[2:3] * lv[3:4], axis=-1, keepdims=True)) + lam_init)
        first = lax.broadcasted_iota(jnp.int32, (tq, A_HEAD_DIM), 1) < A_QK_HALF
        for g in range(hps):
            cols = slice(g * A_HEAD_DIM, (g + 1) * A_HEAD_DIM)
            scores = []
            for q_ref, k_ref in zip(q_refs, k_refs):
                q = q_ref[:, cols]
                zero = jnp.zeros_like(q)
                q2 = jnp.concatenate([jnp.where(first, q, zero), jnp.where(first, zero, q)], axis=0)
                scores.append(_dot_nt(q2, k_ref[:, cols].astype(BF16)))
            mx = functools.reduce(jnp.maximum, [jnp.max(s, axis=-1, keepdims=True) for s in scores])
            ps = [jnp.exp2(s - mx) for s in scores]
            den = functools.reduce(jnp.add, [jnp.sum(p, axis=-1, keepdims=True) for p in ps])
            o2 = functools.reduce(
                jnp.add, [_dot(p.astype(BF16), v_ref[:, cols].astype(BF16)) for p, v_ref in zip(ps, v_refs)])
            o2 = o2 / den
            o = o2[:tq] - lam * o2[tq:]
            o = o * lax.rsqrt(jnp.mean(o * o, axis=-1, keepdims=True) + EPS) * (1.0 - lam_init)
            o_ref[:, cols] = o.astype(o_ref.dtype)

    q_spec = pl.BlockSpec((tq, width), lambda b, h, i: (b * qt + i, h))
    in_specs = ([pl.BlockSpec(lam_vecs.shape, lambda b, h, i: (0, 0))]
                + [q_spec] * nseg
                + [pl.BlockSpec((kl, width), lambda b, h, i: (b, h)) for kl in k_lens] * 2)
    nk = sum(k_lens)
    return _call(body, name="diff_attention", grid=(grp.n_seq, n_hg, qt), in_specs=in_specs,
                 out_specs=q_spec, out_shape=jax.ShapeDtypeStruct((m, A_HEADS * A_V_DIM), BF16),
                 args=[lam_vecs] + list(qs) + list(ks) + list(vs),
                 temp_bytes=4 * _nbytes((2 * tq, nk), F32))


def _mla_attention(qs, kns, krs, vs, grp, hps):
    m = qs[0].shape[0]
    nseg = len(qs)
    tq = _tile(grp.seq, 256)
    qt = grp.seq // tq
    n_hg = B_HEADS // hps
    k_lens = [k.shape[0] // grp.n_seq for k in kns]

    def body(*refs):
        q_refs = refs[:nseg]
        kn_refs = refs[nseg:2 * nseg]
        kr_refs = refs[2 * nseg:3 * nseg]
        v_refs = refs[3 * nseg:4 * nseg]
        o_ref = refs[-1]
        krs_v = [r[...].astype(BF16) for r in kr_refs]
        for g in range(hps):
            kcols = slice(g * NOPE_DIM, (g + 1) * NOPE_DIM)
            vcols = slice(g * B_V_DIM, (g + 1) * B_V_DIM)
            scores = []
            for q_ref, kn_ref, kr in zip(q_refs, kn_refs, krs_v):
                q = q_ref[:, g * MLA_Q_PAD:(g + 1) * MLA_Q_PAD]
                k = jnp.concatenate([kn_ref[:, kcols], kr], axis=1)
                scores.append(_dot_nt(q, k))
            mx = functools.reduce(jnp.maximum, [jnp.max(s, axis=-1, keepdims=True) for s in scores])
            ps = [jnp.exp2(s - mx) for s in scores]
            den = functools.reduce(jnp.add, [jnp.sum(p, axis=-1, keepdims=True) for p in ps])
            o = functools.reduce(
                jnp.add, [_dot(p.astype(BF16), v_ref[:, vcols]) for p, v_ref in zip(ps, v_refs)])
            o_ref[:, vcols] = (o / den).astype(o_ref.dtype)

    q_spec = pl.BlockSpec((tq, hps * MLA_Q_PAD), lambda b, h, i: (b * qt + i, h))
    kv_specs = [pl.BlockSpec((kl, hps * NOPE_DIM), lambda b, h, i: (b, h)) for kl in k_lens]
    kr_specs = [pl.BlockSpec((kl, MLA_KR_PAD), lambda b, h, i: (b, 0)) for kl in k_lens]
    nk = sum(k_lens)
    return _call(body, name="mla_attention", grid=(grp.n_seq, n_hg, qt),
                 in_specs=[q_spec] * nseg + kv_specs + kr_specs + kv_specs,
                 out_specs=pl.BlockSpec((tq, hps * B_V_DIM), lambda b, h, i: (b * qt + i, h)),
                 out_shape=jax.ShapeDtypeStruct((m, B_HEADS * B_V_DIM), BF16),
                 args=list(qs) + list(kns) + list(krs) + list(vs),
                 temp_bytes=4 * _nbytes((tq, nk), F32) + 2 * _nbytes((nk, MLA_Q_PAD), BF16))


def _pool_mixer(h, x, mods, w_pool, layer, pool_scale, grp, k_gate):
    m, d = x.shape
    _, n_groups, c, _ = w_pool.shape
    assert max(POOL_WINDOWS) // 2 <= POOL_HALO and n_groups == len(POOL_WINDOWS)
    rows_per_cond = m // mods.shape[0]
    r = _tile(rows_per_cond, 512, POOL_HALO)
    nt = m // r
    halo_per_tile = r // POOL_HALO
    kc = r + 2 * POOL_HALO
    period = math.lcm(r, grp.seq) // r
    member, inv_len = _pool_tables(grp.seq, r, period, kc)

    def body(hp_ref, hc_ref, hn_ref, mem_ref, inv_ref, w_ref, ps_ref, x_ref, m_ref, o_ref):
        hc = hc_ref[...]
        hcat = jnp.concatenate([hp_ref[...], hc, hn_ref[...]], axis=0)
        win = _dot(mem_ref[...], hcat)
        inv = jnp.concatenate([inv_ref[...]] * (c // V7X_LANES), axis=1)
        pooled = win * inv - hc.astype(F32)
        y = _dot(pooled.astype(BF16), w_ref[...]) * ps_ref[...]
        gate = m_ref[...][k_gate:k_gate + 1]
        o_ref[...] = x_ref[...] + gate * y

    last_halo = m // POOL_HALO - 1
    in_specs = [
        pl.BlockSpec((POOL_HALO, c), lambda g, i: (jnp.maximum(i * halo_per_tile - 1, 0), g)),
        pl.BlockSpec((r, c), lambda g, i: (i, g)),
        pl.BlockSpec((POOL_HALO, c), lambda g, i: (jnp.minimum((i + 1) * halo_per_tile, last_halo), g)),
        pl.BlockSpec((None, None, r, kc), lambda g, i: (g, i % period, 0, 0)),
        pl.BlockSpec((None, r, V7X_LANES), lambda g, i: (g, i % period, 0)),
        pl.BlockSpec((None, None, c, c), lambda g, i: (layer, g, 0, 0)),
        pl.BlockSpec((1, c), lambda g, i: (0, g)),
        pl.BlockSpec((r, c), lambda g, i: (i, g)),
        pl.BlockSpec((None, N_MOD, c), lambda g, i: ((i * r) // rows_per_cond, 0, g)),
    ]
    return _call(body, name="pool_mixer", grid=(n_groups, nt), in_specs=in_specs,
                 out_specs=pl.BlockSpec((r, c), lambda g, i: (i, g)),
                 out_shape=jax.ShapeDtypeStruct((m, d), F32),
                 args=[h, h, h, member, inv_len, w_pool, pool_scale.reshape(1, d), x, mods],
                 temp_bytes=4 * _nbytes((r, c), F32))


def _pool_tables(seq, r, period, kc):
    p = jnp.arange(period)[:, None, None]
    row = p * r + jnp.arange(r)[None, :, None]
    col = p * r - POOL_HALO + jnp.arange(kc)[None, None, :]
    seq_start = (row // seq) * seq
    members, invs = [], []
    for w in POOL_WINDOWS:
        lo = jnp.maximum(row - w // 2, seq_start)
        hi = jnp.minimum(row + w - w // 2, seq_start + seq)
        members.append(((col >= lo) & (col < hi)).astype(BF16))
        invs.append(jnp.broadcast_to(1.0 / (hi - lo).astype(F32), (period, r, V7X_LANES)))
    inv = jnp.stack(invs).reshape(len(POOL_WINDOWS), period * r, V7X_LANES)
    return jnp.stack(members), inv


def _ffn_up(h, w_up, conv_w, conv_b, layer, grp, w_down=None):
    m, d = h.shape
    f = w_up.shape[2] // 2
    tm = _tile(m, max(grp.seq, 2048), grp.seq)
    tn = _tile(f, 256, V7X_LANES)
    nj = f // tn
    seq = grp.seq
    steps = (m // tm) * nj
    fuse_cast = w_down is not None and w_down.shape[1] % (steps * V7X_SUBLANES_BF16) == 0
    slab_rows = w_down.shape[1] // steps if fuse_cast else 0

    rc = _tile(seq, 256)
    pad = 8
    rs = _tile(tm, 512, V7X_SUBLANES_BF16)

    def body(*refs):
        h_ref, wa_ref, wb_ref, cwa_ref, cwb_ref, cba_ref, cbb_ref = refs[:7]
        if fuse_cast:
            wd_ref, o_ref, wdo_ref, acc_ref = refs[7:]
            wdo_ref[...] = wd_ref[...].astype(BF16)
        else:
            o_ref, acc_ref = refs[7:]
        row8 = lax.broadcasted_iota(jnp.int32, (8, tn), 0)

        def matmul(s):
            xs = h_ref[s * rs:(s + 1) * rs, :]
            for k, w_ref in enumerate((wa_ref, wb_ref)):
                acc_ref[k, pad + s * rs:pad + (s + 1) * rs] = _dot(xs, w_ref[...].astype(BF16))

        def piece(p0, p1):
            n = p1 - p0

            def conv(k, cw_ref, cb_ref):
                cw = cw_ref[...]
                u = acc_ref[k, pad + p0:pad + p1]
                prev = acc_ref[k, pad + p0 - 1:pad + p1 - 1]
                nxt = acc_ref[k, pad + p0 + 1:pad + p1 + 1]
                if p0 % seq == 0:
                    top = jnp.where(row8 == 0, 0.0, prev[:8])
                    prev = top if n == 8 else jnp.concatenate([top, prev[8:]], axis=0)
                if p1 % seq == 0:
                    bot = jnp.where(row8 == 7, 0.0, nxt[n - 8:])
                    nxt = bot if n == 8 else jnp.concatenate([nxt[:n - 8], bot], axis=0)
                return prev * cw[0:1] + u * cw[1:2] + nxt * cw[2:3] + cb_ref[...]

            a = conv(0, cwa_ref, cba_ref)
            b = conv(1, cwb_ref, cbb_ref)
            o_ref[p0:p1] = (_silu(a) * b).astype(o_ref.dtype)

        def epilogue(r0, r1):
            cuts = sorted({r0, r1} | {c for c in range(0, tm + 1, rc) if r0 < c < r1})
            for p0, p1 in zip(cuts[:-1], cuts[1:]):
                piece(p0, p1)

        n_stage = tm // rs
        for k in range(2):
            acc_ref[k, 0:pad] = jnp.zeros((pad, tn), F32)
            acc_ref[k, pad + tm:2 * pad + tm] = jnp.zeros((pad, tn), F32)
        for s in range(n_stage + 1):
            if s > 0:
                epilogue(max((s - 1) * rs - pad, 0), tm if s == n_stage else s * rs - pad)
            if s < n_stage:
                matmul(s)

    col = lambda i, j: jnp.where(i % 2 == 0, j, nj - 1 - j)
    in_specs = [
        pl.BlockSpec((tm, d), lambda i, j: (i, 0)),
        pl.BlockSpec((None, d, tn), lambda i, j: (layer, 0, col(i, j))),
        pl.BlockSpec((None, d, tn), lambda i, j: (layer, 0, nj + col(i, j))),
        pl.BlockSpec((None, CONV_W, tn), lambda i, j: (layer, 0, col(i, j))),
        pl.BlockSpec((None, CONV_W, tn), lambda i, j: (layer, 0, nj + col(i, j))),
        pl.BlockSpec((None, 1, tn), lambda i, j: (layer, 0, col(i, j))),
        pl.BlockSpec((None, 1, tn), lambda i, j: (layer, 0, nj + col(i, j))),
    ]
    cb = conv_b.reshape(conv_b.shape[0], 1, 2 * f)
    args = [h, w_up, w_up, conv_w, conv_w, cb, cb]
    out_specs = [pl.BlockSpec((tm, tn), lambda i, j: (i, col(i, j)))]
    out_shape = [jax.ShapeDtypeStruct((m, f), BF16)]
    if fuse_cast:
        slab = (slab_rows, w_down.shape[2])
        in_specs.append(pl.BlockSpec((None,) + slab, lambda i, j: (layer, i * nj + j, 0)))
        args.append(w_down)
        out_specs.append(pl.BlockSpec(slab, lambda i, j: (i * nj + j, 0)))
        out_shape.append(jax.ShapeDtypeStruct(w_down.shape[1:], BF16))
    res = _call(body, name="ffn_up", grid=(m // tm, nj), in_specs=in_specs, out_specs=out_specs,
                out_shape=out_shape, args=args,
                temp_bytes=10 * _nbytes((rc, tn), F32) + 4 * _nbytes((rs, tn), F32) + 2 * _nbytes((d, tn), BF16),
                scratch=[((2, tm + 2 * pad, tn), F32)])
    return res[0], (res[1] if fuse_cast else None)


def _residual_proj(name, lhs, rhs, x, mods, k_gate, *, tm, tn):
    m = x.shape[0]
    rows_per_cond = m // mods.shape[0]
    assert rows_per_cond % tm == 0
    extras = [(x, (tm, tn), lambda i, j: (i, j)),
              (mods, (None, N_MOD, tn), lambda i, j: ((i * tm) // rows_per_cond, 0, j))]
    return _proj(name, lhs, rhs, _ep_residual(k_gate), (F32,), tm=tm, tn=tn, extras=extras)[0]


class _AttnWeights(NamedTuple):
    qa: _W
    ka: _W
    va: _W
    cq: _W
    ckv: _W
    kr: _W
    g_cq: jax.Array
    g_ckv: jax.Array
    q_up: _W
    kn: _W
    vb: _W
    o_a: _W
    o_b: _W
    lam_vecs: jax.Array


def _prep_attn_weights(j, w_in_b, w_in, g_cq, w_q_up, g_ckv, w_kv_up, w_o_b, lam_vecs):
    d = w_in.shape[0]
    a_q = A_HEADS * A_HEAD_DIM
    a_v = A_HEADS * A_V_DIM
    b_out = B_HEADS * B_V_DIM
    assert a_v == b_out
    q_lora = g_cq.shape[0]
    kv_lora = g_ckv.shape[0]
    o0, o1, o2 = a_q, 2 * a_q, 2 * a_q + a_v
    o3, o4 = o2 + q_lora, o2 + q_lora + kv_lora
    kr = jnp.pad(w_in[:, o4:], ((0, 0), (0, MLA_KR_PAD - ROPE_DIM)))
    q_up = w_q_up.reshape(q_lora, B_HEADS, NOPE_DIM + ROPE_DIM)
    q_up = jnp.pad(q_up, ((0, 0), (0, 0), (0, MLA_Q_PAD - NOPE_DIM - ROPE_DIM)))
    kv = w_kv_up.reshape(kv_lora, B_HEADS, NOPE_DIM + B_V_DIM)
    c = lambda w: _whole(w.astype(BF16))
    win = lambda col0, n: _W(w_in_b, j, d, 0, col0, n)
    return _AttnWeights(
        qa=win(0, a_q), ka=win(o0, a_q), va=win(o1, a_v), cq=win(o2, q_lora), ckv=win(o3, kv_lora),
        kr=c(kr), g_cq=g_cq.reshape(1, q_lora), g_ckv=g_ckv.reshape(1, kv_lora),
        q_up=c(q_up.reshape(q_lora, B_HEADS * MLA_Q_PAD)),
        kn=c(kv[:, :, :NOPE_DIM].reshape(kv_lora, B_HEADS * NOPE_DIM)),
        vb=c(kv[:, :, NOPE_DIM:].reshape(kv_lora, B_HEADS * B_V_DIM)),
        o_a=_W(w_o_b, j, a_v, 0, 0, w_o_b.shape[-1]), o_b=_W(w_o_b, j, b_out, 1, 0, w_o_b.shape[-1]),
        lam_vecs=lam_vecs)


def _attn_mixer(h, x, mods, w, grp, lam_init, cache):
    m, d = h.shape
    hps = A_HEADS if grp.seq <= 256 else 4
    qk_scale = A_QK_HALF ** -0.5 * LOG2_E
    mla_scale = (NOPE_DIM + ROPE_DIM) ** -0.5 * LOG2_E
    rows_unit = grp.seq if grp.rope else m

    def tiles(lhs, rhs, heavy_epilogue):
        k, n = rhs.k, rhs.n
        pref = 1024 if k < 1024 else (256 if heavy_epilogue else 512)
        return _tile(min(rows_unit, lhs.shape[0]), pref, V7X_SUBLANES_BF16), min(n, 2048)

    def plain(name, lhs, rhs, dtype, scale=1.0):
        tm_, tn_ = tiles(lhs, rhs, False)
        return _proj(name, [lhs], [rhs], _ep_scale(scale), (dtype,), tm=tm_, tn=tn_)[0]

    def roped(name, lhs, rhs, tables, dtype, scale, with_plain):
        tm_, tn_ = tiles(lhs, rhs, True)
        tiles_per_seq = grp.seq // tm_
        extras = [(t, (tm_, t.shape[1]), lambda i, j: (i % tiles_per_seq, 0)) for t in tables]
        outs = (dtype, dtype) if with_plain else (dtype,)
        return _proj(name, [lhs], [rhs], _ep_rope(scale, with_plain), outs, tm=tm_, tn=tn_, extras=extras)

    def normed(name, rhs, gain, dtype):
        tm_, _ = tiles(h, rhs, False)
        n = rhs.n
        return _proj(name, [h], [rhs], _ep_rmsnorm, (dtype,), tm=tm_, tn=n,
                     extras=[(gain, (1, n), lambda i, j: (0, 0))])[0]

    cqn = normed("cq_norm", w.cq, w.g_cq, BF16)
    ckvn = normed("ckv_norm", w.ckv, w.g_ckv, F32)
    kn = plain("k_nope", ckvn, w.kn, BF16)
    vb = plain("v_mla", ckvn, w.vb, BF16)

    if not grp.rope:
        qa = plain("q_diff", h, w.qa, BF16, qk_scale)
        ka = plain("k_diff", h, w.ka, F32)
        va = plain("v_diff", h, w.va, F32)
        kr = plain("k_rope", h, w.kr, F32)
        q = plain("q_mla", cqn, w.q_up, BF16, mla_scale)
        oa = _diff_attention(w.lam_vecs, lam_init, [qa], [ka], [va], grp, hps)
        ob = _mla_attention([q], [kn], [kr], [vb], grp, hps)
        state = (ka, va, ckvn, kr[:, :ROPE_DIM])
    else:
        cos, sin = _rope_tables(grp.seq)
        t_diff = _widen_tables(cos, sin, 0, 0, A_HEAD_DIM)
        t_mla = _widen_tables(cos, sin, NOPE_DIM, MLA_Q_PAD - NOPE_DIM - ROPE_DIM, MLA_Q_PAD)
        t_kr = _widen_tables(cos, sin, 0, MLA_KR_PAD - ROPE_DIM, MLA_KR_PAD)
        ka_c, va_c, ckv_c, kr_c = cache
        qa_u, qa_r = roped("q_diff", h, w.qa, t_diff, BF16, qk_scale, True)
        ka_r, = roped("k_diff", h, w.ka, t_diff, BF16, 1.0, False)
        va = plain("v_diff", h, w.va, BF16)
        kr_r, = roped("k_rope", h, w.kr, t_kr, BF16, 1.0, False)
        q_u, q_r = roped("q_mla", cqn, w.q_up, t_mla, BF16, mla_scale, True)
        kn_c = plain("k_nope_ctx", ckv_c, w.kn, BF16)
        vb_c = plain("v_mla_ctx", ckv_c, w.vb, BF16)
        oa = _diff_attention(w.lam_vecs, lam_init, [qa_u, qa_r], [ka_c, ka_r], [va_c, va], grp, hps)
        ob = _mla_attention([q_u, q_r], [kn_c, kn], [kr_c, kr_r], [vb_c, vb], grp, hps)
        state = None
    x1 = _residual_proj("o_proj", [oa, ob], [w.o_a, w.o_b], x, mods, 2,
                        tm=_tile(rows_unit, 1024, V7X_SUBLANES_BF16), tn=512)
    return x1, state


def _conv_ffn(x, mods, g2, w_up, conv_w, conv_b, w_down, w_down_b, layer, grp):
    h = _norm(x, g2, mods, 3, 4, BF16)
    act, made = _ffn_up(h, w_up, conv_w, conv_b, layer, grp, None if w_down_b is not None else w_down)
    if w_down_b is None:
        w_down_b = made if made is not None else w_down[layer].astype(BF16)
    out = _residual_proj("ffn_down", [act], [_whole(w_down_b)], x, mods, 5,
                         tm=_tile(x.shape[0] // mods.shape[0], 512, V7X_SUBLANES_BF16), tn=512)
    return out, w_down_b


def kernel(x_prompt, x_sample, cache_diff_k, cache_diff_v, cache_mla_ckv, cache_mla_krope, c, c_ctx,
           norm1_g, norm2_g, w_mod, b_mod, w_in, g_cq, w_q_up, g_ckv, w_kv_up,
           lambda_q1, lambda_k1, lambda_q2, lambda_k2, w_o, w_pool, pool_scale,
           w_up, conv_w, conv_b, w_down, g_final):
    batch, seq, d = x_prompt.shape
    dec_batch, dec_seq, _ = x_sample.shape
    depth = w_mod.shape[0]
    past = cache_diff_k.shape[2]
    groups = (_Group(batch, seq, False), _Group(dec_batch, dec_seq, True))

    n_cond = 1 + dec_batch
    cond8 = jnp.zeros((8, d), F32).at[0].set(c_ctx).at[1:n_cond].set(c)
    mods_all = _adaln(cond8, w_mod, b_mod).reshape(depth, 8, N_MOD, d)
    mods_g = (mods_all[:, 0:1], mods_all[:, 1:n_cond])

    w_in_b = w_in.astype(BF16)
    w_o_b = w_o.astype(BF16)
    attn_w = {}
    for i in range(0, depth, 2):
        j = i // 2
        lam_vecs = jnp.stack([lambda_q1[j], lambda_k1[j], lambda_q2[j], lambda_k2[j]])
        attn_w[i] = _prep_attn_weights(j, w_in_b, w_in[j], g_cq[j], w_q_up[j], g_ckv[j], w_kv_up[j],
                                       w_o_b, lam_vecs)
    w_pool_b = w_pool.astype(BF16)
    w_down_b = [None] * depth

    xs = [x_prompt.reshape(batch * seq, d), x_sample.reshape(dec_batch * dec_seq, d)]
    states = []
    for gi, grp in enumerate(groups):
        x = xs[gi]
        for i in range(depth):
            mods = mods_g[gi][i]
            h = _norm(x, norm1_g[i], mods, 0, 1, BF16)
            if i % 2 == 0:
                j = i // 2
                lam_init = 0.8 - 0.6 * math.exp(-0.3 * i)
                cache = None
                if grp.rope:
                    kr_c = jnp.pad(cache_mla_krope[:, j].reshape(dec_batch * past, ROPE_DIM),
                                   ((0, 0), (0, MLA_KR_PAD - ROPE_DIM)))
                    cache = (cache_diff_k[:, j].reshape(dec_batch * past, A_HEADS * A_HEAD_DIM),
                             cache_diff_v[:, j].reshape(dec_batch * past, A_HEADS * A_V_DIM),
                             cache_mla_ckv[:, j].reshape(dec_batch * past, -1), kr_c)
                x, st = _attn_mixer(h, x, mods, attn_w[i], grp, lam_init, cache)
                if st is not None:
                    states.append(st)
            else:
                x = _pool_mixer(h, x, mods, w_pool_b, i // 2, pool_scale[i // 2], grp, 2)
            x, w_down_b[i] = _conv_ffn(x, mods, norm2_g[i], w_up, conv_w, conv_b, w_down, w_down_b[i], i, grp)
        xs[gi] = _norm(x, g_final, None, 0, 0, F32)

    y_prompt = xs[0].reshape(batch, seq, d)
    y_sample = xs[1].reshape(dec_batch, dec_seq, d)
    stack = lambda k, tail: jnp.stack([s[k].reshape((batch, seq) + tail) for s in states], axis=1)
    new_diff_k = stack(0, (A_HEADS, A_HEAD_DIM))
    new_diff_v = stack(1, (A_HEADS, A_V_DIM))
    new_mla_ckv = stack(2, (g_ckv.shape[-1],))
    new_mla_krope = stack(3, (ROPE_DIM,))
    return (y_prompt, y_sample, new_diff_k, new_diff_v, new_mla_ckv, new_mla_krope)
```

```python
import functools
import math
from typing import NamedTuple

import jax
import jax.numpy as jnp
from jax import lax
from jax.experimental import pallas as pl
from jax.experimental.pallas import tpu as pltpu

F32 = jnp.float32
BF16 = jnp.bfloat16

GRID_W = 64
ROPE_BASE = 10000.0
EPS = 1e-6
A_HEADS = 16
A_QK_HALF = 64
A_HEAD_DIM = 2 * A_QK_HALF
A_V_DIM = 128
B_HEADS = 16
NOPE_DIM = 128
ROPE_DIM = 64
B_V_DIM = 128
POOL_WINDOWS = (2, 4, 8, 16)
N_MOD = 6
CONV_W = 3
LOG2_E = math.log2(math.e)

V7X_VMEM_BYTES = 64 * 1024 * 1024
V7X_LANES = 128
V7X_SUBLANES_BF16 = 16
COMPILER_SCRATCH_BYTES = 2 * 1024 * 1024

MLA_Q_PAD = 256
MLA_KR_PAD = MLA_Q_PAD - NOPE_DIM
ROPE_QUARTER = ROPE_DIM // 4
POOL_HALO = V7X_SUBLANES_BF16


class _Group(NamedTuple):
    n_seq: int
    seq: int
    rope: bool


def _tile(n, pref, mult=8):
    if n <= pref:
        return n
    t = (pref // mult) * mult
    while t >= mult:
        if n % t == 0:
            return t
        t -= mult
    return n


def _nbytes(shape, dtype):
    return math.prod(shape) * jnp.dtype(dtype).itemsize


def _call(body, *, name, grid, in_specs, out_specs, out_shape, args, temp_bytes=0, single_buffered=(),
          scratch=(), semantics=None, flags=None):
    multi = isinstance(out_shape, (list, tuple))
    outs = list(out_shape) if multi else [out_shape]
    ospecs = list(out_specs) if multi else [out_specs]
    total = temp_bytes + COMPILER_SCRATCH_BYTES + sum(_nbytes(s, dt) for s, dt in scratch)
    for k, (a, s) in enumerate(zip(args, in_specs)):
        blk = [1 if b is None else b for b in s.block_shape]
        total += _nbytes(blk, a.dtype) * (1 if k in single_buffered else 2)
    for o, s in zip(outs, ospecs):
        blk = [1 if b is None else b for b in s.block_shape]
        total += 2 * _nbytes(blk, o.dtype)
    limit = min(V7X_VMEM_BYTES - 4 * 1024 * 1024, max(total, 16 * 1024 * 1024))
    return pl.pallas_call(
        body, name=name, grid=grid, in_specs=in_specs, out_specs=out_specs, out_shape=out_shape,
        scratch_shapes=[pltpu.VMEM(s, dt) for s, dt in scratch],
        compiler_params=pltpu.CompilerParams(
            dimension_semantics=semantics or ("parallel",) * len(grid), vmem_limit_bytes=int(limit),
            flags=flags),
    )(*args)


def _dot(a, b):
    return jnp.dot(a, b, preferred_element_type=F32)


def _dot_nt(a, b):
    return lax.dot_general(a, b, (((1,), (1,)), ((), ())), preferred_element_type=F32)


def _silu(x):
    return x / (1.0 + jnp.exp(-x))


def _adaln(cond8, w_mod, b_mod):
    depth, d, n = w_mod.shape
    tn = next(t for t in (512, 256, 128) if n % (2 * t) == 0)

    def body(c_ref, wa_ref, wb_ref, b_ref, o_ref):
        s = _silu(c_ref[...]).astype(BF16)
        ya = _dot(s, wa_ref[...].astype(BF16))
        yb = _dot(s, wb_ref[...].astype(BF16))
        o_ref[...] = jnp.concatenate([ya, yb], axis=1) + b_ref[...]

    return _call(
        body, name="adaln", grid=(depth, n // (2 * tn)),
        in_specs=[pl.BlockSpec((8, d), lambda l, j: (0, 0)),
                  pl.BlockSpec((None, d, tn), lambda l, j: (l, 0, 2 * j)),
                  pl.BlockSpec((None, d, tn), lambda l, j: (l, 0, 2 * j + 1)),
                  pl.BlockSpec((None, 1, 2 * tn), lambda l, j: (l, 0, j))],
        out_specs=pl.BlockSpec((None, 8, 2 * tn), lambda l, j: (l, 0, j)),
        out_shape=jax.ShapeDtypeStruct((depth, 8, n), F32),
        args=(cond8, w_mod, w_mod, b_mod.reshape(depth, 1, n)),
        temp_bytes=2 * _nbytes((d, tn), BF16) + 2 * _nbytes((d, tn), F32))


def _norm(x, g, mods, k_shift, k_scale, out_dtype):
    m, d = x.shape
    modulated = mods is not None
    tm = _tile(m // mods.shape[0] if modulated else m, 512)

    def body(*refs):
        x_ref, g_ref = refs[0], refs[1]
        o_ref = refs[-1]
        xf = x_ref[...]
        y = xf * lax.rsqrt(jnp.mean(xf * xf, axis=-1, keepdims=True) + EPS) * g_ref[...]
        if modulated:
            mm = refs[2][...]
            y = y * (1.0 + mm[k_scale:k_scale + 1]) + mm[k_shift:k_shift + 1]
        o_ref[...] = y.astype(o_ref.dtype)

    in_specs = [pl.BlockSpec((tm, d), lambda i: (i, 0)), pl.BlockSpec((1, d), lambda i: (0, 0))]
    args = [x, g.reshape(1, d)]
    if modulated:
        rows_per_cond = m // mods.shape[0]
        in_specs.append(pl.BlockSpec((None, N_MOD, d), lambda i: ((i * tm) // rows_per_cond, 0, 0)))
        args.append(mods)
    return _call(body, name="norm", grid=(m // tm,), in_specs=in_specs,
                 out_specs=pl.BlockSpec((tm, d), lambda i: (i, 0)),
                 out_shape=jax.ShapeDtypeStruct((m, d), out_dtype), args=args,
                 temp_bytes=3 * _nbytes((tm, d), F32))


class _W(NamedTuple):
    arr: jax.Array
    layer: object
    k: int
    row_blk: int
    col0: int
    n: int

    def spec(self, tn, col=lambda i, j: j, **mode):
        assert self.col0 % tn == 0 and self.n % tn == 0
        c0 = self.col0 // tn
        if self.layer is None:
            return pl.BlockSpec((self.k, tn), lambda i, j: (self.row_blk, c0 + col(i, j)), **mode)
        return pl.BlockSpec((None, self.k, tn),
                            lambda i, j: (self.layer, self.row_blk, c0 + col(i, j)), **mode)


def _whole(arr):
    return _W(arr, None, arr.shape[0], 0, 0, arr.shape[1])


def _proj(name, lhs, rhs, epilogue, outs, *, tm, tn, extras=()):
    m = lhs[0].shape[0]
    n = rhs[0].n
    np_ = len(lhs)
    ne = len(extras)

    def body(*refs):
        acc = None
        for p in range(np_):
            part = _dot(refs[p][...].astype(BF16), refs[np_ + p][...].astype(BF16))
            acc = part if acc is None else acc + part
        epilogue(acc, refs[2 * np_:2 * np_ + ne], refs[2 * np_ + ne:])

    resident = n == tn
    rhs_mode = dict(pipeline_mode=pl.Buffered(1)) if resident else {}
    nj = n // tn
    col = (lambda i, j: j) if nj == 1 else (lambda i, j: jnp.where(i % 2 == 0, j, nj - 1 - j))
    in_specs = ([pl.BlockSpec((tm, a.shape[1]), lambda i, j: (i, 0)) for a in lhs]
                + [w.spec(tn, col, **rhs_mode) for w in rhs]
                + [pl.BlockSpec(bs, lambda i, j, im=im: im(i, col(i, j))) for (_, bs, im) in extras])
    args = list(lhs) + [w.arr for w in rhs] + [e[0] for e in extras]
    out_shape = [jax.ShapeDtypeStruct((m, n), dt) for dt in outs]
    out_specs = [pl.BlockSpec((tm, tn), lambda i, j: (i, col(i, j))) for _ in outs]
    return _call(body, name=name, grid=(m // tm, nj), in_specs=in_specs, out_specs=out_specs,
                 out_shape=out_shape, args=args, temp_bytes=6 * _nbytes((tm, tn), F32),
                 single_buffered=tuple(range(np_, 2 * np_)) if resident else ())


def _ep_scale(scale):
    def ep(acc, ex, outs):
        outs[0][...] = (acc * scale).astype(outs[0].dtype)
    return ep


def _rotate(a, cos, sin):
    width = a.shape[1]
    lane = lax.broadcasted_iota(jnp.int32, a.shape, 1)
    low = (lane & (2 * ROPE_QUARTER - 1)) < ROPE_QUARTER
    partner = jnp.where(low, pltpu.roll(a, width - ROPE_QUARTER, 1), pltpu.roll(a, ROPE_QUARTER, 1))
    return a * cos + partner * sin


def _ep_rope(scale, with_plain):
    def ep(acc, ex, outs):
        a = acc * scale
        reps = a.shape[1] // ex[0].shape[1]
        cos = jnp.concatenate([ex[0][...]] * reps, axis=1)
        sin = jnp.concatenate([ex[1][...]] * reps, axis=1)
        r = _rotate(a, cos, sin)
        if with_plain:
            outs[0][...] = a.astype(outs[0].dtype)
        outs[-1][...] = r.astype(outs[-1].dtype)
    return ep


def _ep_rmsnorm(acc, ex, outs):
    y = acc * lax.rsqrt(jnp.mean(acc * acc, axis=-1, keepdims=True) + EPS) * ex[0][...]
    outs[0][...] = y.astype(outs[0].dtype)


def _ep_residual(k_gate):
    def ep(acc, ex, outs):
        gate = ex[1][...][k_gate:k_gate + 1]
        outs[0][...] = ex[0][...] + gate * acc
    return ep


def _rope_tables(seq):
    half = ROPE_DIM // 2
    inv = ROPE_BASE ** (-jnp.arange(0, half, 2, dtype=F32) / half)
    pos = jnp.arange(seq)
    ar = (pos // GRID_W).astype(F32)[:, None] * inv[None, :]
    ac = (pos % GRID_W).astype(F32)[:, None] * inv[None, :]
    cos = jnp.concatenate([jnp.cos(ar), jnp.cos(ar), jnp.cos(ac), jnp.cos(ac)], axis=1)
    sin = jnp.concatenate([-jnp.sin(ar), jnp.sin(ar), -jnp.sin(ac), jnp.sin(ac)], axis=1)
    return cos, sin


def _widen_tables(cos, sin, lead, trail, width):
    seq = cos.shape[0]
    c = jnp.concatenate([jnp.ones((seq, lead), F32), cos, jnp.ones((seq, trail), F32)], axis=1)
    s = jnp.concatenate([jnp.zeros((seq, lead), F32), sin, jnp.zeros((seq, trail), F32)], axis=1)
    reps = width // c.shape[1]
    assert reps * c.shape[1] == width
    return jnp.tile(c, (1, reps)), jnp.tile(s, (1, reps))


def _diff_attention(lam_vecs, lam_init, qs, ks, vs, grp, hps):
    m = qs[0].shape[0]
    nseg = len(qs)
    tq = _tile(grp.seq, 256)
    qt = grp.seq // tq
    width = hps * A_HEAD_DIM
    n_hg = A_HEADS // hps
    k_lens = [k.shape[0] // grp.n_seq for k in ks]

    def body(*refs):
        lam_ref = refs[0]
        q_refs = refs[1:1 + nseg]
        k_refs = refs[1 + nseg:1 + 2 * nseg]
        v_refs = refs[1 + 2 * nseg:1 + 3 * nseg]
        o_ref = refs[-1]
        lv = lam_ref[...]
        lam = (jnp.exp(jnp.sum(lv[0:1] * lv[1:2], axis=-1, keepdims=True))
               - jnp.exp(jnp.sum(lv[2:3] * lv[3:4], axis=-1, keepdims=True)) + lam_init)
        first = lax.broadcasted_iota(jnp.int32, (tq, A_HEAD_DIM), 1) < A_QK_HALF
        for g in range(hps):
            cols = slice(g * A_HEAD_DIM, (g + 1) * A_HEAD_DIM)
            scores = []
            for q_ref, k_ref in zip(q_refs, k_refs):
                q = q_ref[:, cols]
                zero = jnp.zeros_like(q)
                q2 = jnp.concatenate([jnp.where(first, q, zero), jnp.where(first, zero, q)], axis=0)
                scores.append(_dot_nt(q2, k_ref[:, cols].astype(BF16)))
            mx = functools.reduce(jnp.maximum, [jnp.max(s, axis=-1, keepdims=True) for s in scores])
            ps = [jnp.exp2(s - mx) for s in scores]
            den = functools.reduce(jnp.add, [jnp.sum(p, axis=-1, keepdims=True) for p in ps])
            o2 = functools.reduce(
                jnp.add, [_dot(p.astype(BF16), v_ref[:, cols].astype(BF16)) for p, v_ref in zip(ps, v_refs)])
            o2 = o2 / den
            o = o2[:tq] - lam * o2[tq:]
            o = o * lax.rsqrt(jnp.mean(o * o, axis=-1, keepdims=True) + EPS) * (1.0 - lam_init)
            o_ref[:, cols] = o.astype(o_ref.dtype)

    q_spec = pl.BlockSpec((tq, width), lambda b, h, i: (b * qt + i, h))
    in_specs = ([pl.BlockSpec(lam_vecs.shape, lambda b, h, i: (0, 0))]
                + [q_spec] * nseg
                + [pl.BlockSpec((kl, width), lambda b, h, i: (b, h)) for kl in k_lens] * 2)
    nk = sum(k_lens)
    return _call(body, name="diff_attention", grid=(grp.n_seq, n_hg, qt), in_specs=in_specs,
                 out_specs=q_spec, out_shape=jax.ShapeDtypeStruct((m, A_HEADS * A_V_DIM), BF16),
                 args=[lam_vecs] + list(qs) + list(ks) + list(vs),
                 temp_bytes=4 * _nbytes((2 * tq, nk), F32))


def _mla_attention(qs, kns, krs, vs, grp, hps):
    m = qs[0].shape[0]
    nseg = len(qs)
    tq = _tile(grp.seq, 256)
    qt = grp.seq // tq
    n_hg = B_HEADS // hps
    k_lens = [k.shape[0] // grp.n_seq for k in kns]

    def body(*refs):
        q_refs = refs[:nseg]
        kn_refs = refs[nseg:2 * nseg]
        kr_refs = refs[2 * nseg:3 * nseg]
        v_refs = refs[3 * nseg:4 * nseg]
        o_ref = refs[-1]
        krs_v = [r[...].astype(BF16) for r in kr_refs]
        for g in range(hps):
            kcols = slice(g * NOPE_DIM, (g + 1) * NOPE_DIM)
            vcols = slice(g * B_V_DIM, (g + 1) * B_V_DIM)
            scores = []
            for q_ref, kn_ref, kr in zip(q_refs, kn_refs, krs_v):
                q = q_ref[:, g * MLA_Q_PAD:(g + 1) * MLA_Q_PAD]
                k = jnp.concatenate([kn_ref[:, kcols], kr], axis=1)
                scores.append(_dot_nt(q, k))
            mx = functools.reduce(jnp.maximum, [jnp.max(s, axis=-1, keepdims=True) for s in scores])
            ps = [jnp.exp2(s - mx) for s in scores]
            den = functools.reduce(jnp.add, [jnp.sum(p, axis=-1, keepdims=True) for p in ps])
            o = functools.reduce(
                jnp.add, [_dot(p.astype(BF16), v_ref[:, vcols]) for p, v_ref in zip(ps, v_refs)])
            o_ref[:, vcols] = (o / den).astype(o_ref.dtype)

    q_spec = pl.BlockSpec((tq, hps * MLA_Q_PAD), lambda b, h, i: (b * qt + i, h))
    kv_specs = [pl.BlockSpec((kl, hps * NOPE_DIM), lambda b, h, i: (b, h)) for kl in k_lens]
    kr_specs = [pl.BlockSpec((kl, MLA_KR_PAD), lambda b, h, i: (b, 0)) for kl in k_lens]
    nk = sum(k_lens)
    return _call(body, name="mla_attention", grid=(grp.n_seq, n_hg, qt),
                 in_specs=[q_spec] * nseg + kv_specs + kr_specs + kv_specs,
                 out_specs=pl.BlockSpec((tq, hps * B_V_DIM), lambda b, h, i: (b * qt + i, h)),
                 out_shape=jax.ShapeDtypeStruct((m, B_HEADS * B_V_DIM), BF16),
                 args=list(qs) + list(kns) + list(krs) + list(vs),
                 temp_bytes=4 * _nbytes((tq, nk), F32) + 2 * _nbytes((nk, MLA_Q_PAD), BF16))


def _pool_mixer(h, x, mods, w_pool, layer, pool_scale, grp, k_gate):
    m, d = x.shape
    _, n_groups, c, _ = w_pool.shape
    assert max(POOL_WINDOWS) // 2 <= POOL_HALO and n_groups == len(POOL_WINDOWS)
    rows_per_cond = m // mods.shape[0]
    r = _tile(rows_per_cond, 512, POOL_HALO)
    nt = m // r
    halo_per_tile = r // POOL_HALO
    kc = r + 2 * POOL_HALO
    period = math.lcm(r, grp.seq) // r
    member, inv_len = _pool_tables(grp.seq, r, period, kc)

    def body(hp_ref, hc_ref, hn_ref, mem_ref, inv_ref, w_ref, ps_ref, x_ref, m_ref, o_ref):
        hc = hc_ref[...]
        hcat = jnp.concatenate([hp_ref[...], hc, hn_ref[...]], axis=0)
        win = _dot(mem_ref[...], hcat)
        inv = jnp.concatenate([inv_ref[...]] * (c // V7X_LANES), axis=1)
        pooled = win * inv - hc.astype(F32)
        y = _dot(pooled.astype(BF16), w_ref[...].astype(BF16)) * ps_ref[...]
        gate = m_ref[...][k_gate:k_gate + 1]
        o_ref[...] = x_ref[...] + gate * y

    last_halo = m // POOL_HALO - 1
    in_specs = [
        pl.BlockSpec((POOL_HALO, c), lambda g, i: (jnp.maximum(i * halo_per_tile - 1, 0), g)),
        pl.BlockSpec((r, c), lambda g, i: (i, g)),
        pl.BlockSpec((POOL_HALO, c), lambda g, i: (jnp.minimum((i + 1) * halo_per_tile, last_halo), g)),
        pl.BlockSpec((None, None, r, kc), lambda g, i: (g, i % period, 0, 0)),
        pl.BlockSpec((None, r, V7X_LANES), lambda g, i: (g, i % period, 0)),
        pl.BlockSpec((None, None, c, c), lambda g, i: (layer, g, 0, 0)),
        pl.BlockSpec((1, c), lambda g, i: (0, g)),
        pl.BlockSpec((r, c), lambda g, i: (i, g)),
        pl.BlockSpec((None, N_MOD, c), lambda g, i: ((i * r) // rows_per_cond, 0, g)),
    ]
    return _call(body, name="pool_mixer", grid=(n_groups, nt), in_specs=in_specs,
                 out_specs=pl.BlockSpec((r, c), lambda g, i: (i, g)),
                 out_shape=jax.ShapeDtypeStruct((m, d), F32),
                 args=[h, h, h, member, inv_len, w_pool, pool_scale.reshape(1, d), x, mods],
                 temp_bytes=4 * _nbytes((r, c), F32))


def _pool_tables(seq, r, period, kc):
    p = jnp.arange(period)[:, None, None]
    row = p * r + jnp.arange(r)[None, :, None]
    col = p * r - POOL_HALO + jnp.arange(kc)[None, None, :]
    seq_start = (row // seq) * seq
    members, invs = [], []
    for w in POOL_WINDOWS:
        lo = jnp.maximum(row - w // 2, seq_start)
        hi = jnp.minimum(row + w - w // 2, seq_start + seq)
        members.append(((col >= lo) & (col < hi)).astype(BF16))
        invs.append(jnp.broadcast_to(1.0 / (hi - lo).astype(F32), (period, r, V7X_LANES)))
    inv = jnp.stack(invs).reshape(len(POOL_WINDOWS), period * r, V7X_LANES)
    return jnp.stack(members), inv


def _ffn_up(h, w_up, conv_w, conv_b, layer, grp, w_down=None):
    m, d = h.shape
    f = w_up.shape[2] // 2
    tm = _tile(m, max(grp.seq, 2048), grp.seq)
    tn = _tile(f, 256, V7X_LANES)
    nj = f // tn
    seq = grp.seq
    steps = (m // tm) * nj
    fuse_cast = w_down is not None and w_down.shape[1] % (steps * V7X_SUBLANES_BF16) == 0
    slab_rows = w_down.shape[1] // steps if fuse_cast else 0

    rc = _tile(seq, 512)
    pad = 8
    rs = _tile(tm, 512, V7X_SUBLANES_BF16)

    def body(*refs):
        h_ref, wa_ref, wb_ref, cwa_ref, cwb_ref, cba_ref, cbb_ref = refs[:7]
        if fuse_cast:
            wd_ref, o_ref, wdo_ref, acc_ref = refs[7:]
            wdo_ref[...] = wd_ref[...].astype(BF16)
        else:
            o_ref, acc_ref = refs[7:]
        row8 = lax.broadcasted_iota(jnp.int32, (8, tn), 0)

        def matmul(s):
            xs = h_ref[s * rs:(s + 1) * rs, :]
            for k, w_ref in enumerate((wa_ref, wb_ref)):
                acc_ref[k, pad + s * rs:pad + (s + 1) * rs] = _dot(xs, w_ref[...].astype(BF16))

        def piece(p0, p1):
            n = p1 - p0

            def conv(k, cw_ref, cb_ref):
                cw = cw_ref[...]
                u = acc_ref[k, pad + p0:pad + p1]
                prev = acc_ref[k, pad + p0 - 1:pad + p1 - 1]
                nxt = acc_ref[k, pad + p0 + 1:pad + p1 + 1]
                top = jnp.where(row8 == 0, 0.0, prev[:8]) if p0 % seq == 0 else prev[:8]
                prev = top if n == 8 else jnp.concatenate([top, prev[8:]], axis=0)
                bot = jnp.where(row8 == 7, 0.0, nxt[n - 8:]) if p1 % seq == 0 else nxt[n - 8:]
                nxt = bot if n == 8 else jnp.concatenate([nxt[:n - 8], bot], axis=0)
                return prev * cw[0:1] + u * cw[1:2] + nxt * cw[2:3] + cb_ref[...]

            a = conv(0, cwa_ref, cba_ref)
            b = conv(1, cwb_ref, cbb_ref)
            o_ref[p0:p1] = (_silu(a) * b).astype(o_ref.dtype)

        def epilogue(r0, r1):
            cuts = sorted({r0, r1} | {c for c in range(0, tm + 1, rc) if r0 < c < r1})
            for p0, p1 in zip(cuts[:-1], cuts[1:]):
                piece(p0, p1)

        n_stage = tm // rs
        for k in range(2):
            acc_ref[k, 0:pad] = jnp.zeros((pad, tn), F32)
            acc_ref[k, pad + tm:2 * pad + tm] = jnp.zeros((pad, tn), F32)
        for s in range(n_stage + 1):
            if s > 0:
                epilogue(max((s - 1) * rs - pad, 0), tm if s == n_stage else s * rs - pad)
            if s < n_stage:
                matmul(s)

    col = lambda i, j: jnp.where(i % 2 == 0, j, nj - 1 - j)
    in_specs = [
        pl.BlockSpec((tm, d), lambda i, j: (i, 0)),
        pl.BlockSpec((None, d, tn), lambda i, j: (layer, 0, col(i, j))),
        pl.BlockSpec((None, d, tn), lambda i, j: (layer, 0, nj + col(i, j))),
        pl.BlockSpec((None, CONV_W, tn), lambda i, j: (layer, 0, col(i, j))),
        pl.BlockSpec((None, CONV_W, tn), lambda i, j: (layer, 0, nj + col(i, j))),
        pl.BlockSpec((None, 1, tn), lambda i, j: (layer, 0, col(i, j))),
        pl.BlockSpec((None, 1, tn), lambda i, j: (layer, 0, nj + col(i, j))),
    ]
    cb = conv_b.reshape(conv_b.shape[0], 1, 2 * f)
    args = [h, w_up, w_up, conv_w, conv_w, cb, cb]
    out_specs = [pl.BlockSpec((tm, tn), lambda i, j: (i, col(i, j)))]
    out_shape = [jax.ShapeDtypeStruct((m, f), BF16)]
    if fuse_cast:
        slab = (slab_rows, w_down.shape[2])
        in_specs.append(pl.BlockSpec((None,) + slab, lambda i, j: (layer, i * nj + j, 0)))
        args.append(w_down)
        out_specs.append(pl.BlockSpec(slab, lambda i, j: (i * nj + j, 0)))
        out_shape.append(jax.ShapeDtypeStruct(w_down.shape[1:], BF16))
    res = _call(body, name="ffn_up", grid=(m // tm, nj), in_specs=in_specs, out_specs=out_specs,
                out_shape=out_shape, args=args,
                temp_bytes=10 * _nbytes((rc, tn), F32) + 4 * _nbytes((rs, tn), F32) + 2 * _nbytes((d, tn), BF16),
                scratch=[((2, tm + 2 * pad, tn), F32)])
    return res[0], (res[1] if fuse_cast else None)


def _residual_proj(name, lhs, rhs, x, mods, k_gate, *, tm, tn):
    m = x.shape[0]
    rows_per_cond = m // mods.shape[0]
    assert rows_per_cond % tm == 0
    extras = [(x, (tm, tn), lambda i, j: (i, j)),
              (mods, (None, N_MOD, tn), lambda i, j: ((i * tm) // rows_per_cond, 0, j))]
    return _proj(name, lhs, rhs, _ep_residual(k_gate), (F32,), tm=tm, tn=tn, extras=extras)[0]


class _AttnWeights(NamedTuple):
    qa: _W
    ka: _W
    va: _W
    cq: _W
    ckv: _W
    kr: _W
    g_cq: jax.Array
    g_ckv: jax.Array
    q_up: _W
    kn: _W
    vb: _W
    o_a: _W
    o_b: _W
    lam_vecs: jax.Array


def _prep_attn_weights(j, w_in_b, w_in, g_cq, w_q_up, g_ckv, w_kv_up, w_o, lam_vecs):
    d = w_in.shape[0]
    a_q = A_HEADS * A_HEAD_DIM
    a_v = A_HEADS * A_V_DIM
    b_out = B_HEADS * B_V_DIM
    assert a_v == b_out
    q_lora = g_cq.shape[0]
    kv_lora = g_ckv.shape[0]
    o0, o1, o2 = a_q, 2 * a_q, 2 * a_q + a_v
    o3, o4 = o2 + q_lora, o2 + q_lora + kv_lora
    kr = jnp.pad(w_in[:, o4:], ((0, 0), (0, MLA_KR_PAD - ROPE_DIM)))
    q_up = w_q_up.reshape(q_lora, B_HEADS, NOPE_DIM + ROPE_DIM)
    q_up = jnp.pad(q_up, ((0, 0), (0, 0), (0, MLA_Q_PAD - NOPE_DIM - ROPE_DIM)))
    kv = w_kv_up.reshape(kv_lora, B_HEADS, NOPE_DIM + B_V_DIM)
    c = lambda w: _whole(w.astype(BF16))
    win = lambda col0, n: _W(w_in_b, j, d, 0, col0, n)
    return _AttnWeights(
        qa=win(0, a_q), ka=win(o0, a_q), va=win(o1, a_v), cq=win(o2, q_lora), ckv=win(o3, kv_lora),
        kr=c(kr), g_cq=g_cq.reshape(1, q_lora), g_ckv=g_ckv.reshape(1, kv_lora),
        q_up=c(q_up.reshape(q_lora, B_HEADS * MLA_Q_PAD)),
        kn=c(kv[:, :, :NOPE_DIM].reshape(kv_lora, B_HEADS * NOPE_DIM)),
        vb=c(kv[:, :, NOPE_DIM:].reshape(kv_lora, B_HEADS * B_V_DIM)),
        o_a=_W(w_o, j, a_v, 0, 0, w_o.shape[-1]), o_b=_W(w_o, j, b_out, 1, 0, w_o.shape[-1]),
        lam_vecs=lam_vecs)


def _attn_mixer(h, x, mods, w, grp, lam_init, cache):
    m, d = h.shape
    hps = A_HEADS if grp.seq <= 256 else 4
    qk_scale = A_QK_HALF ** -0.5 * LOG2_E
    mla_scale = (NOPE_DIM + ROPE_DIM) ** -0.5 * LOG2_E
    rows_unit = grp.seq if grp.rope else m

    def tiles(lhs, rhs, heavy_epilogue):
        k, n = rhs.k, rhs.n
        pref = 1024 if k < 1024 else (256 if heavy_epilogue else 512)
        return _tile(min(rows_unit, lhs.shape[0]), pref, V7X_SUBLANES_BF16), min(n, 2048)

    def plain(name, lhs, rhs, dtype, scale=1.0):
        tm_, tn_ = tiles(lhs, rhs, False)
        return _proj(name, [lhs], [rhs], _ep_scale(scale), (dtype,), tm=tm_, tn=tn_)[0]

    def roped(name, lhs, rhs, tables, dtype, scale, with_plain):
        tm_, tn_ = tiles(lhs, rhs, True)
        tiles_per_seq = grp.seq // tm_
        extras = [(t, (tm_, t.shape[1]), lambda i, j: (i % tiles_per_seq, 0)) for t in tables]
        outs = (dtype, dtype) if with_plain else (dtype,)
        return _proj(name, [lhs], [rhs], _ep_rope(scale, with_plain), outs, tm=tm_, tn=tn_, extras=extras)

    def normed(name, rhs, gain, dtype):
        tm_, _ = tiles(h, rhs, False)
        n = rhs.n
        return _proj(name, [h], [rhs], _ep_rmsnorm, (dtype,), tm=tm_, tn=n,
                     extras=[(gain, (1, n), lambda i, j: (0, 0))])[0]

    cqn = normed("cq_norm", w.cq, w.g_cq, BF16)
    ckvn = normed("ckv_norm", w.ckv, w.g_ckv, F32)
    kn = plain("k_nope", ckvn, w.kn, BF16)
    vb = plain("v_mla", ckvn, w.vb, BF16)

    if not grp.rope:
        qa = plain("q_diff", h, w.qa, BF16, qk_scale)
        ka = plain("k_diff", h, w.ka, F32)
        va = plain("v_diff", h, w.va, F32)
        kr = plain("k_rope", h, w.kr, F32)
        q = plain("q_mla", cqn, w.q_up, BF16, mla_scale)
        oa = _diff_attention(w.lam_vecs, lam_init, [qa], [ka], [va], grp, hps)
        ob = _mla_attention([q], [kn], [kr], [vb], grp, hps)
        state = (ka, va, ckvn, kr[:, :ROPE_DIM])
    else:
        cos, sin = _rope_tables(grp.seq)
        t_diff = _widen_tables(cos, sin, 0, 0, A_HEAD_DIM)
        t_mla = _widen_tables(cos, sin, NOPE_DIM, MLA_Q_PAD - NOPE_DIM - ROPE_DIM, MLA_Q_PAD)
        t_kr = _widen_tables(cos, sin, 0, MLA_KR_PAD - ROPE_DIM, MLA_KR_PAD)
        ka_c, va_c, ckv_c, kr_c = cache
        qa_u, qa_r = roped("q_diff", h, w.qa, t_diff, BF16, qk_scale, True)
        ka_r, = roped("k_diff", h, w.ka, t_diff, BF16, 1.0, False)
        va = plain("v_diff", h, w.va, BF16)
        kr_r, = roped("k_rope", h, w.kr, t_kr, BF16, 1.0, False)
        q_u, q_r = roped("q_mla", cqn, w.q_up, t_mla, BF16, mla_scale, True)
        kn_c = plain("k_nope_ctx", ckv_c, w.kn, BF16)
        vb_c = plain("v_mla_ctx", ckv_c, w.vb, BF16)
        oa = _diff_attention(w.lam_vecs, lam_init, [qa_u, qa_r], [ka_c, ka_r], [va_c, va], grp, hps)
        ob = _mla_attention([q_u, q_r], [kn_c, kn], [kr_c, kr_r], [vb_c, vb], grp, hps)
        state = None
    x1 = _residual_proj("o_proj", [oa, ob], [w.o_a, w.o_b], x, mods, 2,
                        tm=_tile(rows_unit, 1024, V7X_SUBLANES_BF16), tn=512)
    return x1, state


def _conv_ffn(x, mods, g2, w_up, conv_w, conv_b, w_down, w_down_b, layer, grp):
    h = _norm(x, g2, mods, 3, 4, BF16)
    act, made = _ffn_up(h, w_up, conv_w, conv_b, layer, grp, None if w_down_b is not None else w_down)
    if w_down_b is None:
        w_down_b = made if made is not None else w_down[layer].astype(BF16)
    out = _residual_proj("ffn_down", [act], [_whole(w_down_b)], x, mods, 5,
                         tm=_tile(x.shape[0] // mods.shape[0], 512, V7X_SUBLANES_BF16), tn=512)
    return out, w_down_b


def kernel(x_prompt, x_sample, cache_diff_k, cache_diff_v, cache_mla_ckv, cache_mla_krope, c, c_ctx,
           norm1_g, norm2_g, w_mod, b_mod, w_in, g_cq, w_q_up, g_ckv, w_kv_up,
           lambda_q1, lambda_k1, lambda_q2, lambda_k2, w_o, w_pool, pool_scale,
           w_up, conv_w, conv_b, w_down, g_final):
    batch, seq, d = x_prompt.shape
    dec_batch, dec_seq, _ = x_sample.shape
    depth = w_mod.shape[0]
    past = cache_diff_k.shape[2]
    groups = (_Group(batch, seq, False), _Group(dec_batch, dec_seq, True))

    n_cond = 1 + dec_batch
    cond8 = jnp.zeros((8, d), F32).at[0].set(c_ctx).at[1:n_cond].set(c)
    mods_all = _adaln(cond8, w_mod, b_mod).reshape(depth, 8, N_MOD, d)
    mods_g = (mods_all[:, 0:1], mods_all[:, 1:n_cond])

    w_in_b = w_in.astype(BF16)
    attn_w = {}
    for i in range(0, depth, 2):
        j = i // 2
        lam_vecs = jnp.stack([lambda_q1[j], lambda_k1[j], lambda_q2[j], lambda_k2[j]])
        attn_w[i] = _prep_attn_weights(j, w_in_b, w_in[j], g_cq[j], w_q_up[j], g_ckv[j], w_kv_up[j],
                                       w_o, lam_vecs)
    w_down_b = [None] * depth

    xs = [x_prompt.reshape(batch * seq, d), x_sample.reshape(dec_batch * dec_seq, d)]
    states = []
    for gi, grp in enumerate(groups):
        x = xs[gi]
        for i in range(depth):
            mods = mods_g[gi][i]
            h = _norm(x, norm1_g[i], mods, 0, 1, BF16)
            if i % 2 == 0:
                j = i // 2
                lam_init = 0.8 - 0.6 * math.exp(-0.3 * i)
                cache = None
                if grp.rope:
                    kr_c = jnp.pad(cache_mla_krope[:, j].reshape(dec_batch * past, ROPE_DIM),
                                   ((0, 0), (0, MLA_KR_PAD - ROPE_DIM)))
                    cache = (cache_diff_k[:, j].reshape(dec_batch * past, A_HEADS * A_HEAD_DIM),
                             cache_diff_v[:, j].reshape(dec_batch * past, A_HEADS * A_V_DIM),
                             cache_mla_ckv[:, j].reshape(dec_batch * past, -1), kr_c)
                x, st = _attn_mixer(h, x, mods, attn_w[i], grp, lam_init, cache)
                if st is not None:
                    states.append(st)
            else:
                x = _pool_mixer(h, x, mods, w_pool, i // 2, pool_scale[i // 2], grp, 2)
            x, w_down_b[i] = _conv_ffn(x, mods, norm2_g[i], w_up, conv_w, conv_b, w_down, w_down_b[i], i, grp)
        xs[gi] = _norm(x, g_final, None, 0, 0, F32)

    y_prompt = xs[0].reshape(batch, seq, d)
    y_sample = xs[1].reshape(dec_batch, dec_seq, d)
    stack = lambda k, tail: jnp.stack([s[k].reshape((batch, seq) + tail) for s in states], axis=1)
    new_diff_k = stack(0, (A_HEADS, A_HEAD_DIM))
    new_diff_v = stack(1, (A_HEADS, A_V_DIM))
    new_mla_ckv = stack(2, (g_ckv.shape[-1],))
    new_mla_krope = stack(3, (ROPE_DIM,))
    return (y_prompt, y_sample, new_diff_k, new_diff_v, new_mla_ckv, new_mla_krope)
```

```python
import functools
import math
from typing import NamedTuple

import jax
import jax.numpy as jnp
from jax import lax
from jax.experimental import pallas as pl
from jax.experimental.pallas import tpu as pltpu

F32 = jnp.float32
BF16 = jnp.bfloat16

GRID_W = 64
ROPE_BASE = 10000.0
EPS = 1e-6
A_HEADS = 16
A_QK_HALF = 64
A_HEAD_DIM = 2 * A_QK_HALF
A_V_DIM = 128
B_HEADS = 16
NOPE_DIM = 128
ROPE_DIM = 64
B_V_DIM = 128
POOL_WINDOWS = (2, 4, 8, 16)
N_MOD = 6
CONV_W = 3
LOG2_E = math.log2(math.e)

V7X_VMEM_BYTES = 64 * 1024 * 1024
V7X_LANES = 128
V7X_SUBLANES_BF16 = 16
COMPILER_SCRATCH_BYTES = 2 * 1024 * 1024

MLA_Q_PAD = 256
MLA_KR_PAD = MLA_Q_PAD - NOPE_DIM
ROPE_QUARTER = ROPE_DIM // 4
POOL_HALO = V7X_SUBLANES_BF16


class _Group(NamedTuple):
    n_seq: int
    seq: int
    rope: bool


def _tile(n, pref, mult=8):
    if n <= pref:
        return n
    t = (pref // mult) * mult
    while t >= mult:
        if n % t == 0:
            return t
        t -= mult
    return n


def _nbytes(shape, dtype):
    return math.prod(shape) * jnp.dtype(dtype).itemsize


def _call(body, *, name, grid, in_specs, out_specs, out_shape, args, temp_bytes=0, single_buffered=(),
          scratch=(), semantics=None, flags=None):
    multi = isinstance(out_shape, (list, tuple))
    outs = list(out_shape) if multi else [out_shape]
    ospecs = list(out_specs) if multi else [out_specs]
    total = temp_bytes + COMPILER_SCRATCH_BYTES + sum(_nbytes(s, dt) for s, dt in scratch)
    for k, (a, s) in enumerate(zip(args, in_specs)):
        blk = [1 if b is None else b for b in s.block_shape]
        total += _nbytes(blk, a.dtype) * (1 if k in single_buffered else 2)
    for o, s in zip(outs, ospecs):
        blk = [1 if b is None else b for b in s.block_shape]
        total += 2 * _nbytes(blk, o.dtype)
    limit = min(V7X_VMEM_BYTES - 4 * 1024 * 1024, max(total, 16 * 1024 * 1024))
    return pl.pallas_call(
        body, name=name, grid=grid, in_specs=in_specs, out_specs=out_specs, out_shape=out_shape,
        scratch_shapes=[pltpu.VMEM(s, dt) for s, dt in scratch],
        compiler_params=pltpu.CompilerParams(
            dimension_semantics=semantics or ("parallel",) * len(grid), vmem_limit_bytes=int(limit),
            flags=flags),
    )(*args)


def _dot(a, b):
    return jnp.dot(a, b, preferred_element_type=F32)


def _dot_nt(a, b):
    return lax.dot_general(a, b, (((1,), (1,)), ((), ())), preferred_element_type=F32)


def _silu(x):
    return x / (1.0 + jnp.exp(-x))


def _adaln(cond8, w_mod, b_mod):
    depth, d, n = w_mod.shape
    tn = next(t for t in (512, 256, 128) if n % (2 * t) == 0)

    def body(c_ref, wa_ref, wb_ref, b_ref, o_ref):
        s = _silu(c_ref[...]).astype(BF16)
        ya = _dot(s, wa_ref[...].astype(BF16))
        yb = _dot(s, wb_ref[...].astype(BF16))
        o_ref[...] = jnp.concatenate([ya, yb], axis=1) + b_ref[...]

    return _call(
        body, name="adaln", grid=(depth, n // (2 * tn)),
        in_specs=[pl.BlockSpec((8, d), lambda l, j: (0, 0)),
                  pl.BlockSpec((None, d, tn), lambda l, j: (l, 0, 2 * j)),
                  pl.BlockSpec((None, d, tn), lambda l, j: (l, 0, 2 * j + 1)),
                  pl.BlockSpec((None, 1, 2 * tn), lambda l, j: (l, 0, j))],
        out_specs=pl.BlockSpec((None, 8, 2 * tn), lambda l, j: (l, 0, j)),
        out_shape=jax.ShapeDtypeStruct((depth, 8, n), F32),
        args=(cond8, w_mod, w_mod, b_mod.reshape(depth, 1, n)),
        temp_bytes=2 * _nbytes((d, tn), BF16) + 2 * _nbytes((d, tn), F32))


def _norm(x, g, mods, k_shift, k_scale, out_dtype):
    m, d = x.shape
    modulated = mods is not None
    tm = _tile(m // mods.shape[0] if modulated else m, 512)

    def body(*refs):
        x_ref, g_ref = refs[0], refs[1]
        o_ref = refs[-1]
        xf = x_ref[...]
        y = xf * lax.rsqrt(jnp.mean(xf * xf, axis=-1, keepdims=True) + EPS) * g_ref[...]
        if modulated:
            mm = refs[2][...]
            y = y * (1.0 + mm[k_scale:k_scale + 1]) + mm[k_shift:k_shift + 1]
        o_ref[...] = y.astype(o_ref.dtype)

    in_specs = [pl.BlockSpec((tm, d), lambda i: (i, 0)), pl.BlockSpec((1, d), lambda i: (0, 0))]
    args = [x, g.reshape(1, d)]
    if modulated:
        rows_per_cond = m // mods.shape[0]
        in_specs.append(pl.BlockSpec((None, N_MOD, d), lambda i: ((i * tm) // rows_per_cond, 0, 0)))
        args.append(mods)
    return _call(body, name="norm", grid=(m // tm,), in_specs=in_specs,
                 out_specs=pl.BlockSpec((tm, d), lambda i: (i, 0)),
                 out_shape=jax.ShapeDtypeStruct((m, d), out_dtype), args=args,
                 temp_bytes=3 * _nbytes((tm, d), F32))


class _W(NamedTuple):
    arr: jax.Array
    layer: object
    k: int
    row_blk: int
    col0: int
    n: int

    def spec(self, tn, col=lambda i, j: j, **mode):
        assert self.col0 % tn == 0 and self.n % tn == 0
        c0 = self.col0 // tn
        if self.layer is None:
            return pl.BlockSpec((self.k, tn), lambda i, j: (self.row_blk, c0 + col(i, j)), **mode)
        return pl.BlockSpec((None, self.k, tn),
                            lambda i, j: (self.layer, self.row_blk, c0 + col(i, j)), **mode)


def _whole(arr):
    return _W(arr, None, arr.shape[0], 0, 0, arr.shape[1])


def _proj(name, lhs, rhs, epilogue, outs, *, tm, tn, extras=()):
    m = lhs[0].shape[0]
    n = rhs[0].n
    np_ = len(lhs)
    ne = len(extras)

    def body(*refs):
        acc = None
        for p in range(np_):
            part = _dot(refs[p][...].astype(BF16), refs[np_ + p][...].astype(BF16))
            acc = part if acc is None else acc + part
        epilogue(acc, refs[2 * np_:2 * np_ + ne], refs[2 * np_ + ne:])

    resident = n == tn
    rhs_mode = dict(pipeline_mode=pl.Buffered(1)) if resident else {}
    nj = n // tn
    col = (lambda i, j: j) if nj == 1 else (lambda i, j: jnp.where(i % 2 == 0, j, nj - 1 - j))
    in_specs = ([pl.BlockSpec((tm, a.shape[1]), lambda i, j: (i, 0)) for a in lhs]
                + [w.spec(tn, col, **rhs_mode) for w in rhs]
                + [pl.BlockSpec(bs, lambda i, j, im=im: im(i, col(i, j))) for (_, bs, im) in extras])
    args = list(lhs) + [w.arr for w in rhs] + [e[0] for e in extras]
    out_shape = [jax.ShapeDtypeStruct((m, n), dt) for dt in outs]
    out_specs = [pl.BlockSpec((tm, tn), lambda i, j: (i, col(i, j))) for _ in outs]
    return _call(body, name=name, grid=(m // tm, nj), in_specs=in_specs, out_specs=out_specs,
                 out_shape=out_shape, args=args, temp_bytes=6 * _nbytes((tm, tn), F32),
                 single_buffered=tuple(range(np_, 2 * np_)) if resident else ())


def _ep_scale(scale):
    def ep(acc, ex, outs):
        outs[0][...] = (acc * scale).astype(outs[0].dtype)
    return ep


def _rotate(a, cos, sin):
    width = a.shape[1]
    lane = lax.broadcasted_iota(jnp.int32, a.shape, 1)
    low = (lane & (2 * ROPE_QUARTER - 1)) < ROPE_QUARTER
    partner = jnp.where(low, pltpu.roll(a, width - ROPE_QUARTER, 1), pltpu.roll(a, ROPE_QUARTER, 1))
    return a * cos + partner * sin


def _ep_rope(scale, with_plain):
    def ep(acc, ex, outs):
        a = acc * scale
        reps = a.shape[1] // ex[0].shape[1]
        cos = jnp.concatenate([ex[0][...]] * reps, axis=1)
        sin = jnp.concatenate([ex[1][...]] * reps, axis=1)
        r = _rotate(a, cos, sin)
        if with_plain:
            outs[0][...] = a.astype(outs[0].dtype)
        outs[-1][...] = r.astype(outs[-1].dtype)
    return ep


def _ep_residual(k_gate):
    def ep(acc, ex, outs):
        gate = ex[1][...][k_gate:k_gate + 1]
        outs[0][...] = ex[0][...] + gate * acc
    return ep


def _rope_tables(seq):
    half = ROPE_DIM // 2
    inv = ROPE_BASE ** (-jnp.arange(0, half, 2, dtype=F32) / half)
    pos = jnp.arange(seq)
    ar = (pos // GRID_W).astype(F32)[:, None] * inv[None, :]
    ac = (pos % GRID_W).astype(F32)[:, None] * inv[None, :]
    cos = jnp.concatenate([jnp.cos(ar), jnp.cos(ar), jnp.cos(ac), jnp.cos(ac)], axis=1)
    sin = jnp.concatenate([-jnp.sin(ar), jnp.sin(ar), -jnp.sin(ac), jnp.sin(ac)], axis=1)
    return cos, sin


def _widen_tables(cos, sin, lead, trail, width):
    seq = cos.shape[0]
    c = jnp.concatenate([jnp.ones((seq, lead), F32), cos, jnp.ones((seq, trail), F32)], axis=1)
    s = jnp.concatenate([jnp.zeros((seq, lead), F32), sin, jnp.zeros((seq, trail), F32)], axis=1)
    reps = width // c.shape[1]
    assert reps * c.shape[1] == width
    return jnp.tile(c, (1, reps)), jnp.tile(s, (1, reps))


def _diff_attention(lam_vecs, lam_init, qs, ks, vs, grp, hps):
    m = qs[0].shape[0]
    nseg = len(qs)
    tq = _tile(grp.seq, 256)
    qt = grp.seq // tq
    width = hps * A_HEAD_DIM
    n_hg = A_HEADS // hps
    k_lens = [k.shape[0] // grp.n_seq for k in ks]

    def body(*refs):
        lam_ref = refs[0]
        q_refs = refs[1:1 + nseg]
        k_refs = refs[1 + nseg:1 + 2 * nseg]
        v_refs = refs[1 + 2 * nseg:1 + 3 * nseg]
        o_ref = refs[-1]
        lv = lam_ref[...]
        lam = (jnp.exp(jnp.sum(lv[0:1] * lv[1:2], axis=-1, keepdims=True))
               - jnp.exp(jnp.sum(lv[2:3] * lv[3:4], axis=-1, keepdims=True)) + lam_init)
        first = lax.broadcasted_iota(jnp.int32, (tq, A_HEAD_DIM), 1) < A_QK_HALF
        for g in range(hps):
            cols = slice(g * A_HEAD_DIM, (g + 1) * A_HEAD_DIM)
            scores = []
            for q_ref, k_ref in zip(q_refs, k_refs):
                q = q_ref[:, cols]
                zero = jnp.zeros_like(q)
                q2 = jnp.concatenate([jnp.where(first, q, zero), jnp.where(first, zero, q)], axis=0)
                scores.append(_dot_nt(q2, k_ref[:, cols].astype(BF16)))
            mx = functools.reduce(jnp.maximum, [jnp.max(s, axis=-1, keepdims=True) for s in scores])
            ps = [jnp.exp2(s - mx) for s in scores]
            den = functools.reduce(jnp.add, [jnp.sum(p, axis=-1, keepdims=True) for p in ps])
            o2 = functools.reduce(
                jnp.add, [_dot(p.astype(BF16), v_ref[:, cols].astype(BF16)) for p, v_ref in zip(ps, v_refs)])
            o2 = o2 / den
            o = o2[:tq] - lam * o2[tq:]
            o = o * lax.rsqrt(jnp.mean(o * o, axis=-1, keepdims=True) + EPS) * (1.0 - lam_init)
            o_ref[:, cols] = o.astype(o_ref.dtype)

    q_spec = pl.BlockSpec((tq, width), lambda b, h, i: (b * qt + i, h))
    in_specs = ([pl.BlockSpec(lam_vecs.shape, lambda b, h, i: (0, 0))]
                + [q_spec] * nseg
                + [pl.BlockSpec((kl, width), lambda b, h, i: (b, h)) for kl in k_lens] * 2)
    nk = sum(k_lens)
    return _call(body, name="diff_attention", grid=(grp.n_seq, n_hg, qt), in_specs=in_specs,
                 out_specs=q_spec, out_shape=jax.ShapeDtypeStruct((m, A_HEADS * A_V_DIM), BF16),
                 args=[lam_vecs] + list(qs) + list(ks) + list(vs),
                 temp_bytes=4 * _nbytes((2 * tq, nk), F32))


def _mla_attention(qs, kvs, krs, grp, hps):
    m = qs[0].shape[0]
    nseg = len(qs)
    tq = _tile(grp.seq, 256)
    qt = grp.seq // tq
    n_hg = B_HEADS // hps
    k_lens = [kv.shape[0] // grp.n_seq for kv in kvs]

    def body(*refs):
        q_refs = refs[:nseg]
        kn_refs = refs[nseg:2 * nseg]
        kr_refs = refs[2 * nseg:3 * nseg]
        v_refs = refs[3 * nseg:4 * nseg]
        o_ref = refs[-1]
        krs_v = [r[...].astype(BF16) for r in kr_refs]
        for g in range(hps):
            kcols = slice(g * NOPE_DIM, (g + 1) * NOPE_DIM)
            vcols = slice(g * B_V_DIM, (g + 1) * B_V_DIM)
            scores = []
            for q_ref, kn_ref, kr in zip(q_refs, kn_refs, krs_v):
                q = q_ref[:, g * MLA_Q_PAD:(g + 1) * MLA_Q_PAD]
                k = jnp.concatenate([kn_ref[:, kcols], kr], axis=1)
                scores.append(_dot_nt(q, k))
            mx = functools.reduce(jnp.maximum, [jnp.max(s, axis=-1, keepdims=True) for s in scores])
            ps = [jnp.exp2(s - mx) for s in scores]
            den = functools.reduce(jnp.add, [jnp.sum(p, axis=-1, keepdims=True) for p in ps])
            o = functools.reduce(
                jnp.add, [_dot(p.astype(BF16), v_ref[:, vcols]) for p, v_ref in zip(ps, v_refs)])
            o_ref[:, vcols] = (o / den).astype(o_ref.dtype)

    q_spec = pl.BlockSpec((tq, hps * MLA_Q_PAD), lambda b, h, i: (b * qt + i, h))
    assert NOPE_DIM == B_V_DIM
    k_specs = [pl.BlockSpec((kl, hps * NOPE_DIM), lambda b, h, i: (b, h)) for kl in k_lens]
    v_specs = [pl.BlockSpec((kl, hps * B_V_DIM), lambda b, h, i: (b, n_hg + h)) for kl in k_lens]
    kr_specs = [pl.BlockSpec((kl, MLA_KR_PAD), lambda b, h, i: (b, 0)) for kl in k_lens]
    nk = sum(k_lens)
    return _call(body, name="mla_attention", grid=(grp.n_seq, n_hg, qt),
                 in_specs=[q_spec] * nseg + k_specs + kr_specs + v_specs,
                 out_specs=pl.BlockSpec((tq, hps * B_V_DIM), lambda b, h, i: (b * qt + i, h)),
                 out_shape=jax.ShapeDtypeStruct((m, B_HEADS * B_V_DIM), BF16),
                 args=list(qs) + list(kvs) + list(krs) + list(kvs),
                 temp_bytes=4 * _nbytes((tq, nk), F32) + 2 * _nbytes((nk, MLA_Q_PAD), BF16))


def _pool_mixer(h, x, mods, w_pool, layer, pool_scale, grp, k_gate):
    m, d = x.shape
    _, n_groups, c, _ = w_pool.shape
    assert max(POOL_WINDOWS) // 2 <= POOL_HALO and n_groups == len(POOL_WINDOWS)
    rows_per_cond = m // mods.shape[0]
    r = _tile(rows_per_cond, 512, POOL_HALO)
    nt = m // r
    halo_per_tile = r // POOL_HALO
    kc = r + 2 * POOL_HALO
    period = math.lcm(r, grp.seq) // r
    member, inv_len = _pool_tables(grp.seq, r, period, kc)

    def body(hp_ref, hc_ref, hn_ref, mem_ref, inv_ref, w_ref, ps_ref, x_ref, m_ref, o_ref):
        hc = hc_ref[...]
        hcat = jnp.concatenate([hp_ref[...], hc, hn_ref[...]], axis=0)
        win = _dot(mem_ref[...], hcat)
        inv = jnp.concatenate([inv_ref[...]] * (c // V7X_LANES), axis=1)
        pooled = win * inv - hc.astype(F32)
        y = _dot(pooled.astype(BF16), w_ref[...].astype(BF16)) * ps_ref[...]
        gate = m_ref[...][k_gate:k_gate + 1]
        o_ref[...] = x_ref[...] + gate * y

    last_halo = m // POOL_HALO - 1
    in_specs = [
        pl.BlockSpec((POOL_HALO, c), lambda g, i: (jnp.maximum(i * halo_per_tile - 1, 0), g)),
        pl.BlockSpec((r, c), lambda g, i: (i, g)),
        pl.BlockSpec((POOL_HALO, c), lambda g, i: (jnp.minimum((i + 1) * halo_per_tile, last_halo), g)),
        pl.BlockSpec((None, None, r, kc), lambda g, i: (g, i % period, 0, 0)),
        pl.BlockSpec((None, r, V7X_LANES), lambda g, i: (g, i % period, 0)),
        pl.BlockSpec((None, None, c, c), lambda g, i: (layer, g, 0, 0)),
        pl.BlockSpec((1, c), lambda g, i: (0, g)),
        pl.BlockSpec((r, c), lambda g, i: (i, g)),
        pl.BlockSpec((None, N_MOD, c), lambda g, i: ((i * r) // rows_per_cond, 0, g)),
    ]
    return _call(body, name="pool_mixer", grid=(n_groups, nt), in_specs=in_specs,
                 out_specs=pl.BlockSpec((r, c), lambda g, i: (i, g)),
                 out_shape=jax.ShapeDtypeStruct((m, d), F32),
                 args=[h, h, h, member, inv_len, w_pool, pool_scale.reshape(1, d), x, mods],
                 temp_bytes=4 * _nbytes((r, c), F32))


def _pool_tables(seq, r, period, kc):
    p = jnp.arange(period)[:, None, None]
    row = p * r + jnp.arange(r)[None, :, None]
    col = p * r - POOL_HALO + jnp.arange(kc)[None, None, :]
    seq_start = (row // seq) * seq
    members, invs = [], []
    for w in POOL_WINDOWS:
        lo = jnp.maximum(row - w // 2, seq_start)
        hi = jnp.minimum(row + w - w // 2, seq_start + seq)
        members.append(((col >= lo) & (col < hi)).astype(BF16))
        invs.append(jnp.broadcast_to(1.0 / (hi - lo).astype(F32), (period, r, V7X_LANES)))
    inv = jnp.stack(invs).reshape(len(POOL_WINDOWS), period * r, V7X_LANES)
    return jnp.stack(members), inv


def _ffn_up(h, w_up, conv_w, conv_b, layer, grp, w_down=None):
    m, d = h.shape
    f = w_up.shape[2] // 2
    tm = _tile(m, max(grp.seq, 2048), grp.seq)
    tn = _tile(f, 256, V7X_LANES)
    nj = f // tn
    seq = grp.seq
    steps = (m // tm) * nj
    fuse_cast = w_down is not None and w_down.shape[1] % (steps * V7X_SUBLANES_BF16) == 0
    slab_rows = w_down.shape[1] // steps if fuse_cast else 0

    rc = _tile(seq, 512)
    pad = 8
    rs = _tile(tm, 512, V7X_SUBLANES_BF16)

    def body(*refs):
        h_ref, wa_ref, wb_ref, cwa_ref, cwb_ref, cba_ref, cbb_ref = refs[:7]
        if fuse_cast:
            wd_ref, o_ref, wdo_ref, acc_ref = refs[7:]
            wdo_ref[...] = wd_ref[...].astype(BF16)
        else:
            o_ref, acc_ref = refs[7:]
        row8 = lax.broadcasted_iota(jnp.int32, (8, tn), 0)

        def matmul(s):
            xs = h_ref[s * rs:(s + 1) * rs, :]
            for k, w_ref in enumerate((wa_ref, wb_ref)):
                acc_ref[k, pad + s * rs:pad + (s + 1) * rs] = _dot(xs, w_ref[...].astype(BF16))

        def piece(p0, p1):
            n = p1 - p0

            def conv(k, cw_ref, cb_ref):
                cw = cw_ref[...]
                u = acc_ref[k, pad + p0:pad + p1]
                prev = acc_ref[k, pad + p0 - 1:pad + p1 - 1]
                nxt = acc_ref[k, pad + p0 + 1:pad + p1 + 1]
                top = jnp.where(row8 == 0, 0.0, prev[:8]) if p0 % seq == 0 else prev[:8]
                prev = top if n == 8 else jnp.concatenate([top, prev[8:]], axis=0)
                bot = jnp.where(row8 == 7, 0.0, nxt[n - 8:]) if p1 % seq == 0 else nxt[n - 8:]
                nxt = bot if n == 8 else jnp.concatenate([nxt[:n - 8], bot], axis=0)
                return prev * cw[0:1] + u * cw[1:2] + nxt * cw[2:3] + cb_ref[...]

            a = conv(0, cwa_ref, cba_ref)
            b = conv(1, cwb_ref, cbb_ref)
            o_ref[p0:p1] = (_silu(a) * b).astype(o_ref.dtype)

        def epilogue(r0, r1):
            cuts = sorted({r0, r1} | {c for c in range(0, tm + 1, rc) if r0 < c < r1})
            for p0, p1 in zip(cuts[:-1], cuts[1:]):
                piece(p0, p1)

        n_stage = tm // rs
        for k in range(2):
            acc_ref[k, 0:pad] = jnp.zeros((pad, tn), F32)
            acc_ref[k, pad + tm:2 * pad + tm] = jnp.zeros((pad, tn), F32)
        for s in range(n_stage + 1):
            if s > 0:
                epilogue(max((s - 1) * rs - pad, 0), tm if s == n_stage else s * rs - pad)
            if s < n_stage:
                matmul(s)

    col = lambda i, j: jnp.where(i % 2 == 0, j, nj - 1 - j)
    in_specs = [
        pl.BlockSpec((tm, d), lambda i, j: (i, 0)),
        pl.BlockSpec((None, d, tn), lambda i, j: (layer, 0, col(i, j))),
        pl.BlockSpec((None, d, tn), lambda i, j: (layer, 0, nj + col(i, j))),
        pl.BlockSpec((None, CONV_W, tn), lambda i, j: (layer, 0, col(i, j))),
        pl.BlockSpec((None, CONV_W, tn), lambda i, j: (layer, 0, nj + col(i, j))),
        pl.BlockSpec((None, 1, tn), lambda i, j: (layer, 0, col(i, j))),
        pl.BlockSpec((None, 1, tn), lambda i, j: (layer, 0, nj + col(i, j))),
    ]
    cb = conv_b.reshape(conv_b.shape[0], 1, 2 * f)
    args = [h, w_up, w_up, conv_w, conv_w, cb, cb]
    out_specs = [pl.BlockSpec((tm, tn), lambda i, j: (i, col(i, j)))]
    out_shape = [jax.ShapeDtypeStruct((m, f), BF16)]
    if fuse_cast:
        slab = (slab_rows, w_down.shape[2])
        in_specs.append(pl.BlockSpec((None,) + slab, lambda i, j: (layer, i * nj + j, 0)))
        args.append(w_down)
        out_specs.append(pl.BlockSpec(slab, lambda i, j: (i * nj + j, 0)))
        out_shape.append(jax.ShapeDtypeStruct(w_down.shape[1:], BF16))
    res = _call(body, name="ffn_up", grid=(m // tm, nj), in_specs=in_specs, out_specs=out_specs,
                out_shape=out_shape, args=args,
                temp_bytes=10 * _nbytes((rc, tn), F32) + 4 * _nbytes((rs, tn), F32) + 2 * _nbytes((d, tn), BF16),
                scratch=[((2, tm + 2 * pad, tn), F32)])
    return res[0], (res[1] if fuse_cast else None)


def _residual_proj(name, lhs, rhs, x, mods, k_gate, *, tm, tn):
    m = x.shape[0]
    rows_per_cond = m // mods.shape[0]
    assert rows_per_cond % tm == 0
    extras = [(x, (tm, tn), lambda i, j: (i, j)),
              (mods, (None, N_MOD, tn), lambda i, j: ((i * tm) // rows_per_cond, 0, j))]
    return _proj(name, lhs, rhs, _ep_residual(k_gate), (F32,), tm=tm, tn=tn, extras=extras)[0]


class _AttnWeights(NamedTuple):
    qa: _W
    ka: _W
    va: _W
    lat: jax.Array
    g_cq: jax.Array
    g_ckv: jax.Array
    q_up: _W
    kv: _W
    o_a: _W
    o_b: _W
    lam_vecs: jax.Array


def _prep_attn_weights(j, w_in_b, w_in, g_cq, w_q_up, g_ckv, w_kv_up, w_o, lam_vecs):
    d = w_in.shape[0]
    a_q = A_HEADS * A_HEAD_DIM
    a_v = A_HEADS * A_V_DIM
    b_out = B_HEADS * B_V_DIM
    assert a_v == b_out
    q_lora = g_cq.shape[0]
    kv_lora = g_ckv.shape[0]
    o0, o1, o2 = a_q, 2 * a_q, 2 * a_q + a_v
    lat = jnp.pad(w_in[:, o2:], ((0, 0), (0, MLA_KR_PAD - ROPE_DIM))).astype(BF16)
    q_up = w_q_up.reshape(q_lora, B_HEADS, NOPE_DIM + ROPE_DIM)
    q_up = jnp.pad(q_up, ((0, 0), (0, 0), (0, MLA_Q_PAD - NOPE_DIM - ROPE_DIM)))
    kv = w_kv_up.reshape(kv_lora, B_HEADS, NOPE_DIM + B_V_DIM)
    kv = jnp.concatenate([kv[:, :, :NOPE_DIM].reshape(kv_lora, B_HEADS * NOPE_DIM),
                          kv[:, :, NOPE_DIM:].reshape(kv_lora, B_HEADS * B_V_DIM)], axis=1)
    c = lambda w: _whole(w.astype(BF16))
    win = lambda col0, n: _W(w_in_b, j, d, 0, col0, n)
    return _AttnWeights(
        qa=win(0, a_q), ka=win(o0, a_q), va=win(o1, a_v), lat=lat,
        g_cq=g_cq.reshape(1, q_lora), g_ckv=g_ckv.reshape(1, kv_lora),
        q_up=c(q_up.reshape(q_lora, B_HEADS * MLA_Q_PAD)), kv=c(kv),
        o_a=_W(w_o, j, a_v, 0, 0, w_o.shape[-1]), o_b=_W(w_o, j, b_out, 1, 0, w_o.shape[-1]),
        lam_vecs=lam_vecs)


def _latent_proj(h, w_lat, g_cq, g_ckv, *, tm, rope=None):
    m, d = h.shape
    q_lora, kv_lora = g_cq.shape[1], g_ckv.shape[1]
    n = w_lat.shape[1]
    assert n == q_lora + kv_lora + MLA_KR_PAD and q_lora % V7X_LANES == 0 and kv_lora % V7X_LANES == 0

    def rms(a, g):
        return a * lax.rsqrt(jnp.mean(a * a, axis=-1, keepdims=True) + EPS) * g

    def body(*refs):
        h_ref, w_ref, gq_ref, gkv_ref = refs[:4]
        cq_ref, ckv_ref, kr_ref = refs[-3:]
        acc = _dot(h_ref[...], w_ref[...])
        cq_ref[...] = rms(acc[:, :q_lora], gq_ref[...]).astype(cq_ref.dtype)
        ckv_ref[...] = rms(acc[:, q_lora:q_lora + kv_lora], gkv_ref[...])
        kr = acc[:, q_lora + kv_lora:]
        if rope is not None:
            kr = _rotate(kr, refs[4][...], refs[5][...])
        kr_ref[...] = kr.astype(kr_ref.dtype)

    row = lambda width: pl.BlockSpec((tm, width), lambda i: (i, 0))
    in_specs = [row(d), pl.BlockSpec((d, n), lambda i: (0, 0), pipeline_mode=pl.Buffered(1)),
                pl.BlockSpec((1, q_lora), lambda i: (0, 0)), pl.BlockSpec((1, kv_lora), lambda i: (0, 0))]
    args = [h, w_lat, g_cq, g_ckv]
    if rope is not None:
        cos, sin, tiles_per_seq = rope
        in_specs += [pl.BlockSpec((tm, MLA_KR_PAD), lambda i: (i % tiles_per_seq, 0))] * 2
        args += [cos, sin]
    return _call(body, name="latent_proj", grid=(m // tm,), in_specs=in_specs,
                 out_specs=[row(q_lora), row(kv_lora), row(MLA_KR_PAD)],
                 out_shape=[jax.ShapeDtypeStruct((m, q_lora), BF16), jax.ShapeDtypeStruct((m, kv_lora), F32),
                            jax.ShapeDtypeStruct((m, MLA_KR_PAD), F32 if rope is None else BF16)],
                 args=args, temp_bytes=4 * _nbytes((tm, n), F32), single_buffered=(1,))


def _attn_mixer(h, x, mods, w, grp, lam_init, cache):
    m, d = h.shape
    hps = A_HEADS if grp.seq <= 256 else 4
    qk_scale = A_QK_HALF ** -0.5 * LOG2_E
    mla_scale = (NOPE_DIM + ROPE_DIM) ** -0.5 * LOG2_E
    rows_unit = grp.seq if grp.rope else m

    def tiles(lhs, rhs, heavy_epilogue):
        k, n = rhs.k, rhs.n
        pref = 1024 if k < 1024 else (256 if heavy_epilogue else 512)
        return _tile(min(rows_unit, lhs.shape[0]), pref, V7X_SUBLANES_BF16), min(n, 2048)

    def plain(name, lhs, rhs, dtype, scale=1.0):
        tm_, tn_ = tiles(lhs, rhs, False)
        return _proj(name, [lhs], [rhs], _ep_scale(scale), (dtype,), tm=tm_, tn=tn_)[0]

    def roped(name, lhs, rhs, tables, dtype, scale, with_plain):
        tm_, tn_ = tiles(lhs, rhs, True)
        tiles_per_seq = grp.seq // tm_
        extras = [(t, (tm_, t.shape[1]), lambda i, j: (i % tiles_per_seq, 0)) for t in tables]
        outs = (dtype, dtype) if with_plain else (dtype,)
        return _proj(name, [lhs], [rhs], _ep_rope(scale, with_plain), outs, tm=tm_, tn=tn_, extras=extras)

    lat_tm = _tile(rows_unit, 256 if grp.rope else 512, V7X_SUBLANES_BF16)
    if not grp.rope:
        cqn, ckvn, kr = _latent_proj(h, w.lat, w.g_cq, w.g_ckv, tm=lat_tm)
        kv = plain("kv_up", ckvn, w.kv, BF16)
        qa = plain("q_diff", h, w.qa, BF16, qk_scale)
        ka = plain("k_diff", h, w.ka, F32)
        va = plain("v_diff", h, w.va, F32)
        q = plain("q_mla", cqn, w.q_up, BF16, mla_scale)
        oa = _diff_attention(w.lam_vecs, lam_init, [qa], [ka], [va], grp, hps)
        ob = _mla_attention([q], [kv], [kr], grp, hps)
        state = (ka, va, ckvn, kr[:, :ROPE_DIM])
    else:
        cos, sin = _rope_tables(grp.seq)
        t_diff = _widen_tables(cos, sin, 0, 0, A_HEAD_DIM)
        t_mla = _widen_tables(cos, sin, NOPE_DIM, MLA_Q_PAD - NOPE_DIM - ROPE_DIM, MLA_Q_PAD)
        t_kr = _widen_tables(cos, sin, 0, MLA_KR_PAD - ROPE_DIM, MLA_KR_PAD)
        ka_c, va_c, ckv_c, kr_c = cache
        cqn, ckvn, kr_r = _latent_proj(h, w.lat, w.g_cq, w.g_ckv, tm=lat_tm,
                                       rope=(t_kr[0], t_kr[1], grp.seq // lat_tm))
        kv = plain("kv_up", ckvn, w.kv, BF16)
        qa_u, qa_r = roped("q_diff", h, w.qa, t_diff, BF16, qk_scale, True)
        ka_r, = roped("k_diff", h, w.ka, t_diff, BF16, 1.0, False)
        va = plain("v_diff", h, w.va, BF16)
        q_u, q_r = roped("q_mla", cqn, w.q_up, t_mla, BF16, mla_scale, True)
        kv_c = plain("kv_up_ctx", ckv_c, w.kv, BF16)
        oa = _diff_attention(w.lam_vecs, lam_init, [qa_u, qa_r], [ka_c, ka_r], [va_c, va], grp, hps)
        ob = _mla_attention([q_u, q_r], [kv_c, kv], [kr_c, kr_r], grp, hps)
        state = None
    x1 = _residual_proj("o_proj", [oa, ob], [w.o_a, w.o_b], x, mods, 2,
                        tm=_tile(rows_unit, 1024, V7X_SUBLANES_BF16), tn=512)
    return x1, state


def _conv_ffn(x, mods, g2, w_up, conv_w, conv_b, w_down, w_down_b, layer, grp):
    h = _norm(x, g2, mods, 3, 4, BF16)
    act, made = _ffn_up(h, w_up, conv_w, conv_b, layer, grp, None if w_down_b is not None else w_down)
    if w_down_b is None:
        w_down_b = made if made is not None else w_down[layer].astype(BF16)
    out = _residual_proj("ffn_down", [act], [_whole(w_down_b)], x, mods, 5,
                         tm=_tile(x.shape[0] // mods.shape[0], 512, V7X_SUBLANES_BF16), tn=512)
    return out, w_down_b


def kernel(x_prompt, x_sample, cache_diff_k, cache_diff_v, cache_mla_ckv, cache_mla_krope, c, c_ctx,
           norm1_g, norm2_g, w_mod, b_mod, w_in, g_cq, w_q_up, g_ckv, w_kv_up,
           lambda_q1, lambda_k1, lambda_q2, lambda_k2, w_o, w_pool, pool_scale,
           w_up, conv_w, conv_b, w_down, g_final):
    batch, seq, d = x_prompt.shape
    dec_batch, dec_seq, _ = x_sample.shape
    depth = w_mod.shape[0]
    past = cache_diff_k.shape[2]
    groups = (_Group(batch, seq, False), _Group(dec_batch, dec_seq, True))

    n_cond = 1 + dec_batch
    cond8 = jnp.zeros((8, d), F32).at[0].set(c_ctx).at[1:n_cond].set(c)
    mods_all = _adaln(cond8, w_mod, b_mod).reshape(depth, 8, N_MOD, d)
    mods_g = (mods_all[:, 0:1], mods_all[:, 1:n_cond])

    w_in_b = w_in.astype(BF16)
    attn_w = {}
    for i in range(0, depth, 2):
        j = i // 2
        lam_vecs = jnp.stack([lambda_q1[j], lambda_k1[j], lambda_q2[j], lambda_k2[j]])
        attn_w[i] = _prep_attn_weights(j, w_in_b, w_in[j], g_cq[j], w_q_up[j], g_ckv[j], w_kv_up[j],
                                       w_o, lam_vecs)
    w_down_b = [None] * depth

    xs = [x_prompt.reshape(batch * seq, d), x_sample.reshape(dec_batch * dec_seq, d)]
    states = []
    for gi, grp in enumerate(groups):
        x = xs[gi]
        for i in range(depth):
            mods = mods_g[gi][i]
            h = _norm(x, norm1_g[i], mods, 0, 1, BF16)
            if i % 2 == 0:
                j = i // 2
                lam_init = 0.8 - 0.6 * math.exp(-0.3 * i)
                cache = None
                if grp.rope:
                    kr_c = jnp.pad(cache_mla_krope[:, j].reshape(dec_batch * past, ROPE_DIM),
                                   ((0, 0), (0, MLA_KR_PAD - ROPE_DIM)))
                    cache = (cache_diff_k[:, j].reshape(dec_batch * past, A_HEADS * A_HEAD_DIM),
                             cache_diff_v[:, j].reshape(dec_batch * past, A_HEADS * A_V_DIM),
                             cache_mla_ckv[:, j].reshape(dec_batch * past, -1), kr_c)
                x, st = _attn_mixer(h, x, mods, attn_w[i], grp, lam_init, cache)
                if st is not None:
                    states.append(st)
            else:
                x = _pool_mixer(h, x, mods, w_pool, i // 2, pool_scale[i // 2], grp, 2)
            x, w_down_b[i] = _conv_ffn(x, mods, norm2_g[i], w_up, conv_w, conv_b, w_down, w_down_b[i], i, grp)
        xs[gi] = _norm(x, g_final, None, 0, 0, F32)

    y_prompt = xs[0].reshape(batch, seq, d)
    y_sample = xs[1].reshape(dec_batch, dec_seq, d)
    stack = lambda k, tail: jnp.stack([s[k].reshape((batch, seq) + tail) for s in states], axis=1)
    new_diff_k = stack(0, (A_HEADS, A_HEAD_DIM))
    new_diff_v = stack(1, (A_HEADS, A_V_DIM))
    new_mla_ckv = stack(2, (g_ckv.shape[-1],))
    new_mla_krope = stack(3, (ROPE_DIM,))
    return (y_prompt, y_sample, new_diff_k, new_diff_v, new_mla_ckv, new_mla_krope)
```

```python
import functools
import math
from typing import NamedTuple

import jax
import jax.numpy as jnp
from jax import lax
from jax.experimental import pallas as pl
from jax.experimental.pallas import tpu as pltpu

F32 = jnp.float32
BF16 = jnp.bfloat16

GRID_W = 64
ROPE_BASE = 10000.0
EPS = 1e-6
A_HEADS = 16
A_QK_HALF = 64
A_HEAD_DIM = 2 * A_QK_HALF
A_V_DIM = 128
B_HEADS = 16
NOPE_DIM = 128
ROPE_DIM = 64
B_V_DIM = 128
POOL_WINDOWS = (2, 4, 8, 16)
N_MOD = 6
CONV_W = 3
LOG2_E = math.log2(math.e)

V7X_VMEM_BYTES = 64 * 1024 * 1024
V7X_LANES = 128
V7X_SUBLANES_BF16 = 16
COMPILER_SCRATCH_BYTES = 2 * 1024 * 1024

MLA_Q_PAD = 256
MLA_KR_PAD = MLA_Q_PAD - NOPE_DIM
ROPE_QUARTER = ROPE_DIM // 4
POOL_HALO = V7X_SUBLANES_BF16


class _Group(NamedTuple):
    n_seq: int
    seq: int
    rope: bool


def _tile(n, pref, mult=8):
    if n <= pref:
        return n
    t = (pref // mult) * mult
    while t >= mult:
        if n % t == 0:
            return t
        t -= mult
    return n


def _nbytes(shape, dtype):
    return math.prod(shape) * jnp.dtype(dtype).itemsize


def _call(body, *, name, grid, in_specs, out_specs, out_shape, args, temp_bytes=0, single_buffered=(),
          scratch=(), semantics=None, flags=None):
    multi = isinstance(out_shape, (list, tuple))
    outs = list(out_shape) if multi else [out_shape]
    ospecs = list(out_specs) if multi else [out_specs]
    total = temp_bytes + COMPILER_SCRATCH_BYTES + sum(_nbytes(s, dt) for s, dt in scratch)
    for k, (a, s) in enumerate(zip(args, in_specs)):
        blk = [1 if b is None else b for b in s.block_shape]
        total += _nbytes(blk, a.dtype) * (1 if k in single_buffered else 2)
    for o, s in zip(outs, ospecs):
        blk = [1 if b is None else b for b in s.block_shape]
        total += 2 * _nbytes(blk, o.dtype)
    limit = min(V7X_VMEM_BYTES - 4 * 1024 * 1024, max(total, 16 * 1024 * 1024))
    return pl.pallas_call(
        body, name=name, grid=grid, in_specs=in_specs, out_specs=out_specs, out_shape=out_shape,
        scratch_shapes=[pltpu.VMEM(s, dt) for s, dt in scratch],
        compiler_params=pltpu.CompilerParams(
            dimension_semantics=semantics or ("parallel",) * len(grid), vmem_limit_bytes=int(limit),
            flags=flags),
    )(*args)


def _dot(a, b):
    return jnp.dot(a, b, preferred_element_type=F32)


def _dot_nt(a, b):
    return lax.dot_general(a, b, (((1,), (1,)), ((), ())), preferred_element_type=F32)


def _silu(x):
    return x / (1.0 + jnp.exp(-x))


def _adaln(cond8, w_mod, b_mod):
    depth, d, n = w_mod.shape
    tn = next(t for t in (512, 256, 128) if n % (2 * t) == 0)

    def body(c_ref, wa_ref, wb_ref, b_ref, o_ref):
        s = _silu(c_ref[...]).astype(BF16)
        ya = _dot(s, wa_ref[...].astype(BF16))
        yb = _dot(s, wb_ref[...].astype(BF16))
        o_ref[...] = jnp.concatenate([ya, yb], axis=1) + b_ref[...]

    return _call(
        body, name="adaln", grid=(depth, n // (2 * tn)),
        in_specs=[pl.BlockSpec((8, d), lambda l, j: (0, 0)),
                  pl.BlockSpec((None, d, tn), lambda l, j: (l, 0, 2 * j)),
                  pl.BlockSpec((None, d, tn), lambda l, j: (l, 0, 2 * j + 1)),
                  pl.BlockSpec((None, 1, 2 * tn), lambda l, j: (l, 0, j))],
        out_specs=pl.BlockSpec((None, 8, 2 * tn), lambda l, j: (l, 0, j)),
        out_shape=jax.ShapeDtypeStruct((depth, 8, n), F32),
        args=(cond8, w_mod, w_mod, b_mod.reshape(depth, 1, n)),
        temp_bytes=2 * _nbytes((d, tn), BF16) + 2 * _nbytes((d, tn), F32))


def _norm(x, g, mods, k_shift, k_scale, out_dtype):
    m, d = x.shape
    modulated = mods is not None
    tm = _tile(m // mods.shape[0] if modulated else m, 512)

    halves = 2 if d % (2 * V7X_LANES) == 0 else 1

    def body(*refs):
        g_ref = refs[halves]
        o_ref = refs[-1]
        xf = jnp.concatenate([r[...] for r in refs[:halves]], axis=1)
        y = xf * lax.rsqrt(jnp.mean(xf * xf, axis=-1, keepdims=True) + EPS) * g_ref[...]
        if modulated:
            mm = refs[halves + 1][...]
            y = y * (1.0 + mm[k_scale:k_scale + 1]) + mm[k_shift:k_shift + 1]
        o_ref[...] = y.astype(o_ref.dtype)

    in_specs = ([pl.BlockSpec((tm, d // halves), lambda i, c=c: (i, c)) for c in range(halves)]
                + [pl.BlockSpec((1, d), lambda i: (0, 0))])
    args = [x] * halves + [g.reshape(1, d)]
    if modulated:
        rows_per_cond = m // mods.shape[0]
        in_specs.append(pl.BlockSpec((None, N_MOD, d), lambda i: ((i * tm) // rows_per_cond, 0, 0)))
        args.append(mods)
    return _call(body, name="norm", grid=(m // tm,), in_specs=in_specs,
                 out_specs=pl.BlockSpec((tm, d), lambda i: (i, 0)),
                 out_shape=jax.ShapeDtypeStruct((m, d), out_dtype), args=args,
                 temp_bytes=3 * _nbytes((tm, d), F32))


class _W(NamedTuple):
    arr: jax.Array
    layer: object
    k: int
    row_blk: int
    col0: int
    n: int

    def spec(self, tn, col=lambda i, j: j, **mode):
        assert self.col0 % tn == 0 and self.n % tn == 0
        c0 = self.col0 // tn
        if self.layer is None:
            return pl.BlockSpec((self.k, tn), lambda i, j: (self.row_blk, c0 + col(i, j)), **mode)
        return pl.BlockSpec((None, self.k, tn),
                            lambda i, j: (self.layer, self.row_blk, c0 + col(i, j)), **mode)


def _whole(arr):
    return _W(arr, None, arr.shape[0], 0, 0, arr.shape[1])


def _proj(name, lhs, rhs, epilogue, outs, *, tm, tn, extras=()):
    m = lhs[0].shape[0]
    n = rhs[0].n
    np_ = len(lhs)
    ne = len(extras)

    def body(*refs):
        acc = None
        for p in range(np_):
            part = _dot(refs[p][...].astype(BF16), refs[np_ + p][...].astype(BF16))
            acc = part if acc is None else acc + part
        epilogue(acc, refs[2 * np_:2 * np_ + ne], refs[2 * np_ + ne:])

    resident = n == tn
    rhs_mode = dict(pipeline_mode=pl.Buffered(1)) if resident else {}
    nj = n // tn
    col = (lambda i, j: j) if nj == 1 else (lambda i, j: jnp.where(i % 2 == 0, j, nj - 1 - j))
    in_specs = ([pl.BlockSpec((tm, a.shape[1]), lambda i, j: (i, 0)) for a in lhs]
                + [w.spec(tn, col, **rhs_mode) for w in rhs]
                + [pl.BlockSpec(bs, lambda i, j, im=im: im(i, col(i, j))) for (_, bs, im) in extras])
    args = list(lhs) + [w.arr for w in rhs] + [e[0] for e in extras]
    out_shape = [jax.ShapeDtypeStruct((m, n), dt) for dt in outs]
    out_specs = [pl.BlockSpec((tm, tn), lambda i, j: (i, col(i, j))) for _ in outs]
    return _call(body, name=name, grid=(m // tm, nj), in_specs=in_specs, out_specs=out_specs,
                 out_shape=out_shape, args=args, temp_bytes=6 * _nbytes((tm, tn), F32),
                 single_buffered=tuple(range(np_, 2 * np_)) if resident else ())


def _ep_scale(scale):
    def ep(acc, ex, outs):
        outs[0][...] = (acc * scale).astype(outs[0].dtype)
    return ep


def _rotate(a, cos, sin):
    width = a.shape[1]
    lane = lax.broadcasted_iota(jnp.int32, a.shape, 1)
    low = (lane & (2 * ROPE_QUARTER - 1)) < ROPE_QUARTER
    partner = jnp.where(low, pltpu.roll(a, width - ROPE_QUARTER, 1), pltpu.roll(a, ROPE_QUARTER, 1))
    return a * cos + partner * sin


def _ep_rope(scale, with_plain):
    def ep(acc, ex, outs):
        a = acc * scale
        reps = a.shape[1] // ex[0].shape[1]
        cos = jnp.concatenate([ex[0][...]] * reps, axis=1)
        sin = jnp.concatenate([ex[1][...]] * reps, axis=1)
        r = _rotate(a, cos, sin)
        if with_plain:
            outs[0][...] = a.astype(outs[0].dtype)
        outs[-1][...] = r.astype(outs[-1].dtype)
    return ep


def _ep_residual(k_gate):
    def ep(acc, ex, outs):
        gate = ex[1][...][k_gate:k_gate + 1]
        outs[0][...] = ex[0][...] + gate * acc
    return ep


def _rope_tables(seq):
    half = ROPE_DIM // 2
    inv = ROPE_BASE ** (-jnp.arange(0, half, 2, dtype=F32) / half)
    pos = jnp.arange(seq)
    ar = (pos // GRID_W).astype(F32)[:, None] * inv[None, :]
    ac = (pos % GRID_W).astype(F32)[:, None] * inv[None, :]
    cos = jnp.concatenate([jnp.cos(ar), jnp.cos(ar), jnp.cos(ac), jnp.cos(ac)], axis=1)
    sin = jnp.concatenate([-jnp.sin(ar), jnp.sin(ar), -jnp.sin(ac), jnp.sin(ac)], axis=1)
    return cos, sin


def _widen_tables(cos, sin, lead, trail, width):
    seq = cos.shape[0]
    c = jnp.concatenate([jnp.ones((seq, lead), F32), cos, jnp.ones((seq, trail), F32)], axis=1)
    s = jnp.concatenate([jnp.zeros((seq, lead), F32), sin, jnp.zeros((seq, trail), F32)], axis=1)
    reps = width // c.shape[1]
    assert reps * c.shape[1] == width
    return jnp.tile(c, (1, reps)), jnp.tile(s, (1, reps))


def _diff_attention(lam_vecs, lam_init, qs, ks, vs, grp, hps):
    m = qs[0].shape[0]
    nseg = len(qs)
    tq = _tile(grp.seq, 256)
    qt = grp.seq // tq
    width = hps * A_HEAD_DIM
    n_hg = A_HEADS // hps
    k_lens = [k.shape[0] // grp.n_seq for k in ks]

    def body(*refs):
        lam_ref = refs[0]
        q_refs = refs[1:1 + nseg]
        k_refs = refs[1 + nseg:1 + 2 * nseg]
        v_refs = refs[1 + 2 * nseg:1 + 3 * nseg]
        o_ref = refs[-1]
        lv = lam_ref[...]
        lam = (jnp.exp(jnp.sum(lv[0:1] * lv[1:2], axis=-1, keepdims=True))
               - jnp.exp(jnp.sum(lv[2:3] * lv[3:4], axis=-1, keepdims=True)) + lam_init)
        first = lax.broadcasted_iota(jnp.int32, (tq, A_HEAD_DIM), 1) < A_QK_HALF
        for g in range(hps):
            cols = slice(g * A_HEAD_DIM, (g + 1) * A_HEAD_DIM)
            scores = []
            for q_ref, k_ref in zip(q_refs, k_refs):
                q = q_ref[:, cols]
                zero = jnp.zeros_like(q)
                q2 = jnp.concatenate([jnp.where(first, q, zero), jnp.where(first, zero, q)], axis=0)
                scores.append(_dot_nt(q2, k_ref[:, cols].astype(BF16)))
            mx = functools.reduce(jnp.maximum, [jnp.max(s, axis=-1, keepdims=True) for s in scores])
            ps = [jnp.exp2(s - mx) for s in scores]
            den = functools.reduce(jnp.add, [jnp.sum(p, axis=-1, keepdims=True) for p in ps])
            o2 = functools.reduce(
                jnp.add, [_dot(p.astype(BF16), v_ref[:, cols].astype(BF16)) for p, v_ref in zip(ps, v_refs)])
            o2 = o2 / den
            o = o2[:tq] - lam * o2[tq:]
            o = o * lax.rsqrt(jnp.mean(o * o, axis=-1, keepdims=True) + EPS) * (1.0 - lam_init)
            o_ref[:, cols] = o.astype(o_ref.dtype)

    q_spec = pl.BlockSpec((tq, width), lambda b, h, i: (b * qt + i, h))
    in_specs = ([pl.BlockSpec(lam_vecs.shape, lambda b, h, i: (0, 0))]
                + [q_spec] * nseg
                + [pl.BlockSpec((kl, width), lambda b, h, i: (b, h)) for kl in k_lens] * 2)
    nk = sum(k_lens)
    return _call(body, name="diff_attention", grid=(grp.n_seq, n_hg, qt), in_specs=in_specs,
                 out_specs=q_spec, out_shape=jax.ShapeDtypeStruct((m, A_HEADS * A_V_DIM), BF16),
                 args=[lam_vecs] + list(qs) + list(ks) + list(vs),
                 temp_bytes=4 * _nbytes((2 * tq, nk), F32))


def _mla_attention(qs, kvs, krs, grp, hps):
    m = qs[0].shape[0]
    nseg = len(qs)
    tq = _tile(grp.seq, 256)
    qt = grp.seq // tq
    n_hg = B_HEADS // hps
    k_lens = [kv.shape[0] // grp.n_seq for kv in kvs]

    def body(*refs):
        q_refs = refs[:nseg]
        kn_refs = refs[nseg:2 * nseg]
        kr_refs = refs[2 * nseg:3 * nseg]
        v_refs = refs[3 * nseg:4 * nseg]
        o_ref = refs[-1]
        krs_v = [r[...].astype(BF16) for r in kr_refs]
        for g in range(hps):
            kcols = slice(g * NOPE_DIM, (g + 1) * NOPE_DIM)
            vcols = slice(g * B_V_DIM, (g + 1) * B_V_DIM)
            scores = []
            for q_ref, kn_ref, kr in zip(q_refs, kn_refs, krs_v):
                q = q_ref[:, g * MLA_Q_PAD:(g + 1) * MLA_Q_PAD]
                k = jnp.concatenate([kn_ref[:, kcols], kr], axis=1)
                scores.append(_dot_nt(q, k))
            mx = functools.reduce(jnp.maximum, [jnp.max(s, axis=-1, keepdims=True) for s in scores])
            ps = [jnp.exp2(s - mx) for s in scores]
            den = functools.reduce(jnp.add, [jnp.sum(p, axis=-1, keepdims=True) for p in ps])
            o = functools.reduce(
                jnp.add, [_dot(p.astype(BF16), v_ref[:, vcols]) for p, v_ref in zip(ps, v_refs)])
            o_ref[:, vcols] = (o / den).astype(o_ref.dtype)

    q_spec = pl.BlockSpec((tq, hps * MLA_Q_PAD), lambda b, h, i: (b * qt + i, h))
    assert NOPE_DIM == B_V_DIM
    k_specs = [pl.BlockSpec((kl, hps * NOPE_DIM), lambda b, h, i: (b, h)) for kl in k_lens]
    v_specs = [pl.BlockSpec((kl, hps * B_V_DIM), lambda b, h, i: (b, n_hg + h)) for kl in k_lens]
    kr_specs = [pl.BlockSpec((kl, MLA_KR_PAD), lambda b, h, i: (b, 0)) for kl in k_lens]
    nk = sum(k_lens)
    return _call(body, name="mla_attention", grid=(grp.n_seq, n_hg, qt),
                 in_specs=[q_spec] * nseg + k_specs + kr_specs + v_specs,
                 out_specs=pl.BlockSpec((tq, hps * B_V_DIM), lambda b, h, i: (b * qt + i, h)),
                 out_shape=jax.ShapeDtypeStruct((m, B_HEADS * B_V_DIM), BF16),
                 args=list(qs) + list(kvs) + list(krs) + list(kvs),
                 temp_bytes=4 * _nbytes((tq, nk), F32) + 2 * _nbytes((nk, MLA_Q_PAD), BF16))


def _pool_mixer(h, x, mods, w_pool, layer, pool_scale, grp, k_gate):
    m, d = x.shape
    _, n_groups, c, _ = w_pool.shape
    assert max(POOL_WINDOWS) // 2 <= POOL_HALO and n_groups == len(POOL_WINDOWS)
    rows_per_cond = m // mods.shape[0]
    r = _tile(rows_per_cond, 512, POOL_HALO)
    nt = m // r
    halo_per_tile = r // POOL_HALO
    kc = r + 2 * POOL_HALO
    period = math.lcm(r, grp.seq) // r
    member, inv_len = _pool_tables(grp.seq, r, period, kc)

    def body(hp_ref, hc_ref, hn_ref, mem_ref, inv_ref, w_ref, ps_ref, x_ref, m_ref, o_ref):
        hc = hc_ref[...]
        hcat = jnp.concatenate([hp_ref[...], hc, hn_ref[...]], axis=0)
        win = _dot(mem_ref[...], hcat)
        inv = jnp.concatenate([inv_ref[...]] * (c // V7X_LANES), axis=1)
        pooled = win * inv - hc.astype(F32)
        y = _dot(pooled.astype(BF16), w_ref[...].astype(BF16)) * ps_ref[...]
        gate = m_ref[...][k_gate:k_gate + 1]
        o_ref[...] = x_ref[...] + gate * y

    last_halo = m // POOL_HALO - 1
    in_specs = [
        pl.BlockSpec((POOL_HALO, c), lambda g, i: (jnp.maximum(i * halo_per_tile - 1, 0), g)),
        pl.BlockSpec((r, c), lambda g, i: (i, g)),
        pl.BlockSpec((POOL_HALO, c), lambda g, i: (jnp.minimum((i + 1) * halo_per_tile, last_halo), g)),
        pl.BlockSpec((None, None, r, kc), lambda g, i: (g, i % period, 0, 0)),
        pl.BlockSpec((None, r, V7X_LANES), lambda g, i: (g, i % period, 0)),
        pl.BlockSpec((None, None, c, c), lambda g, i: (layer, g, 0, 0)),
        pl.BlockSpec((1, c), lambda g, i: (0, g)),
        pl.BlockSpec((r, c), lambda g, i: (i, g)),
        pl.BlockSpec((None, N_MOD, c), lambda g, i: ((i * r) // rows_per_cond, 0, g)),
    ]
    return _call(body, name="pool_mixer", grid=(n_groups, nt), in_specs=in_specs,
                 out_specs=pl.BlockSpec((r, c), lambda g, i: (i, g)),
                 out_shape=jax.ShapeDtypeStruct((m, d), F32),
                 args=[h, h, h, member, inv_len, w_pool, pool_scale.reshape(1, d), x, mods],
                 temp_bytes=4 * _nbytes((r, c), F32))


def _pool_tables(seq, r, period, kc):
    p = jnp.arange(period)[:, None, None]
    row = p * r + jnp.arange(r)[None, :, None]
    col = p * r - POOL_HALO + jnp.arange(kc)[None, None, :]
    seq_start = (row // seq) * seq
    members, invs = [], []
    for w in POOL_WINDOWS:
        lo = jnp.maximum(row - w // 2, seq_start)
        hi = jnp.minimum(row + w - w // 2, seq_start + seq)
        members.append(((col >= lo) & (col < hi)).astype(BF16))
        invs.append(jnp.broadcast_to(1.0 / (hi - lo).astype(F32), (period, r, V7X_LANES)))
    inv = jnp.stack(invs).reshape(len(POOL_WINDOWS), period * r, V7X_LANES)
    return jnp.stack(members), inv


def _ffn_up(h, w_up, conv_w, conv_b, layer, grp, w_down=None):
    m, d = h.shape
    f = w_up.shape[2] // 2
    tm = _tile(m, max(grp.seq, 2048), grp.seq)
    tn = _tile(f, 256, V7X_LANES)
    nj = f // tn
    seq = grp.seq
    steps = (m // tm) * nj
    fuse_cast = w_down is not None and w_down.shape[1] % (steps * V7X_SUBLANES_BF16) == 0
    slab_rows = w_down.shape[1] // steps if fuse_cast else 0

    rc = _tile(seq, 512)
    pad = 8
    rs = _tile(tm, 512, V7X_SUBLANES_BF16)

    def body(*refs):
        h_ref, wa_ref, wb_ref, cwa_ref, cwb_ref, cba_ref, cbb_ref = refs[:7]
        if fuse_cast:
            wd_ref, o_ref, wdo_ref, acc_ref = refs[7:]
            wdo_ref[...] = wd_ref[...].astype(BF16)
        else:
            o_ref, acc_ref = refs[7:]
        row8 = lax.broadcasted_iota(jnp.int32, (8, tn), 0)

        def matmul(s):
            xs = h_ref[s * rs:(s + 1) * rs, :]
            for k, w_ref in enumerate((wa_ref, wb_ref)):
                acc_ref[k, pad + s * rs:pad + (s + 1) * rs] = _dot(xs, w_ref[...].astype(BF16))

        def piece(p0, p1):
            n = p1 - p0

            def conv(k, cw_ref, cb_ref):
                cw = cw_ref[...]
                u = acc_ref[k, pad + p0:pad + p1]
                prev = acc_ref[k, pad + p0 - 1:pad + p1 - 1]
                nxt = acc_ref[k, pad + p0 + 1:pad + p1 + 1]
                top = jnp.where(row8 == 0, 0.0, prev[:8]) if p0 % seq == 0 else prev[:8]
                prev = top if n == 8 else jnp.concatenate([top, prev[8:]], axis=0)
                bot = jnp.where(row8 == 7, 0.0, nxt[n - 8:]) if p1 % seq == 0 else nxt[n - 8:]
                nxt = bot if n == 8 else jnp.concatenate([nxt[:n - 8], bot], axis=0)
                return prev * cw[0:1] + u * cw[1:2] + nxt * cw[2:3] + cb_ref[...]

            a = conv(0, cwa_ref, cba_ref)
            b = conv(1, cwb_ref, cbb_ref)
            o_ref[p0:p1] = (_silu(a) * b).astype(o_ref.dtype)

        def epilogue(r0, r1):
            cuts = sorted({r0, r1} | {c for c in range(0, tm + 1, rc) if r0 < c < r1})
            for p0, p1 in zip(cuts[:-1], cuts[1:]):
                piece(p0, p1)

        n_stage = tm // rs
        for k in range(2):
            acc_ref[k, 0:pad] = jnp.zeros((pad, tn), F32)
            acc_ref[k, pad + tm:2 * pad + tm] = jnp.zeros((pad, tn), F32)
        for s in range(n_stage + 1):
            if s > 0:
                epilogue(max((s - 1) * rs - pad, 0), tm if s == n_stage else s * rs - pad)
            if s < n_stage:
                matmul(s)

    col = lambda i, j: jnp.where(i % 2 == 0, j, nj - 1 - j)
    in_specs = [
        pl.BlockSpec((tm, d), lambda i, j: (i, 0)),
        pl.BlockSpec((None, d, tn), lambda i, j: (layer, 0, col(i, j))),
        pl.BlockSpec((None, d, tn), lambda i, j: (layer, 0, nj + col(i, j))),
        pl.BlockSpec((None, CONV_W, tn), lambda i, j: (layer, 0, col(i, j))),
        pl.BlockSpec((None, CONV_W, tn), lambda i, j: (layer, 0, nj + col(i, j))),
        pl.BlockSpec((None, 1, tn), lambda i, j: (layer, 0, col(i, j))),
        pl.BlockSpec((None, 1, tn), lambda i, j: (layer, 0, nj + col(i, j))),
    ]
    cb = conv_b.reshape(conv_b.shape[0], 1, 2 * f)
    args = [h, w_up, w_up, conv_w, conv_w, cb, cb]
    out_specs = [pl.BlockSpec((tm, tn), lambda i, j: (i, col(i, j)))]
    out_shape = [jax.ShapeDtypeStruct((m, f), BF16)]
    if fuse_cast:
        slab = (slab_rows, w_down.shape[2])
        in_specs.append(pl.BlockSpec((None,) + slab, lambda i, j: (layer, i * nj + j, 0)))
        args.append(w_down)
        out_specs.append(pl.BlockSpec(slab, lambda i, j: (i * nj + j, 0)))
        out_shape.append(jax.ShapeDtypeStruct(w_down.shape[1:], BF16))
    res = _call(body, name="ffn_up", grid=(m // tm, nj), in_specs=in_specs, out_specs=out_specs,
                out_shape=out_shape, args=args,
                temp_bytes=10 * _nbytes((rc, tn), F32) + 4 * _nbytes((rs, tn), F32) + 2 * _nbytes((d, tn), BF16),
                scratch=[((2, tm + 2 * pad, tn), F32)])
    return res[0], (res[1] if fuse_cast else None)


def _residual_proj(name, lhs, rhs, x, mods, k_gate, *, tm, tn):
    m = x.shape[0]
    rows_per_cond = m // mods.shape[0]
    assert rows_per_cond % tm == 0
    extras = [(x, (tm, tn), lambda i, j: (i, j)),
              (mods, (None, N_MOD, tn), lambda i, j: ((i * tm) // rows_per_cond, 0, j))]
    return _proj(name, lhs, rhs, _ep_residual(k_gate), (F32,), tm=tm, tn=tn, extras=extras)[0]


class _AttnWeights(NamedTuple):
    w_in: jax.Array
    layer: int
    qa: tuple
    ka: tuple
    va: tuple
    lat: jax.Array
    g_cq: jax.Array
    g_ckv: jax.Array
    q_up: _W
    kv: _W
    o_a: _W
    o_b: _W
    lam_vecs: jax.Array


def _prep_attn_weights(j, w_in_all, g_cq, w_q_up, g_ckv, w_kv_up, w_o, lam_vecs):
    w_in = w_in_all[j]
    d = w_in.shape[0]
    a_q = A_HEADS * A_HEAD_DIM
    a_v = A_HEADS * A_V_DIM
    b_out = B_HEADS * B_V_DIM
    assert a_v == b_out
    q_lora = g_cq.shape[0]
    kv_lora = g_ckv.shape[0]
    o0, o1, o2 = a_q, 2 * a_q, 2 * a_q + a_v
    lat = jnp.pad(w_in[:, o2:], ((0, 0), (0, MLA_KR_PAD - ROPE_DIM))).astype(BF16)
    q_up = w_q_up.reshape(q_lora, B_HEADS, NOPE_DIM + ROPE_DIM)
    q_up = jnp.pad(q_up, ((0, 0), (0, 0), (0, MLA_Q_PAD - NOPE_DIM - ROPE_DIM)))
    kv = w_kv_up.reshape(kv_lora, B_HEADS, NOPE_DIM + B_V_DIM)
    kv = jnp.concatenate([kv[:, :, :NOPE_DIM].reshape(kv_lora, B_HEADS * NOPE_DIM),
                          kv[:, :, NOPE_DIM:].reshape(kv_lora, B_HEADS * B_V_DIM)], axis=1)
    c = lambda w: _whole(w.astype(BF16))
    return _AttnWeights(
        w_in=w_in_all, layer=j, qa=(0, a_q), ka=(o0, a_q), va=(o1, a_v), lat=lat,
        g_cq=g_cq.reshape(1, q_lora), g_ckv=g_ckv.reshape(1, kv_lora),
        q_up=c(q_up.reshape(q_lora, B_HEADS * MLA_Q_PAD)), kv=c(kv),
        o_a=_W(w_o, j, a_v, 0, 0, w_o.shape[-1]), o_b=_W(w_o, j, b_out, 1, 0, w_o.shape[-1]),
        lam_vecs=lam_vecs)


def _latent_proj(h, w_lat, g_cq, g_ckv, *, tm, rope=None, cast=None):
    m, d = h.shape
    q_lora, kv_lora = g_cq.shape[1], g_ckv.shape[1]
    n = w_lat.shape[1]
    assert n == q_lora + kv_lora + MLA_KR_PAD and q_lora % V7X_LANES == 0 and kv_lora % V7X_LANES == 0
    n_out = 3 if cast is None else 4

    def rms(a, g):
        return a * lax.rsqrt(jnp.mean(a * a, axis=-1, keepdims=True) + EPS) * g

    def body(*refs):
        h_ref, w_ref, gq_ref, gkv_ref = refs[:4]
        cq_ref, ckv_ref, kr_ref = refs[-n_out:][:3]
        if cast is not None:
            refs[-1][...] = refs[-n_out - 1][...].astype(BF16)
        acc = _dot(h_ref[...], w_ref[...])
        cq_ref[...] = rms(acc[:, :q_lora], gq_ref[...]).astype(cq_ref.dtype)
        ckv_ref[...] = rms(acc[:, q_lora:q_lora + kv_lora], gkv_ref[...])
        kr = acc[:, q_lora + kv_lora:]
        if rope is not None:
            kr = _rotate(kr, refs[4][...], refs[5][...])
        kr_ref[...] = kr.astype(kr_ref.dtype)

    row = lambda width: pl.BlockSpec((tm, width), lambda i: (i, 0))
    in_specs = [row(d), pl.BlockSpec((d, n), lambda i: (0, 0), pipeline_mode=pl.Buffered(1)),
                pl.BlockSpec((1, q_lora), lambda i: (0, 0)), pl.BlockSpec((1, kv_lora), lambda i: (0, 0))]
    args = [h, w_lat, g_cq, g_ckv]
    if rope is not None:
        cos, sin, tiles_per_seq = rope
        in_specs += [pl.BlockSpec((tm, MLA_KR_PAD), lambda i: (i % tiles_per_seq, 0))] * 2
        args += [cos, sin]
    out_specs = [row(q_lora), row(kv_lora), row(MLA_KR_PAD)]
    out_shape = [jax.ShapeDtypeStruct((m, q_lora), BF16), jax.ShapeDtypeStruct((m, kv_lora), F32),
                 jax.ShapeDtypeStruct((m, MLA_KR_PAD), F32 if rope is None else BF16)]
    if cast is not None:
        w_src, layer, n_cols = cast
        slab = n_cols // (m // tm)
        assert slab * (m // tm) == n_cols and slab % V7X_LANES == 0
        in_specs.append(pl.BlockSpec((None, d, slab), lambda i: (layer, 0, i)))
        args.append(w_src)
        out_specs.append(pl.BlockSpec((d, slab), lambda i: (0, i)))
        out_shape.append(jax.ShapeDtypeStruct((d, n_cols), BF16))
    return _call(body, name="latent_proj", grid=(m // tm,), in_specs=in_specs, out_specs=out_specs,
                 out_shape=out_shape, args=args, temp_bytes=4 * _nbytes((tm, n), F32), single_buffered=(1,))


def _attn_mixer(h, x, mods, w, grp, lam_init, cache, wide_b):
    m, d = h.shape
    hps = A_HEADS if grp.seq <= 256 else 4
    qk_scale = A_QK_HALF ** -0.5 * LOG2_E
    mla_scale = (NOPE_DIM + ROPE_DIM) ** -0.5 * LOG2_E
    rows_unit = grp.seq if grp.rope else m

    def tiles(lhs, rhs, heavy_epilogue):
        k, n = rhs.k, rhs.n
        pref = 1024 if k < 1024 else (256 if heavy_epilogue else 512)
        return _tile(min(rows_unit, lhs.shape[0]), pref, V7X_SUBLANES_BF16), min(n, 2048)

    def plain(name, lhs, rhs, dtype, scale=1.0):
        tm_, tn_ = tiles(lhs, rhs, False)
        return _proj(name, [lhs], [rhs], _ep_scale(scale), (dtype,), tm=tm_, tn=tn_)[0]

    def roped(name, lhs, rhs, tables, dtype, scale, with_plain):
        tm_, tn_ = tiles(lhs, rhs, True)
        tiles_per_seq = grp.seq // tm_
        extras = [(t, (tm_, t.shape[1]), lambda i, j: (i % tiles_per_seq, 0)) for t in tables]
        outs = (dtype, dtype) if with_plain else (dtype,)
        return _proj(name, [lhs], [rhs], _ep_rope(scale, with_plain), outs, tm=tm_, tn=tn_, extras=extras)

    lat_tm = _tile(rows_unit, 256 if grp.rope else 512, V7X_SUBLANES_BF16)
    rope = None
    if grp.rope:
        cos, sin = _rope_tables(grp.seq)
        t_diff = _widen_tables(cos, sin, 0, 0, A_HEAD_DIM)
        t_mla = _widen_tables(cos, sin, NOPE_DIM, MLA_Q_PAD - NOPE_DIM - ROPE_DIM, MLA_Q_PAD)
        t_kr = _widen_tables(cos, sin, 0, MLA_KR_PAD - ROPE_DIM, MLA_KR_PAD)
        rope = (t_kr[0], t_kr[1], grp.seq // lat_tm)

    wide_cols = w.va[0] + w.va[1]
    lat_steps = m // lat_tm
    cast = None
    if wide_b is None and wide_cols % (lat_steps * V7X_LANES) == 0:
        cast = (w.w_in, w.layer, wide_cols)
    lat_out = _latent_proj(h, w.lat, w.g_cq, w.g_ckv, tm=lat_tm, rope=rope, cast=cast)
    cqn, ckvn, kr = lat_out[:3]
    if wide_b is None:
        wide_b = lat_out[3] if cast is not None else w.w_in[w.layer][:, :wide_cols].astype(BF16)
    wide = lambda cols: _W(wide_b, None, d, 0, cols[0], cols[1])
    w = w._replace(qa=wide(w.qa), ka=wide(w.ka), va=wide(w.va))

    if not grp.rope:
        kv = plain("kv_up", ckvn, w.kv, BF16)
        qa = plain("q_diff", h, w.qa, BF16, qk_scale)
        ka = plain("k_diff", h, w.ka, F32)
        va = plain("v_diff", h, w.va, F32)
        q = plain("q_mla", cqn, w.q_up, BF16, mla_scale)
        oa = _diff_attention(w.lam_vecs, lam_init, [qa], [ka], [va], grp, hps)
        ob = _mla_attention([q], [kv], [kr], grp, hps)
        state = (ka, va, ckvn, kr[:, :ROPE_DIM])
    else:
        ka_c, va_c, ckv_c, kr_c = cache
        kr_r = kr
        kv = plain("kv_up", ckvn, w.kv, BF16)
        qa_u, qa_r = roped("q_diff", h, w.qa, t_diff, BF16, qk_scale, True)
        ka_r, = roped("k_diff", h, w.ka, t_diff, BF16, 1.0, False)
        va = plain("v_diff", h, w.va, BF16)
        q_u, q_r = roped("q_mla", cqn, w.q_up, t_mla, BF16, mla_scale, True)
        kv_c = plain("kv_up_ctx", ckv_c, w.kv, BF16)
        oa = _diff_attention(w.lam_vecs, lam_init, [qa_u, qa_r], [ka_c, ka_r], [va_c, va], grp, hps)
        ob = _mla_attention([q_u, q_r], [kv_c, kv], [kr_c, kr_r], grp, hps)
        state = None
    x1 = _residual_proj("o_proj", [oa, ob], [w.o_a, w.o_b], x, mods, 2,
                        tm=_tile(rows_unit, 1024, V7X_SUBLANES_BF16), tn=512)
    return x1, state, wide_b


def _conv_ffn(x, mods, g2, w_up, conv_w, conv_b, w_down, w_down_b, layer, grp):
    h = _norm(x, g2, mods, 3, 4, BF16)
    act, made = _ffn_up(h, w_up, conv_w, conv_b, layer, grp, None if w_down_b is not None else w_down)
    if w_down_b is None:
        w_down_b = made if made is not None else w_down[layer].astype(BF16)
    out = _residual_proj("ffn_down", [act], [_whole(w_down_b)], x, mods, 5,
                         tm=_tile(x.shape[0] // mods.shape[0], 512, V7X_SUBLANES_BF16), tn=512)
    return out, w_down_b


def kernel(x_prompt, x_sample, cache_diff_k, cache_diff_v, cache_mla_ckv, cache_mla_krope, c, c_ctx,
           norm1_g, norm2_g, w_mod, b_mod, w_in, g_cq, w_q_up, g_ckv, w_kv_up,
           lambda_q1, lambda_k1, lambda_q2, lambda_k2, w_o, w_pool, pool_scale,
           w_up, conv_w, conv_b, w_down, g_final):
    batch, seq, d = x_prompt.shape
    dec_batch, dec_seq, _ = x_sample.shape
    depth = w_mod.shape[0]
    past = cache_diff_k.shape[2]
    groups = (_Group(batch, seq, False), _Group(dec_batch, dec_seq, True))

    n_cond = 1 + dec_batch
    cond8 = jnp.zeros((8, d), F32).at[0].set(c_ctx).at[1:n_cond].set(c)
    mods_all = _adaln(cond8, w_mod, b_mod).reshape(depth, 8, N_MOD, d)
    mods_g = (mods_all[:, 0:1], mods_all[:, 1:n_cond])

    attn_w = {}
    for i in range(0, depth, 2):
        j = i // 2
        lam_vecs = jnp.stack([lambda_q1[j], lambda_k1[j], lambda_q2[j], lambda_k2[j]])
        attn_w[i] = _prep_attn_weights(j, w_in, g_cq[j], w_q_up[j], g_ckv[j], w_kv_up[j], w_o, lam_vecs)
    wide_b = [None] * depth
    w_down_b = [None] * depth

    xs = [x_prompt.reshape(batch * seq, d), x_sample.reshape(dec_batch * dec_seq, d)]
    states = []
    for gi, grp in enumerate(groups):
        x = xs[gi]
        for i in range(depth):
            mods = mods_g[gi][i]
            h = _norm(x, norm1_g[i], mods, 0, 1, BF16)
            if i % 2 == 0:
                j = i // 2
                lam_init = 0.8 - 0.6 * math.exp(-0.3 * i)
                cache = None
                if grp.rope:
                    kr_c = jnp.pad(cache_mla_krope[:, j].reshape(dec_batch * past, ROPE_DIM),
                                   ((0, 0), (0, MLA_KR_PAD - ROPE_DIM)))
                    cache = (cache_diff_k[:, j].reshape(dec_batch * past, A_HEADS * A_HEAD_DIM),
                             cache_diff_v[:, j].reshape(dec_batch * past, A_HEADS * A_V_DIM),
                             cache_mla_ckv[:, j].reshape(dec_batch * past, -1), kr_c)
                x, st, wide_b[i] = _attn_mixer(h, x, mods, attn_w[i], grp, lam_init, cache, wide_b[i])
                if st is not None:
                    states.append(st)
            else:
                x = _pool_mixer(h, x, mods, w_pool, i // 2, pool_scale[i // 2], grp, 2)
            x, w_down_b[i] = _conv_ffn(x, mods, norm2_g[i], w_up, conv_w, conv_b, w_down, w_down_b[i], i, grp)
        xs[gi] = _norm(x, g_final, None, 0, 0, F32)

    y_prompt = xs[0].reshape(batch, seq, d)
    y_sample = xs[1].reshape(dec_batch, dec_seq, d)
    stack = lambda k, tail: jnp.stack([s[k].reshape((batch, seq) + tail) for s in states], axis=1)
    new_diff_k = stack(0, (A_HEADS, A_HEAD_DIM))
    new_diff_v = stack(1, (A_HEADS, A_V_DIM))
    new_mla_ckv = stack(2, (g_ckv.shape[-1],))
    new_mla_krope = stack(3, (ROPE_DIM,))
    return (y_prompt, y_sample, new_diff_k, new_diff_v, new_mla_ckv, new_mla_krope)
```

```python
import functools
import math
from typing import NamedTuple

import jax
import jax.numpy as jnp
from jax import lax
from jax.experimental import pallas as pl
from jax.experimental.pallas import tpu as pltpu

F32 = jnp.float32
BF16 = jnp.bfloat16

GRID_W = 64
ROPE_BASE = 10000.0
EPS = 1e-6
A_HEADS = 16
A_QK_HALF = 64
A_HEAD_DIM = 2 * A_QK_HALF
A_V_DIM = 128
B_HEADS = 16
NOPE_DIM = 128
ROPE_DIM = 64
B_V_DIM = 128
POOL_WINDOWS = (2, 4, 8, 16)
N_MOD = 6
CONV_W = 3
LOG2_E = math.log2(math.e)

V7X_VMEM_BYTES = 64 * 1024 * 1024
V7X_LANES = 128
V7X_SUBLANES_BF16 = 16
COMPILER_SCRATCH_BYTES = 2 * 1024 * 1024

MLA_Q_PAD = 256
MLA_KR_PAD = MLA_Q_PAD - NOPE_DIM
ROPE_QUARTER = ROPE_DIM // 4
POOL_HALO = V7X_SUBLANES_BF16


class _Group(NamedTuple):
    n_seq: int
    seq: int
    rope: bool


def _tile(n, pref, mult=8):
    if n <= pref:
        return n
    t = (pref // mult) * mult
    while t >= mult:
        if n % t == 0:
            return t
        t -= mult
    return n


def _nbytes(shape, dtype):
    return math.prod(shape) * jnp.dtype(dtype).itemsize


def _call(body, *, name, grid, in_specs, out_specs, out_shape, args, temp_bytes=0, single_buffered=(),
          scratch=(), semantics=None, flags=None):
    multi = isinstance(out_shape, (list, tuple))
    outs = list(out_shape) if multi else [out_shape]
    ospecs = list(out_specs) if multi else [out_specs]
    total = temp_bytes + COMPILER_SCRATCH_BYTES + sum(_nbytes(s, dt) for s, dt in scratch)
    for k, (a, s) in enumerate(zip(args, in_specs)):
        blk = [1 if b is None else b for b in s.block_shape]
        total += _nbytes(blk, a.dtype) * (1 if k in single_buffered else 2)
    for o, s in zip(outs, ospecs):
        blk = [1 if b is None else b for b in s.block_shape]
        total += 2 * _nbytes(blk, o.dtype)
    limit = min(V7X_VMEM_BYTES - 4 * 1024 * 1024, max(total, 16 * 1024 * 1024))
    return pl.pallas_call(
        body, name=name, grid=grid, in_specs=in_specs, out_specs=out_specs, out_shape=out_shape,
        scratch_shapes=[pltpu.VMEM(s, dt) for s, dt in scratch],
        compiler_params=pltpu.CompilerParams(
            dimension_semantics=semantics or ("parallel",) * len(grid), vmem_limit_bytes=int(limit),
            flags=flags),
    )(*args)


def _dot(a, b):
    return jnp.dot(a, b, preferred_element_type=F32)


def _dot_nt(a, b):
    return lax.dot_general(a, b, (((1,), (1,)), ((), ())), preferred_element_type=F32)


def _silu(x):
    return x / (1.0 + jnp.exp(-x))


def _adaln(cond8, w_mod, b_mod):
    depth, d, n = w_mod.shape
    tn = next(t for t in (512, 256, 128) if n % (2 * t) == 0)

    def body(c_ref, wa_ref, wb_ref, b_ref, o_ref):
        s = _silu(c_ref[...]).astype(BF16)
        ya = _dot(s, wa_ref[...].astype(BF16))
        yb = _dot(s, wb_ref[...].astype(BF16))
        o_ref[...] = jnp.concatenate([ya, yb], axis=1) + b_ref[...]

    return _call(
        body, name="adaln", grid=(depth, n // (2 * tn)),
        in_specs=[pl.BlockSpec((8, d), lambda l, j: (0, 0)),
                  pl.BlockSpec((None, d, tn), lambda l, j: (l, 0, 2 * j)),
                  pl.BlockSpec((None, d, tn), lambda l, j: (l, 0, 2 * j + 1)),
                  pl.BlockSpec((None, 1, 2 * tn), lambda l, j: (l, 0, j))],
        out_specs=pl.BlockSpec((None, 8, 2 * tn), lambda l, j: (l, 0, j)),
        out_shape=jax.ShapeDtypeStruct((depth, 8, n), F32),
        args=(cond8, w_mod, w_mod, b_mod.reshape(depth, 1, n)),
        temp_bytes=2 * _nbytes((d, tn), BF16) + 2 * _nbytes((d, tn), F32))


def _norm(x, g, mods, k_shift, k_scale, out_dtype):
    m, d = x.shape
    modulated = mods is not None
    tm = _tile(m // mods.shape[0] if modulated else m, 512)

    def body(*refs):
        x_ref, g_ref = refs[0], refs[1]
        o_ref = refs[-1]
        xf = x_ref[...]
        y = xf * lax.rsqrt(jnp.mean(xf * xf, axis=-1, keepdims=True) + EPS) * g_ref[...]
        if modulated:
            mm = refs[2][...]
            y = y * (1.0 + mm[k_scale:k_scale + 1]) + mm[k_shift:k_shift + 1]
        o_ref[...] = y.astype(o_ref.dtype)

    in_specs = [pl.BlockSpec((tm, d), lambda i: (i, 0)), pl.BlockSpec((1, d), lambda i: (0, 0))]
    args = [x, g.reshape(1, d)]
    if modulated:
        rows_per_cond = m // mods.shape[0]
        in_specs.append(pl.BlockSpec((None, N_MOD, d), lambda i: ((i * tm) // rows_per_cond, 0, 0)))
        args.append(mods)
    return _call(body, name="norm", grid=(m // tm,), in_specs=in_specs,
                 out_specs=pl.BlockSpec((tm, d), lambda i: (i, 0)),
                 out_shape=jax.ShapeDtypeStruct((m, d), out_dtype), args=args,
                 temp_bytes=3 * _nbytes((tm, d), F32))


class _W(NamedTuple):
    arr: jax.Array
    layer: object
    k: int
    row_blk: int
    col0: int
    n: int

    def spec(self, tn, col=lambda i, j: j, **mode):
        assert self.col0 % tn == 0 and self.n % tn == 0
        c0 = self.col0 // tn
        if self.layer is None:
            return pl.BlockSpec((self.k, tn), lambda i, j: (self.row_blk, c0 + col(i, j)), **mode)
        return pl.BlockSpec((None, self.k, tn),
                            lambda i, j: (self.layer, self.row_blk, c0 + col(i, j)), **mode)


def _whole(arr):
    return _W(arr, None, arr.shape[0], 0, 0, arr.shape[1])


def _proj(name, lhs, rhs, epilogue, outs, *, tm, tn, extras=()):
    m = lhs[0].shape[0]
    n = rhs[0].n
    np_ = len(lhs)
    ne = len(extras)

    def body(*refs):
        acc = None
        for p in range(np_):
            part = _dot(refs[p][...].astype(BF16), refs[np_ + p][...].astype(BF16))
            acc = part if acc is None else acc + part
        epilogue(acc, refs[2 * np_:2 * np_ + ne], refs[2 * np_ + ne:])

    resident = n == tn
    rhs_mode = dict(pipeline_mode=pl.Buffered(1)) if resident else {}
    nj = n // tn
    col = (lambda i, j: j) if nj == 1 else (lambda i, j: jnp.where(i % 2 == 0, j, nj - 1 - j))
    in_specs = ([pl.BlockSpec((tm, a.shape[1]), lambda i, j: (i, 0)) for a in lhs]
                + [w.spec(tn, col, **rhs_mode) for w in rhs]
                + [pl.BlockSpec(bs, lambda i, j, im=im: im(i, col(i, j))) for (_, bs, im) in extras])
    args = list(lhs) + [w.arr for w in rhs] + [e[0] for e in extras]
    out_shape = [jax.ShapeDtypeStruct((m, n), dt) for dt in outs]
    out_specs = [pl.BlockSpec((tm, tn), lambda i, j: (i, col(i, j))) for _ in outs]
    return _call(body, name=name, grid=(m // tm, nj), in_specs=in_specs, out_specs=out_specs,
                 out_shape=out_shape, args=args, temp_bytes=6 * _nbytes((tm, tn), F32),
                 single_buffered=tuple(range(np_, 2 * np_)) if resident else ())


def _ep_scale(scale):
    def ep(acc, ex, outs):
        outs[0][...] = (acc * scale).astype(outs[0].dtype)
    return ep


def _rotate(a, cos, sin):
    width = a.shape[1]
    lane = lax.broadcasted_iota(jnp.int32, a.shape, 1)
    low = (lane & (2 * ROPE_QUARTER - 1)) < ROPE_QUARTER
    partner = jnp.where(low, pltpu.roll(a, width - ROPE_QUARTER, 1), pltpu.roll(a, ROPE_QUARTER, 1))
    return a * cos + partner * sin


def _ep_rope(scale, with_plain):
    def ep(acc, ex, outs):
        a = acc * scale
        reps = a.shape[1] // ex[0].shape[1]
        cos = jnp.concatenate([ex[0][...]] * reps, axis=1)
        sin = jnp.concatenate([ex[1][...]] * reps, axis=1)
        r = _rotate(a, cos, sin)
        if with_plain:
            outs[0][...] = a.astype(outs[0].dtype)
        outs[-1][...] = r.astype(outs[-1].dtype)
    return ep


def _ep_residual(k_gate):
    def ep(acc, ex, outs):
        gate = ex[1][...][k_gate:k_gate + 1]
        outs[0][...] = ex[0][...] + gate * acc
    return ep


def _rope_tables(seq):
    half = ROPE_DIM // 2
    inv = ROPE_BASE ** (-jnp.arange(0, half, 2, dtype=F32) / half)
    pos = jnp.arange(seq)
    ar = (pos // GRID_W).astype(F32)[:, None] * inv[None, :]
    ac = (pos % GRID_W).astype(F32)[:, None] * inv[None, :]
    cos = jnp.concatenate([jnp.cos(ar), jnp.cos(ar), jnp.cos(ac), jnp.cos(ac)], axis=1)
    sin = jnp.concatenate([-jnp.sin(ar), jnp.sin(ar), -jnp.sin(ac), jnp.sin(ac)], axis=1)
    return cos, sin


def _widen_tables(cos, sin, lead, trail, width):
    seq = cos.shape[0]
    c = jnp.concatenate([jnp.ones((seq, lead), F32), cos, jnp.ones((seq, trail), F32)], axis=1)
    s = jnp.concatenate([jnp.zeros((seq, lead), F32), sin, jnp.zeros((seq, trail), F32)], axis=1)
    reps = width // c.shape[1]
    assert reps * c.shape[1] == width
    return jnp.tile(c, (1, reps)), jnp.tile(s, (1, reps))


def _diff_attention(lam_vecs, lam_init, qs, ks, vs, grp, hps):
    m = qs[0].shape[0]
    nseg = len(qs)
    tq = _tile(grp.seq, 256)
    qt = grp.seq // tq
    width = hps * A_HEAD_DIM
    n_hg = A_HEADS // hps
    k_lens = [k.shape[0] // grp.n_seq for k in ks]

    def body(*refs):
        lam_ref = refs[0]
        q_refs = refs[1:1 + nseg]
        k_refs = refs[1 + nseg:1 + 2 * nseg]
        v_refs = refs[1 + 2 * nseg:1 + 3 * nseg]
        o_ref = refs[-1]
        lv = lam_ref[...]
        lam = (jnp.exp(jnp.sum(lv[0:1] * lv[1:2], axis=-1, keepdims=True))
               - jnp.exp(jnp.sum(lv[2:3] * lv[3:4], axis=-1, keepdims=True)) + lam_init)
        first = lax.broadcasted_iota(jnp.int32, (tq, A_HEAD_DIM), 1) < A_QK_HALF
        for g in range(hps):
            cols = slice(g * A_HEAD_DIM, (g + 1) * A_HEAD_DIM)
            scores = []
            for q_ref, k_ref in zip(q_refs, k_refs):
                q = q_ref[:, cols]
                zero = jnp.zeros_like(q)
                q2 = jnp.concatenate([jnp.where(first, q, zero), jnp.where(first, zero, q)], axis=0)
                scores.append(_dot_nt(q2, k_ref[:, cols].astype(BF16)))
            mx = functools.reduce(jnp.maximum, [jnp.max(s, axis=-1, keepdims=True) for s in scores])
            ps = [jnp.exp2(s - mx) for s in scores]
            den = functools.reduce(jnp.add, [jnp.sum(p, axis=-1, keepdims=True) for p in ps])
            o2 = functools.reduce(
                jnp.add, [_dot(p.astype(BF16), v_ref[:, cols].astype(BF16)) for p, v_ref in zip(ps, v_refs)])
            o2 = o2 / den
            o = o2[:tq] - lam * o2[tq:]
            o = o * lax.rsqrt(jnp.mean(o * o, axis=-1, keepdims=True) + EPS) * (1.0 - lam_init)
            o_ref[:, cols] = o.astype(o_ref.dtype)

    q_spec = pl.BlockSpec((tq, width), lambda b, h, i: (b * qt + i, h))
    kv_mode = dict(pipeline_mode=pl.Buffered(1)) if qt > 1 else {}
    in_specs = ([pl.BlockSpec(lam_vecs.shape, lambda b, h, i: (0, 0))]
                + [q_spec] * nseg
                + [pl.BlockSpec((kl, width), lambda b, h, i: (b, h), **kv_mode) for kl in k_lens] * 2)
    nk = sum(k_lens)
    return _call(body, name="diff_attention", grid=(grp.n_seq, n_hg, qt), in_specs=in_specs,
                 out_specs=q_spec, out_shape=jax.ShapeDtypeStruct((m, A_HEADS * A_V_DIM), BF16),
                 args=[lam_vecs] + list(qs) + list(ks) + list(vs),
                 temp_bytes=(hps + 1) * _nbytes((2 * tq, nk), F32),
                 single_buffered=tuple(range(1 + nseg, 1 + 3 * nseg)) if kv_mode else ())


def _mla_attention(qs, kvs, krs, grp, hps):
    m = qs[0].shape[0]
    nseg = len(qs)
    tq = _tile(grp.seq, 256)
    qt = grp.seq // tq
    n_hg = B_HEADS // hps
    k_lens = [kv.shape[0] // grp.n_seq for kv in kvs]

    def body(*refs):
        q_refs = refs[:nseg]
        kn_refs = refs[nseg:2 * nseg]
        kr_refs = refs[2 * nseg:3 * nseg]
        v_refs = refs[3 * nseg:4 * nseg]
        o_ref = refs[-1]
        krs_v = [r[...].astype(BF16) for r in kr_refs]
        for g in range(hps):
            kcols = slice(g * NOPE_DIM, (g + 1) * NOPE_DIM)
            vcols = slice(g * B_V_DIM, (g + 1) * B_V_DIM)
            scores = []
            for q_ref, kn_ref, kr in zip(q_refs, kn_refs, krs_v):
                q = q_ref[:, g * MLA_Q_PAD:(g + 1) * MLA_Q_PAD]
                k = jnp.concatenate([kn_ref[:, kcols], kr], axis=1)
                scores.append(_dot_nt(q, k))
            mx = functools.reduce(jnp.maximum, [jnp.max(s, axis=-1, keepdims=True) for s in scores])
            ps = [jnp.exp2(s - mx) for s in scores]
            den = functools.reduce(jnp.add, [jnp.sum(p, axis=-1, keepdims=True) for p in ps])
            o = functools.reduce(
                jnp.add, [_dot(p.astype(BF16), v_ref[:, vcols]) for p, v_ref in zip(ps, v_refs)])
            o_ref[:, vcols] = (o / den).astype(o_ref.dtype)

    q_spec = pl.BlockSpec((tq, hps * MLA_Q_PAD), lambda b, h, i: (b * qt + i, h))
    assert NOPE_DIM == B_V_DIM
    k_specs = [pl.BlockSpec((kl, hps * NOPE_DIM), lambda b, h, i: (b, h)) for kl in k_lens]
    v_specs = [pl.BlockSpec((kl, hps * B_V_DIM), lambda b, h, i: (b, n_hg + h)) for kl in k_lens]
    kr_specs = [pl.BlockSpec((kl, MLA_KR_PAD), lambda b, h, i: (b, 0)) for kl in k_lens]
    nk = sum(k_lens)
    return _call(body, name="mla_attention", grid=(grp.n_seq, n_hg, qt),
                 in_specs=[q_spec] * nseg + k_specs + kr_specs + v_specs,
                 out_specs=pl.BlockSpec((tq, hps * B_V_DIM), lambda b, h, i: (b * qt + i, h)),
                 out_shape=jax.ShapeDtypeStruct((m, B_HEADS * B_V_DIM), BF16),
                 args=list(qs) + list(kvs) + list(krs) + list(kvs),
                 temp_bytes=4 * _nbytes((tq, nk), F32) + 2 * _nbytes((nk, MLA_Q_PAD), BF16))


def _pool_mixer(h, x, mods, w_pool, layer, pool_scale, grp, k_gate):
    m, d = x.shape
    _, n_groups, c, _ = w_pool.shape
    assert max(POOL_WINDOWS) // 2 <= POOL_HALO and n_groups == len(POOL_WINDOWS)
    rows_per_cond = m // mods.shape[0]
    r = _tile(rows_per_cond, 512, POOL_HALO)
    nt = m // r
    halo_per_tile = r // POOL_HALO
    kc = r + 2 * POOL_HALO
    period = math.lcm(r, grp.seq) // r
    member, inv_len = _pool_tables(grp.seq, r, period, kc)

    def body(hp_ref, hc_ref, hn_ref, mem_ref, inv_ref, w_ref, ps_ref, x_ref, m_ref, o_ref):
        hc = hc_ref[...]
        hcat = jnp.concatenate([hp_ref[...], hc, hn_ref[...]], axis=0)
        win = _dot(mem_ref[...], hcat)
        inv = jnp.concatenate([inv_ref[...]] * (c // V7X_LANES), axis=1)
        pooled = win * inv - hc.astype(F32)
        y = _dot(pooled.astype(BF16), w_ref[...].astype(BF16)) * ps_ref[...]
        gate = m_ref[...][k_gate:k_gate + 1]
        o_ref[...] = x_ref[...] + gate * y

    last_halo = m // POOL_HALO - 1
    in_specs = [
        pl.BlockSpec((POOL_HALO, c), lambda g, i: (jnp.maximum(i * halo_per_tile - 1, 0), g)),
        pl.BlockSpec((r, c), lambda g, i: (i, g)),
        pl.BlockSpec((POOL_HALO, c), lambda g, i: (jnp.minimum((i + 1) * halo_per_tile, last_halo), g)),
        pl.BlockSpec((None, None, r, kc), lambda g, i: (g, i % period, 0, 0)),
        pl.BlockSpec((None, r, V7X_LANES), lambda g, i: (g, i % period, 0)),
        pl.BlockSpec((None, None, c, c), lambda g, i: (layer, g, 0, 0)),
        pl.BlockSpec((1, c), lambda g, i: (0, g)),
        pl.BlockSpec((r, c), lambda g, i: (i, g)),
        pl.BlockSpec((None, N_MOD, c), lambda g, i: ((i * r) // rows_per_cond, 0, g)),
    ]
    return _call(body, name="pool_mixer", grid=(n_groups, nt), in_specs=in_specs,
                 out_specs=pl.BlockSpec((r, c), lambda g, i: (i, g)),
                 out_shape=jax.ShapeDtypeStruct((m, d), F32),
                 args=[h, h, h, member, inv_len, w_pool, pool_scale.reshape(1, d), x, mods],
                 temp_bytes=4 * _nbytes((r, c), F32))


def _pool_tables(seq, r, period, kc):
    p = jnp.arange(period)[:, None, None]
    row = p * r + jnp.arange(r)[None, :, None]
    col = p * r - POOL_HALO + jnp.arange(kc)[None, None, :]
    seq_start = (row // seq) * seq
    members, invs = [], []
    for w in POOL_WINDOWS:
        lo = jnp.maximum(row - w // 2, seq_start)
        hi = jnp.minimum(row + w - w // 2, seq_start + seq)
        members.append(((col >= lo) & (col < hi)).astype(BF16))
        invs.append(jnp.broadcast_to(1.0 / (hi - lo).astype(F32), (period, r, V7X_LANES)))
    inv = jnp.stack(invs).reshape(len(POOL_WINDOWS), period * r, V7X_LANES)
    return jnp.stack(members), inv


def _ffn_up(h, w_up, conv_w, conv_b, layer, grp, w_down=None):
    m, d = h.shape
    f = w_up.shape[2] // 2
    tm = _tile(m, max(grp.seq, 2048), grp.seq)
    tn = _tile(f, 256, V7X_LANES)
    nj = f // tn
    seq = grp.seq
    steps = (m // tm) * nj
    fuse_cast = w_down is not None and w_down.shape[1] % (steps * V7X_SUBLANES_BF16) == 0
    slab_rows = w_down.shape[1] // steps if fuse_cast else 0

    rc = _tile(seq, 512)
    pad = 8
    rs = _tile(tm, 512, V7X_SUBLANES_BF16)

    def body(*refs):
        h_ref, wa_ref, wb_ref, cwa_ref, cwb_ref, cba_ref, cbb_ref = refs[:7]
        if fuse_cast:
            wd_ref, o_ref, wdo_ref, acc_ref = refs[7:]
            wdo_ref[...] = wd_ref[...].astype(BF16)
        else:
            o_ref, acc_ref = refs[7:]
        row8 = lax.broadcasted_iota(jnp.int32, (8, tn), 0)

        def matmul(s):
            xs = h_ref[s * rs:(s + 1) * rs, :]
            for k, w_ref in enumerate((wa_ref, wb_ref)):
                acc_ref[k, pad + s * rs:pad + (s + 1) * rs] = _dot(xs, w_ref[...].astype(BF16))

        def piece(p0, p1):
            n = p1 - p0

            def conv(k, cw_ref, cb_ref):
                cw = cw_ref[...]
                u = acc_ref[k, pad + p0:pad + p1]
                prev = acc_ref[k, pad + p0 - 1:pad + p1 - 1]
                nxt = acc_ref[k, pad + p0 + 1:pad + p1 + 1]
                top = jnp.where(row8 == 0, 0.0, prev[:8]) if p0 % seq == 0 else prev[:8]
                prev = top if n == 8 else jnp.concatenate([top, prev[8:]], axis=0)
                bot = jnp.where(row8 == 7, 0.0, nxt[n - 8:]) if p1 % seq == 0 else nxt[n - 8:]
                nxt = bot if n == 8 else jnp.concatenate([nxt[:n - 8], bot], axis=0)
                return prev * cw[0:1] + u * cw[1:2] + nxt * cw[2:3] + cb_ref[...]

            a = conv(0, cwa_ref, cba_ref)
            b = conv(1, cwb_ref, cbb_ref)
            o_ref[p0:p1] = (_silu(a) * b).astype(o_ref.dtype)

        def epilogue(r0, r1):
            cuts = sorted({r0, r1} | {c for c in range(0, tm + 1, rc) if r0 < c < r1})
            for p0, p1 in zip(cuts[:-1], cuts[1:]):
                piece(p0, p1)

        n_stage = tm // rs
        for k in range(2):
            acc_ref[k, 0:pad] = jnp.zeros((pad, tn), F32)
            acc_ref[k, pad + tm:2 * pad + tm] = jnp.zeros((pad, tn), F32)
        for s in range(n_stage + 1):
            if s > 0:
                epilogue(max((s - 1) * rs - pad, 0), tm if s == n_stage else s * rs - pad)
            if s < n_stage:
                matmul(s)

    col = lambda i, j: jnp.where(i % 2 == 0, j, nj - 1 - j)
    in_specs = [
        pl.BlockSpec((tm, d), lambda i, j: (i, 0)),
        pl.BlockSpec((None, d, tn), lambda i, j: (layer, 0, col(i, j))),
        pl.BlockSpec((None, d, tn), lambda i, j: (layer, 0, nj + col(i, j))),
        pl.BlockSpec((None, CONV_W, tn), lambda i, j: (layer, 0, col(i, j))),
        pl.BlockSpec((None, CONV_W, tn), lambda i, j: (layer, 0, nj + col(i, j))),
        pl.BlockSpec((None, 1, tn), lambda i, j: (layer, 0, col(i, j))),
        pl.BlockSpec((None, 1, tn), lambda i, j: (layer, 0, nj + col(i, j))),
    ]
    cb = conv_b.reshape(conv_b.shape[0], 1, 2 * f)
    args = [h, w_up, w_up, conv_w, conv_w, cb, cb]
    out_specs = [pl.BlockSpec((tm, tn), lambda i, j: (i, col(i, j)))]
    out_shape = [jax.ShapeDtypeStruct((m, f), BF16)]
    if fuse_cast:
        slab = (slab_rows, w_down.shape[2])
        in_specs.append(pl.BlockSpec((None,) + slab, lambda i, j: (layer, i * nj + j, 0)))
        args.append(w_down)
        out_specs.append(pl.BlockSpec(slab, lambda i, j: (i * nj + j, 0)))
        out_shape.append(jax.ShapeDtypeStruct(w_down.shape[1:], BF16))
    res = _call(body, name="ffn_up", grid=(m // tm, nj), in_specs=in_specs, out_specs=out_specs,
                out_shape=out_shape, args=args,
                temp_bytes=10 * _nbytes((rc, tn), F32) + 4 * _nbytes((rs, tn), F32) + 2 * _nbytes((d, tn), BF16),
                scratch=[((2, tm + 2 * pad, tn), F32)])
    return res[0], (res[1] if fuse_cast else None)


def _residual_proj(name, lhs, rhs, x, mods, k_gate, *, tm, tn):
    m = x.shape[0]
    rows_per_cond = m // mods.shape[0]
    assert rows_per_cond % tm == 0
    extras = [(x, (tm, tn), lambda i, j: (i, j)),
              (mods, (None, N_MOD, tn), lambda i, j: ((i * tm) // rows_per_cond, 0, j))]
    return _proj(name, lhs, rhs, _ep_residual(k_gate), (F32,), tm=tm, tn=tn, extras=extras)[0]


class _AttnWeights(NamedTuple):
    qa: _W
    ka: _W
    va: _W
    lat: jax.Array
    g_cq: jax.Array
    g_ckv: jax.Array
    q_up: _W
    kv: _W
    o_a: _W
    o_b: _W
    lam_vecs: jax.Array


def _prep_attn_weights(j, w_in_b, w_in, g_cq, w_q_up, g_ckv, w_kv_up, w_o, lam_vecs):
    d = w_in.shape[0]
    a_q = A_HEADS * A_HEAD_DIM
    a_v = A_HEADS * A_V_DIM
    b_out = B_HEADS * B_V_DIM
    assert a_v == b_out
    q_lora = g_cq.shape[0]
    kv_lora = g_ckv.shape[0]
    o0, o1, o2 = a_q, 2 * a_q, 2 * a_q + a_v
    lat = jnp.pad(w_in[:, o2:], ((0, 0), (0, MLA_KR_PAD - ROPE_DIM))).astype(BF16)
    q_up = w_q_up.reshape(q_lora, B_HEADS, NOPE_DIM + ROPE_DIM)
    q_up = jnp.pad(q_up, ((0, 0), (0, 0), (0, MLA_Q_PAD - NOPE_DIM - ROPE_DIM)))
    kv = w_kv_up.reshape(kv_lora, B_HEADS, NOPE_DIM + B_V_DIM)
    kv = jnp.concatenate([kv[:, :, :NOPE_DIM].reshape(kv_lora, B_HEADS * NOPE_DIM),
                          kv[:, :, NOPE_DIM:].reshape(kv_lora, B_HEADS * B_V_DIM)], axis=1)
    c = lambda w: _whole(w.astype(BF16))
    win = lambda col0, n: _W(w_in_b, j, d, 0, col0, n)
    return _AttnWeights(
        qa=win(0, a_q), ka=win(o0, a_q), va=win(o1, a_v), lat=lat,
        g_cq=g_cq.reshape(1, q_lora), g_ckv=g_ckv.reshape(1, kv_lora),
        q_up=c(q_up.reshape(q_lora, B_HEADS * MLA_Q_PAD)), kv=c(kv),
        o_a=_W(w_o, j, a_v, 0, 0, w_o.shape[-1]), o_b=_W(w_o, j, b_out, 1, 0, w_o.shape[-1]),
        lam_vecs=lam_vecs)


def _latent_proj(h, w_lat, g_cq, g_ckv, *, tm, rope=None):
    m, d = h.shape
    q_lora, kv_lora = g_cq.shape[1], g_ckv.shape[1]
    n = w_lat.shape[1]
    assert n == q_lora + kv_lora + MLA_KR_PAD and q_lora % V7X_LANES == 0 and kv_lora % V7X_LANES == 0

    def rms(a, g):
        return a * lax.rsqrt(jnp.mean(a * a, axis=-1, keepdims=True) + EPS) * g

    def body(*refs):
        h_ref, w_ref, gq_ref, gkv_ref = refs[:4]
        cq_ref, ckv_ref, kr_ref = refs[-3:]
        acc = _dot(h_ref[...], w_ref[...])
        cq_ref[...] = rms(acc[:, :q_lora], gq_ref[...]).astype(cq_ref.dtype)
        ckv_ref[...] = rms(acc[:, q_lora:q_lora + kv_lora], gkv_ref[...])
        kr = acc[:, q_lora + kv_lora:]
        if rope is not None:
            kr = _rotate(kr, refs[4][...], refs[5][...])
        kr_ref[...] = kr.astype(kr_ref.dtype)

    row = lambda width: pl.BlockSpec((tm, width), lambda i: (i, 0))
    in_specs = [row(d), pl.BlockSpec((d, n), lambda i: (0, 0), pipeline_mode=pl.Buffered(1)),
                pl.BlockSpec((1, q_lora), lambda i: (0, 0)), pl.BlockSpec((1, kv_lora), lambda i: (0, 0))]
    args = [h, w_lat, g_cq, g_ckv]
    if rope is not None:
        cos, sin, tiles_per_seq = rope
        in_specs += [pl.BlockSpec((tm, MLA_KR_PAD), lambda i: (i % tiles_per_seq, 0))] * 2
        args += [cos, sin]
    return _call(body, name="latent_proj", grid=(m // tm,), in_specs=in_specs,
                 out_specs=[row(q_lora), row(kv_lora), row(MLA_KR_PAD)],
                 out_shape=[jax.ShapeDtypeStruct((m, q_lora), BF16), jax.ShapeDtypeStruct((m, kv_lora), F32),
                            jax.ShapeDtypeStruct((m, MLA_KR_PAD), F32 if rope is None else BF16)],
                 args=args, temp_bytes=4 * _nbytes((tm, n), F32), single_buffered=(1,))


def _attn_mixer(h, x, mods, w, grp, lam_init, cache):
    m, d = h.shape
    hps = A_HEADS if grp.seq <= 256 else 8
    qk_scale = A_QK_HALF ** -0.5 * LOG2_E
    mla_scale = (NOPE_DIM + ROPE_DIM) ** -0.5 * LOG2_E
    rows_unit = grp.seq if grp.rope else m

    def tiles(lhs, rhs, heavy_epilogue):
        k, n = rhs.k, rhs.n
        pref = 1024 if k < 1024 else (256 if heavy_epilogue else 512)
        return _tile(min(rows_unit, lhs.shape[0]), pref, V7X_SUBLANES_BF16), min(n, 2048)

    def plain(name, lhs, rhs, dtype, scale=1.0):
        tm_, tn_ = tiles(lhs, rhs, False)
        return _proj(name, [lhs], [rhs], _ep_scale(scale), (dtype,), tm=tm_, tn=tn_)[0]

    def roped(name, lhs, rhs, tables, dtype, scale, with_plain):
        tm_, tn_ = tiles(lhs, rhs, True)
        tiles_per_seq = grp.seq // tm_
        extras = [(t, (tm_, t.shape[1]), lambda i, j: (i % tiles_per_seq, 0)) for t in tables]
        outs = (dtype, dtype) if with_plain else (dtype,)
        return _proj(name, [lhs], [rhs], _ep_rope(scale, with_plain), outs, tm=tm_, tn=tn_, extras=extras)

    lat_tm = _tile(rows_unit, 256 if grp.rope else 512, V7X_SUBLANES_BF16)
    if not grp.rope:
        cqn, ckvn, kr = _latent_proj(h, w.lat, w.g_cq, w.g_ckv, tm=lat_tm)
        kv = plain("kv_up", ckvn, w.kv, BF16)
        qa = plain("q_diff", h, w.qa, BF16, qk_scale)
        ka = plain("k_diff", h, w.ka, F32)
        va = plain("v_diff", h, w.va, F32)
        q = plain("q_mla", cqn, w.q_up, BF16, mla_scale)
        oa = _diff_attention(w.lam_vecs, lam_init, [qa], [ka], [va], grp, hps)
        ob = _mla_attention([q], [kv], [kr], grp, hps)
        state = (ka, va, ckvn, kr[:, :ROPE_DIM])
    else:
        cos, sin = _rope_tables(grp.seq)
        t_diff = _widen_tables(cos, sin, 0, 0, A_HEAD_DIM)
        t_mla = _widen_tables(cos, sin, NOPE_DIM, MLA_Q_PAD - NOPE_DIM - ROPE_DIM, MLA_Q_PAD)
        t_kr = _widen_tables(cos, sin, 0, MLA_KR_PAD - ROPE_DIM, MLA_KR_PAD)
        ka_c, va_c, ckv_c, kr_c = cache
        cqn, ckvn, kr_r = _latent_proj(h, w.lat, w.g_cq, w.g_ckv, tm=lat_tm,
                                       rope=(t_kr[0], t_kr[1], grp.seq // lat_tm))
        kv = plain("kv_up", ckvn, w.kv, BF16)
        qa_u, qa_r = roped("q_diff", h, w.qa, t_diff, BF16, qk_scale, True)
        ka_r, = roped("k_diff", h, w.ka, t_diff, BF16, 1.0, False)
        va = plain("v_diff", h, w.va, BF16)
        q_u, q_r = roped("q_mla", cqn, w.q_up, t_mla, BF16, mla_scale, True)
        kv_c = plain("kv_up_ctx", ckv_c, w.kv, BF16)
        oa = _diff_attention(w.lam_vecs, lam_init, [qa_u, qa_r], [ka_c, ka_r], [va_c, va], grp, hps)
        ob = _mla_attention([q_u, q_r], [kv_c, kv], [kr_c, kr_r], grp, hps)
        state = None
    x1 = _residual_proj("o_proj", [oa, ob], [w.o_a, w.o_b], x, mods, 2,
                        tm=_tile(rows_unit, 1024, V7X_SUBLANES_BF16), tn=512)
    return x1, state


def _conv_ffn(x, mods, g2, w_up, conv_w, conv_b, w_down, w_down_b, layer, grp):
    h = _norm(x, g2, mods, 3, 4, BF16)
    act, made = _ffn_up(h, w_up, conv_w, conv_b, layer, grp, None if w_down_b is not None else w_down)
    if w_down_b is None:
        w_down_b = made if made is not None else w_down[layer].astype(BF16)
    out = _residual_proj("ffn_down", [act], [_whole(w_down_b)], x, mods, 5,
                         tm=_tile(x.shape[0] // mods.shape[0], 512, V7X_SUBLANES_BF16), tn=512)
    return out, w_down_b


def kernel(x_prompt, x_sample, cache_diff_k, cache_diff_v, cache_mla_ckv, cache_mla_krope, c, c_ctx,
           norm1_g, norm2_g, w_mod, b_mod, w_in, g_cq, w_q_up, g_ckv, w_kv_up,
           lambda_q1, lambda_k1, lambda_q2, lambda_k2, w_o, w_pool, pool_scale,
           w_up, conv_w, conv_b, w_down, g_final):
    batch, seq, d = x_prompt.shape
    dec_batch, dec_seq, _ = x_sample.shape
    depth = w_mod.shape[0]
    past = cache_diff_k.shape[2]
    groups = (_Group(batch, seq, False), _Group(dec_batch, dec_seq, True))

    n_cond = 1 + dec_batch
    cond8 = jnp.zeros((8, d), F32).at[0].set(c_ctx).at[1:n_cond].set(c)
    mods_all = _adaln(cond8, w_mod, b_mod).reshape(depth, 8, N_MOD, d)
    mods_g = (mods_all[:, 0:1], mods_all[:, 1:n_cond])

    w_in_b = w_in.astype(BF16)
    attn_w = {}
    for i in range(0, depth, 2):
        j = i // 2
        lam_vecs = jnp.stack([lambda_q1[j], lambda_k1[j], lambda_q2[j], lambda_k2[j]])
        attn_w[i] = _prep_attn_weights(j, w_in_b, w_in[j], g_cq[j], w_q_up[j], g_ckv[j], w_kv_up[j],
                                       w_o, lam_vecs)
    w_down_b = [None] * depth

    xs = [x_prompt.reshape(batch * seq, d), x_sample.reshape(dec_batch * dec_seq, d)]
    states = []
    for gi, grp in enumerate(groups):
        x = xs[gi]
        for i in range(depth):
            mods = mods_g[gi][i]
            h = _norm(x, norm1_g[i], mods, 0, 1, BF16)
            if i % 2 == 0:
                j = i // 2
                lam_init = 0.8 - 0.6 * math.exp(-0.3 * i)
                cache = None
                if grp.rope:
                    kr_c = jnp.pad(cache_mla_krope[:, j].reshape(dec_batch * past, ROPE_DIM),
                                   ((0, 0), (0, MLA_KR_PAD - ROPE_DIM)))
                    cache = (cache_diff_k[:, j].reshape(dec_batch * past, A_HEADS * A_HEAD_DIM),
                             cache_diff_v[:, j].reshape(dec_batch * past, A_HEADS * A_V_DIM),
                             cache_mla_ckv[:, j].reshape(dec_batch * past, -1), kr_c)
                x, st = _attn_mixer(h, x, mods, attn_w[i], grp, lam_init, cache)
                if st is not None:
                    states.append(st)
            else:
                x = _pool_mixer(h, x, mods, w_pool, i // 2, pool_scale[i // 2], grp, 2)
            x, w_down_b[i] = _conv_ffn(x, mods, norm2_g[i], w_up, conv_w, conv_b, w_down, w_down_b[i], i, grp)
        xs[gi] = _norm(x, g_final, None, 0, 0, F32)

    y_prompt = xs[0].reshape(batch, seq, d)
    y_sample = xs[1].reshape(dec_batch, dec_seq, d)
    stack = lambda k, tail: jnp.stack([s[k].reshape((batch, seq) + tail) for s in states], axis=1)
    new_diff_k = stack(0, (A_HEADS, A_HEAD_DIM))
    new_diff_v = stack(1, (A_HEADS, A_V_DIM))
    new_mla_ckv = stack(2, (g_ckv.shape[-1],))
    new_mla_krope = stack(3, (ROPE_DIM,))
    return (y_prompt, y_sample, new_diff_k, new_diff_v, new_mla_ckv, new_mla_krope)
```

```python
import functools
import math
from typing import NamedTuple

import jax
import jax.numpy as jnp
from jax import lax
from jax.experimental import pallas as pl
from jax.experimental.pallas import tpu as pltpu

F32 = jnp.float32
BF16 = jnp.bfloat16

GRID_W = 64
ROPE_BASE = 10000.0
EPS = 1e-6
A_HEADS = 16
A_QK_HALF = 64
A_HEAD_DIM = 2 * A_QK_HALF
A_V_DIM = 128
B_HEADS = 16
NOPE_DIM = 128
ROPE_DIM = 64
B_V_DIM = 128
POOL_WINDOWS = (2, 4, 8, 16)
N_MOD = 6
CONV_W = 3
LOG2_E = math.log2(math.e)

V7X_VMEM_BYTES = 64 * 1024 * 1024
V7X_LANES = 128
V7X_SUBLANES_BF16 = 16
COMPILER_SCRATCH_BYTES = 2 * 1024 * 1024

MLA_Q_PAD = 256
MLA_KR_PAD = MLA_Q_PAD - NOPE_DIM
ROPE_QUARTER = ROPE_DIM // 4
POOL_HALO = V7X_SUBLANES_BF16
ROWSUM_IN_MATMUL_MIN_KEYS = 1024


class _Group(NamedTuple):
    n_seq: int
    seq: int
    rope: bool


def _tile(n, pref, mult=8):
    if n <= pref:
        return n
    t = (pref // mult) * mult
    while t >= mult:
        if n % t == 0:
            return t
        t -= mult
    return n


def _nbytes(shape, dtype):
    return math.prod(shape) * jnp.dtype(dtype).itemsize


def _call(body, *, name, grid, in_specs, out_specs, out_shape, args, temp_bytes=0, single_buffered=(),
          scratch=(), semantics=None, flags=None):
    multi = isinstance(out_shape, (list, tuple))
    outs = list(out_shape) if multi else [out_shape]
    ospecs = list(out_specs) if multi else [out_specs]
    total = temp_bytes + COMPILER_SCRATCH_BYTES + sum(_nbytes(s, dt) for s, dt in scratch)
    for k, (a, s) in enumerate(zip(args, in_specs)):
        blk = [1 if b is None else b for b in s.block_shape]
        total += _nbytes(blk, a.dtype) * (1 if k in single_buffered else 2)
    for o, s in zip(outs, ospecs):
        blk = [1 if b is None else b for b in s.block_shape]
        total += 2 * _nbytes(blk, o.dtype)
    limit = min(V7X_VMEM_BYTES - 4 * 1024 * 1024, max(total, 16 * 1024 * 1024))
    return pl.pallas_call(
        body, name=name, grid=grid, in_specs=in_specs, out_specs=out_specs, out_shape=out_shape,
        scratch_shapes=[pltpu.VMEM(s, dt) for s, dt in scratch],
        compiler_params=pltpu.CompilerParams(
            dimension_semantics=semantics or ("parallel",) * len(grid), vmem_limit_bytes=int(limit),
            flags=flags),
    )(*args)


def _dot(a, b):
    return jnp.dot(a, b, preferred_element_type=F32)


def _dot_nt(a, b):
    return lax.dot_general(a, b, (((1,), (1,)), ((), ())), preferred_element_type=F32)


def _silu(x):
    return x / (1.0 + jnp.exp(-x))


def _adaln(cond8, w_mod, b_mod):
    depth, d, n = w_mod.shape
    tn = next(t for t in (512, 256, 128) if n % (2 * t) == 0)

    def body(c_ref, wa_ref, wb_ref, b_ref, o_ref):
        s = _silu(c_ref[...]).astype(BF16)
        ya = _dot(s, wa_ref[...].astype(BF16))
        yb = _dot(s, wb_ref[...].astype(BF16))
        o_ref[...] = jnp.concatenate([ya, yb], axis=1) + b_ref[...]

    return _call(
        body, name="adaln", grid=(depth, n // (2 * tn)),
        in_specs=[pl.BlockSpec((8, d), lambda l, j: (0, 0)),
                  pl.BlockSpec((None, d, tn), lambda l, j: (l, 0, 2 * j)),
                  pl.BlockSpec((None, d, tn), lambda l, j: (l, 0, 2 * j + 1)),
                  pl.BlockSpec((None, 1, 2 * tn), lambda l, j: (l, 0, j))],
        out_specs=pl.BlockSpec((None, 8, 2 * tn), lambda l, j: (l, 0, j)),
        out_shape=jax.ShapeDtypeStruct((depth, 8, n), F32),
        args=(cond8, w_mod, w_mod, b_mod.reshape(depth, 1, n)),
        temp_bytes=2 * _nbytes((d, tn), BF16) + 2 * _nbytes((d, tn), F32))


def _norm(x, g, mods, k_shift, k_scale, out_dtype):
    m, d = x.shape
    modulated = mods is not None
    tm = _tile(m // mods.shape[0] if modulated else m, 512)

    def body(*refs):
        x_ref, g_ref = refs[0], refs[1]
        o_ref = refs[-1]
        xf = x_ref[...]
        y = xf * lax.rsqrt(jnp.mean(xf * xf, axis=-1, keepdims=True) + EPS) * g_ref[...]
        if modulated:
            mm = refs[2][...]
            y = y * (1.0 + mm[k_scale:k_scale + 1]) + mm[k_shift:k_shift + 1]
        o_ref[...] = y.astype(o_ref.dtype)

    in_specs = [pl.BlockSpec((tm, d), lambda i: (i, 0)), pl.BlockSpec((1, d), lambda i: (0, 0))]
    args = [x, g.reshape(1, d)]
    if modulated:
        rows_per_cond = m // mods.shape[0]
        in_specs.append(pl.BlockSpec((None, N_MOD, d), lambda i: ((i * tm) // rows_per_cond, 0, 0)))
        args.append(mods)
    return _call(body, name="norm", grid=(m // tm,), in_specs=in_specs,
                 out_specs=pl.BlockSpec((tm, d), lambda i: (i, 0)),
                 out_shape=jax.ShapeDtypeStruct((m, d), out_dtype), args=args,
                 temp_bytes=3 * _nbytes((tm, d), F32))


class _W(NamedTuple):
    arr: jax.Array
    layer: object
    k: int
    row_blk: int
    col0: int
    n: int

    def spec(self, tn, col=lambda i, j: j, **mode):
        assert self.col0 % tn == 0 and self.n % tn == 0
        c0 = self.col0 // tn
        if self.layer is None:
            return pl.BlockSpec((self.k, tn), lambda i, j: (self.row_blk, c0 + col(i, j)), **mode)
        return pl.BlockSpec((None, self.k, tn),
                            lambda i, j: (self.layer, self.row_blk, c0 + col(i, j)), **mode)


def _whole(arr):
    return _W(arr, None, arr.shape[0], 0, 0, arr.shape[1])


def _proj(name, lhs, rhs, epilogue, outs, *, tm, tn, extras=()):
    m = lhs[0].shape[0]
    n = rhs[0].n
    np_ = len(lhs)
    ne = len(extras)

    def body(*refs):
        acc = None
        for p in range(np_):
            part = _dot(refs[p][...].astype(BF16), refs[np_ + p][...].astype(BF16))
            acc = part if acc is None else acc + part
        epilogue(acc, refs[2 * np_:2 * np_ + ne], refs[2 * np_ + ne:])

    resident = n == tn
    rhs_mode = dict(pipeline_mode=pl.Buffered(1)) if resident else {}
    nj = n // tn
    col = (lambda i, j: j) if nj == 1 else (lambda i, j: jnp.where(i % 2 == 0, j, nj - 1 - j))
    in_specs = ([pl.BlockSpec((tm, a.shape[1]), lambda i, j: (i, 0)) for a in lhs]
                + [w.spec(tn, col, **rhs_mode) for w in rhs]
                + [pl.BlockSpec(bs, lambda i, j, im=im: im(i, col(i, j))) for (_, bs, im) in extras])
    args = list(lhs) + [w.arr for w in rhs] + [e[0] for e in extras]
    out_shape = [jax.ShapeDtypeStruct((m, n), dt) for dt in outs]
    out_specs = [pl.BlockSpec((tm, tn), lambda i, j: (i, col(i, j))) for _ in outs]
    return _call(body, name=name, grid=(m // tm, nj), in_specs=in_specs, out_specs=out_specs,
                 out_shape=out_shape, args=args, temp_bytes=6 * _nbytes((tm, tn), F32),
                 single_buffered=tuple(range(np_, 2 * np_)) if resident else ())


def _ep_scale(scale):
    def ep(acc, ex, outs):
        outs[0][...] = (acc * scale).astype(outs[0].dtype)
    return ep


def _rotate(a, cos, sin):
    width = a.shape[1]
    lane = lax.broadcasted_iota(jnp.int32, a.shape, 1)
    low = (lane & (2 * ROPE_QUARTER - 1)) < ROPE_QUARTER
    partner = jnp.where(low, pltpu.roll(a, width - ROPE_QUARTER, 1), pltpu.roll(a, ROPE_QUARTER, 1))
    return a * cos + partner * sin


def _ep_rope(scale, with_plain):
    def ep(acc, ex, outs):
        a = acc * scale
        reps = a.shape[1] // ex[0].shape[1]
        cos = jnp.concatenate([ex[0][...]] * reps, axis=1)
        sin = jnp.concatenate([ex[1][...]] * reps, axis=1)
        r = _rotate(a, cos, sin)
        if with_plain:
            outs[0][...] = a.astype(outs[0].dtype)
        outs[-1][...] = r.astype(outs[-1].dtype)
    return ep


def _ep_residual(k_gate):
    def ep(acc, ex, outs):
        gate = ex[1][...][k_gate:k_gate + 1]
        outs[0][...] = ex[0][...] + gate * acc
    return ep


def _rope_tables(seq):
    half = ROPE_DIM // 2
    inv = ROPE_BASE ** (-jnp.arange(0, half, 2, dtype=F32) / half)
    pos = jnp.arange(seq)
    ar = (pos // GRID_W).astype(F32)[:, None] * inv[None, :]
    ac = (pos % GRID_W).astype(F32)[:, None] * inv[None, :]
    cos = jnp.concatenate([jnp.cos(ar), jnp.cos(ar), jnp.cos(ac), jnp.cos(ac)], axis=1)
    sin = jnp.concatenate([-jnp.sin(ar), jnp.sin(ar), -jnp.sin(ac), jnp.sin(ac)], axis=1)
    return cos, sin


def _widen_tables(cos, sin, lead, trail, width):
    seq = cos.shape[0]
    c = jnp.concatenate([jnp.ones((seq, lead), F32), cos, jnp.ones((seq, trail), F32)], axis=1)
    s = jnp.concatenate([jnp.zeros((seq, lead), F32), sin, jnp.zeros((seq, trail), F32)], axis=1)
    reps = width // c.shape[1]
    assert reps * c.shape[1] == width
    return jnp.tile(c, (1, reps)), jnp.tile(s, (1, reps))


def _pv_and_rowsum(ps, vs):
    if sum(v.shape[0] for v in vs) < ROWSUM_IN_MATMUL_MIN_KEYS:
        den = functools.reduce(jnp.add, [jnp.sum(p, axis=-1, keepdims=True) for p in ps])
        return functools.reduce(jnp.add, [_dot(p.astype(v.dtype), v) for p, v in zip(ps, vs)]), den
    acc = None
    for p, v in zip(ps, vs):
        ones = (lax.broadcasted_iota(jnp.int32, (v.shape[0], V7X_LANES), 1) == 0).astype(v.dtype)
        part = _dot(p.astype(v.dtype), jnp.concatenate([v, ones], axis=1))
        acc = part if acc is None else acc + part
    width = vs[0].shape[1]
    return acc[:, :width], acc[:, width:width + 1]


def _diff_attention(lam_vecs, lam_init, qs, ks, vs, grp, hps):
    m = qs[0].shape[0]
    nseg = len(qs)
    tq = _tile(grp.seq, 256)
    qt = grp.seq // tq
    width = hps * A_HEAD_DIM
    n_hg = A_HEADS // hps
    k_lens = [k.shape[0] // grp.n_seq for k in ks]

    def body(*refs):
        lam_ref = refs[0]
        q_refs = refs[1:1 + nseg]
        k_refs = refs[1 + nseg:1 + 2 * nseg]
        v_refs = refs[1 + 2 * nseg:1 + 3 * nseg]
        o_ref = refs[-1]
        lv = lam_ref[...]
        lam = (jnp.exp(jnp.sum(lv[0:1] * lv[1:2], axis=-1, keepdims=True))
               - jnp.exp(jnp.sum(lv[2:3] * lv[3:4], axis=-1, keepdims=True)) + lam_init)
        first = lax.broadcasted_iota(jnp.int32, (tq, A_HEAD_DIM), 1) < A_QK_HALF
        for g in range(hps):
            cols = slice(g * A_HEAD_DIM, (g + 1) * A_HEAD_DIM)
            scores = []
            for q_ref, k_ref in zip(q_refs, k_refs):
                q = q_ref[:, cols]
                zero = jnp.zeros_like(q)
                q2 = jnp.concatenate([jnp.where(first, q, zero), jnp.where(first, zero, q)], axis=0)
                scores.append(_dot_nt(q2, k_ref[:, cols].astype(BF16)))
            mx = functools.reduce(jnp.maximum, [jnp.max(s, axis=-1, keepdims=True) for s in scores])
            ps = [jnp.exp2(s - mx) for s in scores]
            o2, den = _pv_and_rowsum(ps, [v_ref[:, cols].astype(BF16) for v_ref in v_refs])
            o2 = o2 / den
            o = o2[:tq] - lam * o2[tq:]
            o = o * lax.rsqrt(jnp.mean(o * o, axis=-1, keepdims=True) + EPS) * (1.0 - lam_init)
            o_ref[:, cols] = o.astype(o_ref.dtype)

    q_spec = pl.BlockSpec((tq, width), lambda b, h, i: (b * qt + i, h))
    kv_mode = dict(pipeline_mode=pl.Buffered(1)) if qt > 1 else {}
    in_specs = ([pl.BlockSpec(lam_vecs.shape, lambda b, h, i: (0, 0))]
                + [q_spec] * nseg
                + [pl.BlockSpec((kl, width), lambda b, h, i: (b, h), **kv_mode) for kl in k_lens] * 2)
    nk = sum(k_lens)
    return _call(body, name="diff_attention", grid=(grp.n_seq, n_hg, qt), in_specs=in_specs,
                 out_specs=q_spec, out_shape=jax.ShapeDtypeStruct((m, A_HEADS * A_V_DIM), BF16),
                 args=[lam_vecs] + list(qs) + list(ks) + list(vs),
                 temp_bytes=(hps + 1) * _nbytes((2 * tq, nk), F32),
                 single_buffered=tuple(range(1 + nseg, 1 + 3 * nseg)) if kv_mode else ())


def _mla_attention(qs, kvs, krs, grp, hps):
    m = qs[0].shape[0]
    nseg = len(qs)
    tq = _tile(grp.seq, 256)
    qt = grp.seq // tq
    n_hg = B_HEADS // hps
    k_lens = [kv.shape[0] // grp.n_seq for kv in kvs]

    def body(*refs):
        q_refs = refs[:nseg]
        kn_refs = refs[nseg:2 * nseg]
        kr_refs = refs[2 * nseg:3 * nseg]
        v_refs = refs[3 * nseg:4 * nseg]
        o_ref = refs[-1]
        krs_v = [r[...].astype(BF16) for r in kr_refs]
        for g in range(hps):
            kcols = slice(g * NOPE_DIM, (g + 1) * NOPE_DIM)
            vcols = slice(g * B_V_DIM, (g + 1) * B_V_DIM)
            scores = []
            for q_ref, kn_ref, kr in zip(q_refs, kn_refs, krs_v):
                q = q_ref[:, g * MLA_Q_PAD:(g + 1) * MLA_Q_PAD]
                k = jnp.concatenate([kn_ref[:, kcols], kr], axis=1)
                scores.append(_dot_nt(q, k))
            mx = functools.reduce(jnp.maximum, [jnp.max(s, axis=-1, keepdims=True) for s in scores])
            ps = [jnp.exp2(s - mx) for s in scores]
            o, den = _pv_and_rowsum(ps, [v_ref[:, vcols] for v_ref in v_refs])
            o_ref[:, vcols] = (o / den).astype(o_ref.dtype)

    q_spec = pl.BlockSpec((tq, hps * MLA_Q_PAD), lambda b, h, i: (b * qt + i, h))
    assert NOPE_DIM == B_V_DIM
    k_specs = [pl.BlockSpec((kl, hps * NOPE_DIM), lambda b, h, i: (b, h)) for kl in k_lens]
    v_specs = [pl.BlockSpec((kl, hps * B_V_DIM), lambda b, h, i: (b, n_hg + h)) for kl in k_lens]
    kr_specs = [pl.BlockSpec((kl, MLA_KR_PAD), lambda b, h, i: (b, 0)) for kl in k_lens]
    nk = sum(k_lens)
    return _call(body, name="mla_attention", grid=(grp.n_seq, n_hg, qt),
                 in_specs=[q_spec] * nseg + k_specs + kr_specs + v_specs,
                 out_specs=pl.BlockSpec((tq, hps * B_V_DIM), lambda b, h, i: (b * qt + i, h)),
                 out_shape=jax.ShapeDtypeStruct((m, B_HEADS * B_V_DIM), BF16),
                 args=list(qs) + list(kvs) + list(krs) + list(kvs),
                 temp_bytes=4 * _nbytes((tq, nk), F32) + 2 * _nbytes((nk, MLA_Q_PAD), BF16))


def _pool_mixer(h, x, mods, w_pool, layer, pool_scale, grp, k_gate):
    m, d = x.shape
    _, n_groups, c, _ = w_pool.shape
    assert max(POOL_WINDOWS) // 2 <= POOL_HALO and n_groups == len(POOL_WINDOWS)
    rows_per_cond = m // mods.shape[0]
    r = _tile(rows_per_cond, 512, POOL_HALO)
    nt = m // r
    halo_per_tile = r // POOL_HALO
    kc = r + 2 * POOL_HALO
    period = math.lcm(r, grp.seq) // r
    member, inv_len = _pool_tables(grp.seq, r, period, kc)

    def body(hp_ref, hc_ref, hn_ref, mem_ref, inv_ref, w_ref, ps_ref, x_ref, m_ref, o_ref):
        hc = hc_ref[...]
        hcat = jnp.concatenate([hp_ref[...], hc, hn_ref[...]], axis=0)
        win = _dot(mem_ref[...], hcat)
        inv = jnp.concatenate([inv_ref[...]] * (c // V7X_LANES), axis=1)
        pooled = win * inv - hc.astype(F32)
        y = _dot(pooled.astype(BF16), w_ref[...].astype(BF16)) * ps_ref[...]
        gate = m_ref[...][k_gate:k_gate + 1]
        o_ref[...] = x_ref[...] + gate * y

    last_halo = m // POOL_HALO - 1
    in_specs = [
        pl.BlockSpec((POOL_HALO, c), lambda g, i: (jnp.maximum(i * halo_per_tile - 1, 0), g)),
        pl.BlockSpec((r, c), lambda g, i: (i, g)),
        pl.BlockSpec((POOL_HALO, c), lambda g, i: (jnp.minimum((i + 1) * halo_per_tile, last_halo), g)),
        pl.BlockSpec((None, None, r, kc), lambda g, i: (g, i % period, 0, 0)),
        pl.BlockSpec((None, r, V7X_LANES), lambda g, i: (g, i % period, 0)),
        pl.BlockSpec((None, None, c, c), lambda g, i: (layer, g, 0, 0)),
        pl.BlockSpec((1, c), lambda g, i: (0, g)),
        pl.BlockSpec((r, c), lambda g, i: (i, g)),
        pl.BlockSpec((None, N_MOD, c), lambda g, i: ((i * r) // rows_per_cond, 0, g)),
    ]
    return _call(body, name="pool_mixer", grid=(n_groups, nt), in_specs=in_specs,
                 out_specs=pl.BlockSpec((r, c), lambda g, i: (i, g)),
                 out_shape=jax.ShapeDtypeStruct((m, d), F32),
                 args=[h, h, h, member, inv_len, w_pool, pool_scale.reshape(1, d), x, mods],
                 temp_bytes=4 * _nbytes((r, c), F32))


def _pool_tables(seq, r, period, kc):
    p = jnp.arange(period)[:, None, None]
    row = p * r + jnp.arange(r)[None, :, None]
    col = p * r - POOL_HALO + jnp.arange(kc)[None, None, :]
    seq_start = (row // seq) * seq
    members, invs = [], []
    for w in POOL_WINDOWS:
        lo = jnp.maximum(row - w // 2, seq_start)
        hi = jnp.minimum(row + w - w // 2, seq_start + seq)
        members.append(((col >= lo) & (col < hi)).astype(BF16))
        invs.append(jnp.broadcast_to(1.0 / (hi - lo).astype(F32), (period, r, V7X_LANES)))
    inv = jnp.stack(invs).reshape(len(POOL_WINDOWS), period * r, V7X_LANES)
    return jnp.stack(members), inv


def _ffn_up(h, w_up, conv_w, conv_b, layer, grp, w_down=None):
    m, d = h.shape
    f = w_up.shape[2] // 2
    tm = _tile(m, max(grp.seq, 2048), grp.seq)
    tn = _tile(f, 256, V7X_LANES)
    nj = f // tn
    seq = grp.seq
    steps = (m // tm) * nj
    fuse_cast = w_down is not None and w_down.shape[1] % (steps * V7X_SUBLANES_BF16) == 0
    slab_rows = w_down.shape[1] // steps if fuse_cast else 0

    rc = _tile(seq, 512)
    pad = 8
    rs = _tile(tm, 512, V7X_SUBLANES_BF16)

    def body(*refs):
        h_ref, wa_ref, wb_ref, cwa_ref, cwb_ref, cba_ref, cbb_ref = refs[:7]
        if fuse_cast:
            wd_ref, o_ref, wdo_ref, acc_ref = refs[7:]
            wdo_ref[...] = wd_ref[...].astype(BF16)
        else:
            o_ref, acc_ref = refs[7:]
        row8 = lax.broadcasted_iota(jnp.int32, (8, tn), 0)

        def matmul(s):
            xs = h_ref[s * rs:(s + 1) * rs, :]
            for k, w_ref in enumerate((wa_ref, wb_ref)):
                acc_ref[k, pad + s * rs:pad + (s + 1) * rs] = _dot(xs, w_ref[...].astype(BF16))

        def piece(p0, p1):
            n = p1 - p0

            def conv(k, cw_ref, cb_ref):
                cw = cw_ref[...]
                u = acc_ref[k, pad + p0:pad + p1]
                prev = acc_ref[k, pad + p0 - 1:pad + p1 - 1]
                nxt = acc_ref[k, pad + p0 + 1:pad + p1 + 1]
                top = jnp.where(row8 == 0, 0.0, prev[:8]) if p0 % seq == 0 else prev[:8]
                prev = top if n == 8 else jnp.concatenate([top, prev[8:]], axis=0)
                bot = jnp.where(row8 == 7, 0.0, nxt[n - 8:]) if p1 % seq == 0 else nxt[n - 8:]
                nxt = bot if n == 8 else jnp.concatenate([nxt[:n - 8], bot], axis=0)
                return prev * cw[0:1] + u * cw[1:2] + nxt * cw[2:3] + cb_ref[...]

            a = conv(0, cwa_ref, cba_ref)
            b = conv(1, cwb_ref, cbb_ref)
            o_ref[p0:p1] = (_silu(a) * b).astype(o_ref.dtype)

        def epilogue(r0, r1):
            cuts = sorted({r0, r1} | {c for c in range(0, tm + 1, rc) if r0 < c < r1})
            for p0, p1 in zip(cuts[:-1], cuts[1:]):
                piece(p0, p1)

        n_stage = tm // rs
        for k in range(2):
            acc_ref[k, 0:pad] = jnp.zeros((pad, tn), F32)
            acc_ref[k, pad + tm:2 * pad + tm] = jnp.zeros((pad, tn), F32)
        for s in range(n_stage + 1):
            if s > 0:
                epilogue(max((s - 1) * rs - pad, 0), tm if s == n_stage else s * rs - pad)
            if s < n_stage:
                matmul(s)

    col = lambda i, j: jnp.where(i % 2 == 0, j, nj - 1 - j)
    in_specs = [
        pl.BlockSpec((tm, d), lambda i, j: (i, 0)),
        pl.BlockSpec((None, d, tn), lambda i, j: (layer, 0, col(i, j))),
        pl.BlockSpec((None, d, tn), lambda i, j: (layer, 0, nj + col(i, j))),
        pl.BlockSpec((None, CONV_W, tn), lambda i, j: (layer, 0, col(i, j))),
        pl.BlockSpec((None, CONV_W, tn), lambda i, j: (layer, 0, nj + col(i, j))),
        pl.BlockSpec((None, 1, tn), lambda i, j: (layer, 0, col(i, j))),
        pl.BlockSpec((None, 1, tn), lambda i, j: (layer, 0, nj + col(i, j))),
    ]
    cb = conv_b.reshape(conv_b.shape[0], 1, 2 * f)
    args = [h, w_up, w_up, conv_w, conv_w, cb, cb]
    out_specs = [pl.BlockSpec((tm, tn), lambda i, j: (i, col(i, j)))]
    out_shape = [jax.ShapeDtypeStruct((m, f), BF16)]
    if fuse_cast:
        slab = (slab_rows, w_down.shape[2])
        in_specs.append(pl.BlockSpec((None,) + slab, lambda i, j: (layer, i * nj + j, 0)))
        args.append(w_down)
        out_specs.append(pl.BlockSpec(slab, lambda i, j: (i * nj + j, 0)))
        out_shape.append(jax.ShapeDtypeStruct(w_down.shape[1:], BF16))
    res = _call(body, name="ffn_up", grid=(m // tm, nj), in_specs=in_specs, out_specs=out_specs,
                out_shape=out_shape, args=args,
                temp_bytes=10 * _nbytes((rc, tn), F32) + 4 * _nbytes((rs, tn), F32) + 2 * _nbytes((d, tn), BF16),
                scratch=[((2, tm + 2 * pad, tn), F32)])
    return res[0], (res[1] if fuse_cast else None)


def _residual_proj(name, lhs, rhs, x, mods, k_gate, *, tm, tn):
    m = x.shape[0]
    rows_per_cond = m // mods.shape[0]
    assert rows_per_cond % tm == 0
    extras = [(x, (tm, tn), lambda i, j: (i, j)),
              (mods, (None, N_MOD, tn), lambda i, j: ((i * tm) // rows_per_cond, 0, j))]
    return _proj(name, lhs, rhs, _ep_residual(k_gate), (F32,), tm=tm, tn=tn, extras=extras)[0]


class _AttnWeights(NamedTuple):
    qa: _W
    ka: _W
    va: _W
    lat: jax.Array
    g_cq: jax.Array
    g_ckv: jax.Array
    q_up: _W
    kv: _W
    o_a: _W
    o_b: _W
    lam_vecs: jax.Array


def _prep_attn_weights(j, w_in_b, w_in, g_cq, w_q_up, g_ckv, w_kv_up, w_o, lam_vecs):
    d = w_in.shape[0]
    a_q = A_HEADS * A_HEAD_DIM
    a_v = A_HEADS * A_V_DIM
    b_out = B_HEADS * B_V_DIM
    assert a_v == b_out
    q_lora = g_cq.shape[0]
    kv_lora = g_ckv.shape[0]
    o0, o1, o2 = a_q, 2 * a_q, 2 * a_q + a_v
    lat = jnp.pad(w_in[:, o2:], ((0, 0), (0, MLA_KR_PAD - ROPE_DIM))).astype(BF16)
    q_up = w_q_up.reshape(q_lora, B_HEADS, NOPE_DIM + ROPE_DIM)
    q_up = jnp.pad(q_up, ((0, 0), (0, 0), (0, MLA_Q_PAD - NOPE_DIM - ROPE_DIM)))
    kv = w_kv_up.reshape(kv_lora, B_HEADS, NOPE_DIM + B_V_DIM)
    kv = jnp.concatenate([kv[:, :, :NOPE_DIM].reshape(kv_lora, B_HEADS * NOPE_DIM),
                          kv[:, :, NOPE_DIM:].reshape(kv_lora, B_HEADS * B_V_DIM)], axis=1)
    c = lambda w: _whole(w.astype(BF16))
    win = lambda col0, n: _W(w_in_b, j, d, 0, col0, n)
    return _AttnWeights(
        qa=win(0, a_q), ka=win(o0, a_q), va=win(o1, a_v), lat=lat,
        g_cq=g_cq.reshape(1, q_lora), g_ckv=g_ckv.reshape(1, kv_lora),
        q_up=c(q_up.reshape(q_lora, B_HEADS * MLA_Q_PAD)), kv=c(kv),
        o_a=_W(w_o, j, a_v, 0, 0, w_o.shape[-1]), o_b=_W(w_o, j, b_out, 1, 0, w_o.shape[-1]),
        lam_vecs=lam_vecs)


def _latent_proj(h, w_lat, g_cq, g_ckv, *, tm, rope=None):
    m, d = h.shape
    q_lora, kv_lora = g_cq.shape[1], g_ckv.shape[1]
    n = w_lat.shape[1]
    assert n == q_lora + kv_lora + MLA_KR_PAD and q_lora % V7X_LANES == 0 and kv_lora % V7X_LANES == 0

    def rms(a, g):
        return a * lax.rsqrt(jnp.mean(a * a, axis=-1, keepdims=True) + EPS) * g

    def body(*refs):
        h_ref, w_ref, gq_ref, gkv_ref = refs[:4]
        cq_ref, ckv_ref, kr_ref = refs[-3:]
        acc = _dot(h_ref[...], w_ref[...])
        cq_ref[...] = rms(acc[:, :q_lora], gq_ref[...]).astype(cq_ref.dtype)
        ckv_ref[...] = rms(acc[:, q_lora:q_lora + kv_lora], gkv_ref[...])
        kr = acc[:, q_lora + kv_lora:]
        if rope is not None:
            kr = _rotate(kr, refs[4][...], refs[5][...])
        kr_ref[...] = kr.astype(kr_ref.dtype)

    row = lambda width: pl.BlockSpec((tm, width), lambda i: (i, 0))
    in_specs = [row(d), pl.BlockSpec((d, n), lambda i: (0, 0), pipeline_mode=pl.Buffered(1)),
                pl.BlockSpec((1, q_lora), lambda i: (0, 0)), pl.BlockSpec((1, kv_lora), lambda i: (0, 0))]
    args = [h, w_lat, g_cq, g_ckv]
    if rope is not None:
        cos, sin, tiles_per_seq = rope
        in_specs += [pl.BlockSpec((tm, MLA_KR_PAD), lambda i: (i % tiles_per_seq, 0))] * 2
        args += [cos, sin]
    return _call(body, name="latent_proj", grid=(m // tm,), in_specs=in_specs,
                 out_specs=[row(q_lora), row(kv_lora), row(MLA_KR_PAD)],
                 out_shape=[jax.ShapeDtypeStruct((m, q_lora), BF16), jax.ShapeDtypeStruct((m, kv_lora), F32),
                            jax.ShapeDtypeStruct((m, MLA_KR_PAD), F32 if rope is None else BF16)],
                 args=args, temp_bytes=4 * _nbytes((tm, n), F32), single_buffered=(1,))


def _attn_mixer(h, x, mods, w, grp, lam_init, cache):
    m, d = h.shape
    hps = A_HEADS if grp.seq <= 256 else 8
    qk_scale = A_QK_HALF ** -0.5 * LOG2_E
    mla_scale = (NOPE_DIM + ROPE_DIM) ** -0.5 * LOG2_E
    rows_unit = grp.seq if grp.rope else m

    def tiles(lhs, rhs, heavy_epilogue):
        k, n = rhs.k, rhs.n
        pref = 1024 if k < 1024 else (256 if heavy_epilogue else 512)
        return _tile(min(rows_unit, lhs.shape[0]), pref, V7X_SUBLANES_BF16), min(n, 2048)

    def plain(name, lhs, rhs, dtype, scale=1.0):
        tm_, tn_ = tiles(lhs, rhs, False)
        return _proj(name, [lhs], [rhs], _ep_scale(scale), (dtype,), tm=tm_, tn=tn_)[0]

    def roped(name, lhs, rhs, tables, dtype, scale, with_plain):
        tm_, tn_ = tiles(lhs, rhs, True)
        tiles_per_seq = grp.seq // tm_
        extras = [(t, (tm_, t.shape[1]), lambda i, j: (i % tiles_per_seq, 0)) for t in tables]
        outs = (dtype, dtype) if with_plain else (dtype,)
        return _proj(name, [lhs], [rhs], _ep_rope(scale, with_plain), outs, tm=tm_, tn=tn_, extras=extras)

    lat_tm = _tile(rows_unit, 256 if grp.rope else 512, V7X_SUBLANES_BF16)
    if not grp.rope:
        cqn, ckvn, kr = _latent_proj(h, w.lat, w.g_cq, w.g_ckv, tm=lat_tm)
        kv = plain("kv_up", ckvn, w.kv, BF16)
        qa = plain("q_diff", h, w.qa, BF16, qk_scale)
        ka = plain("k_diff", h, w.ka, F32)
        va = plain("v_diff", h, w.va, F32)
        q = plain("q_mla", cqn, w.q_up, BF16, mla_scale)
        oa = _diff_attention(w.lam_vecs, lam_init, [qa], [ka], [va], grp, hps)
        ob = _mla_attention([q], [kv], [kr], grp, hps)
        state = (ka, va, ckvn, kr[:, :ROPE_DIM])
    else:
        cos, sin = _rope_tables(grp.seq)
        t_diff = _widen_tables(cos, sin, 0, 0, A_HEAD_DIM)
        t_mla = _widen_tables(cos, sin, NOPE_DIM, MLA_Q_PAD - NOPE_DIM - ROPE_DIM, MLA_Q_PAD)
        t_kr = _widen_tables(cos, sin, 0, MLA_KR_PAD - ROPE_DIM, MLA_KR_PAD)
        ka_c, va_c, ckv_c, kr_c = cache
        cqn, ckvn, kr_r = _latent_proj(h, w.lat, w.g_cq, w.g_ckv, tm=lat_tm,
                                       rope=(t_kr[0], t_kr[1], grp.seq // lat_tm))
        kv = plain("kv_up", ckvn, w.kv, BF16)
        qa_u, qa_r = roped("q_diff", h, w.qa, t_diff, BF16, qk_scale, True)
        ka_r, = roped("k_diff", h, w.ka, t_diff, BF16, 1.0, False)
        va = plain("v_diff", h, w.va, BF16)
        q_u, q_r = roped("q_mla", cqn, w.q_up, t_mla, BF16, mla_scale, True)
        kv_c = plain("kv_up_ctx", ckv_c, w.kv, BF16)
        oa = _diff_attention(w.lam_vecs, lam_init, [qa_u, qa_r], [ka_c, ka_r], [va_c, va], grp, hps)
        ob = _mla_attention([q_u, q_r], [kv_c, kv], [kr_c, kr_r], grp, hps)
        state = None
    x1 = _residual_proj("o_proj", [oa, ob], [w.o_a, w.o_b], x, mods, 2,
                        tm=_tile(rows_unit, 1024, V7X_SUBLANES_BF16), tn=512)
    return x1, state


def _conv_ffn(x, mods, g2, w_up, conv_w, conv_b, w_down, w_down_b, layer, grp):
    h = _norm(x, g2, mods, 3, 4, BF16)
    act, made = _ffn_up(h, w_up, conv_w, conv_b, layer, grp, None if w_down_b is not None else w_down)
    if w_down_b is None:
        w_down_b = made if made is not None else w_down[layer].astype(BF16)
    out = _residual_proj("ffn_down", [act], [_whole(w_down_b)], x, mods, 5,
                         tm=_tile(x.shape[0] // mods.shape[0], 512, V7X_SUBLANES_BF16), tn=512)
    return out, w_down_b


def kernel(x_prompt, x_sample, cache_diff_k, cache_diff_v, cache_mla_ckv, cache_mla_krope, c, c_ctx,
           norm1_g, norm2_g, w_mod, b_mod, w_in, g_cq, w_q_up, g_ckv, w_kv_up,
           lambda_q1, lambda_k1, lambda_q2, lambda_k2, w_o, w_pool, pool_scale,
           w_up, conv_w, conv_b, w_down, g_final):
    batch, seq, d = x_prompt.shape
    dec_batch, dec_seq, _ = x_sample.shape
    depth = w_mod.shape[0]
    past = cache_diff_k.shape[2]
    groups = (_Group(batch, seq, False), _Group(dec_batch, dec_seq, True))

    n_cond = 1 + dec_batch
    cond8 = jnp.zeros((8, d), F32).at[0].set(c_ctx).at[1:n_cond].set(c)
    mods_all = _adaln(cond8, w_mod, b_mod).reshape(depth, 8, N_MOD, d)
    mods_g = (mods_all[:, 0:1], mods_all[:, 1:n_cond])

    w_in_b = w_in.astype(BF16)
    attn_w = {}
    for i in range(0, depth, 2):
        j = i // 2
        lam_vecs = jnp.stack([lambda_q1[j], lambda_k1[j], lambda_q2[j], lambda_k2[j]])
        attn_w[i] = _prep_attn_weights(j, w_in_b, w_in[j], g_cq[j], w_q_up[j], g_ckv[j], w_kv_up[j],
                                       w_o, lam_vecs)
    w_down_b = [None] * depth

    xs = [x_prompt.reshape(batch * seq, d), x_sample.reshape(dec_batch * dec_seq, d)]
    states = []
    for gi, grp in enumerate(groups):
        x = xs[gi]
        for i in range(depth):
            mods = mods_g[gi][i]
            h = _norm(x, norm1_g[i], mods, 0, 1, BF16)
            if i % 2 == 0:
                j = i // 2
                lam_init = 0.8 - 0.6 * math.exp(-0.3 * i)
                cache = None
                if grp.rope:
                    kr_c = jnp.pad(cache_mla_krope[:, j].reshape(dec_batch * past, ROPE_DIM),
                                   ((0, 0), (0, MLA_KR_PAD - ROPE_DIM)))
                    cache = (cache_diff_k[:, j].reshape(dec_batch * past, A_HEADS * A_HEAD_DIM),
                             cache_diff_v[:, j].reshape(dec_batch * past, A_HEADS * A_V_DIM),
                             cache_mla_ckv[:, j].reshape(dec_batch * past, -1), kr_c)
                x, st = _attn_mixer(h, x, mods, attn_w[i], grp, lam_init, cache)
                if st is not None:
                    states.append(st)
            else:
                x = _pool_mixer(h, x, mods, w_pool, i // 2, pool_scale[i // 2], grp, 2)
            x, w_down_b[i] = _conv_ffn(x, mods, norm2_g[i], w_up, conv_w, conv_b, w_down, w_down_b[i], i, grp)
        xs[gi] = _norm(x, g_final, None, 0, 0, F32)

    y_prompt = xs[0].reshape(batch, seq, d)
    y_sample = xs[1].reshape(dec_batch, dec_seq, d)
    stack = lambda k, tail: jnp.stack([s[k].reshape((batch, seq) + tail) for s in states], axis=1)
    new_diff_k = stack(0, (A_HEADS, A_HEAD_DIM))
    new_diff_v = stack(1, (A_HEADS, A_V_DIM))
    new_mla_ckv = stack(2, (g_ckv.shape[-1],))
    new_mla_krope = stack(3, (ROPE_DIM,))
    return (y_prompt, y_sample, new_diff_k, new_diff_v, new_mla_ckv, new_mla_krope)
```

```python
import functools
import math
from typing import NamedTuple

import jax
import jax.numpy as jnp
from jax import lax
from jax.experimental import pallas as pl
from jax.experimental.pallas import tpu as pltpu

F32 = jnp.float32
BF16 = jnp.bfloat16

GRID_W = 64
ROPE_BASE = 10000.0
EPS = 1e-6
A_HEADS = 16
A_QK_HALF = 64
A_HEAD_DIM = 2 * A_QK_HALF
A_V_DIM = 128
B_HEADS = 16
NOPE_DIM = 128
ROPE_DIM = 64
B_V_DIM = 128
POOL_WINDOWS = (2, 4, 8, 16)
N_MOD = 6
CONV_W = 3
LOG2_E = math.log2(math.e)

V7X_VMEM_BYTES = 64 * 1024 * 1024
V7X_LANES = 128
V7X_SUBLANES_BF16 = 16
COMPILER_SCRATCH_BYTES = 2 * 1024 * 1024

MLA_Q_PAD = 256
MLA_KR_PAD = MLA_Q_PAD - NOPE_DIM
ROPE_QUARTER = ROPE_DIM // 4
POOL_HALO = V7X_SUBLANES_BF16
ROWSUM_IN_MATMUL_MIN_KEYS = 1024


class _Group(NamedTuple):
    n_seq: int
    seq: int
    rope: bool


def _tile(n, pref, mult=8):
    if n <= pref:
        return n
    t = (pref // mult) * mult
    while t >= mult:
        if n % t == 0:
            return t
        t -= mult
    return n


def _nbytes(shape, dtype):
    return math.prod(shape) * jnp.dtype(dtype).itemsize


def _call(body, *, name, grid, in_specs, out_specs, out_shape, args, temp_bytes=0, single_buffered=(),
          scratch=(), semantics=None, flags=None):
    multi = isinstance(out_shape, (list, tuple))
    outs = list(out_shape) if multi else [out_shape]
    ospecs = list(out_specs) if multi else [out_specs]
    total = temp_bytes + COMPILER_SCRATCH_BYTES + sum(_nbytes(s, dt) for s, dt in scratch)
    for k, (a, s) in enumerate(zip(args, in_specs)):
        blk = [1 if b is None else b for b in s.block_shape]
        total += _nbytes(blk, a.dtype) * (1 if k in single_buffered else 2)
    for o, s in zip(outs, ospecs):
        blk = [1 if b is None else b for b in s.block_shape]
        total += 2 * _nbytes(blk, o.dtype)
    limit = min(V7X_VMEM_BYTES - 4 * 1024 * 1024, max(total, 16 * 1024 * 1024))
    return pl.pallas_call(
        body, name=name, grid=grid, in_specs=in_specs, out_specs=out_specs, out_shape=out_shape,
        scratch_shapes=[pltpu.VMEM(s, dt) for s, dt in scratch],
        compiler_params=pltpu.CompilerParams(
            dimension_semantics=semantics or ("parallel",) * len(grid), vmem_limit_bytes=int(limit),
            flags=flags),
    )(*args)


def _dot(a, b):
    return jnp.dot(a, b, preferred_element_type=F32)


def _dot_nt(a, b):
    return lax.dot_general(a, b, (((1,), (1,)), ((), ())), preferred_element_type=F32)


def _silu(x):
    return x / (1.0 + jnp.exp(-x))


def _adaln(cond8, w_mod, b_mod):
    depth, d, n = w_mod.shape
    tn = next(t for t in (512, 256, 128) if n % (2 * t) == 0)

    def body(c_ref, wa_ref, wb_ref, b_ref, o_ref):
        s = _silu(c_ref[...]).astype(BF16)
        ya = _dot(s, wa_ref[...].astype(BF16))
        yb = _dot(s, wb_ref[...].astype(BF16))
        o_ref[...] = jnp.concatenate([ya, yb], axis=1) + b_ref[...]

    return _call(
        body, name="adaln", grid=(depth, n // (2 * tn)),
        in_specs=[pl.BlockSpec((8, d), lambda l, j: (0, 0)),
                  pl.BlockSpec((None, d, tn), lambda l, j: (l, 0, 2 * j)),
                  pl.BlockSpec((None, d, tn), lambda l, j: (l, 0, 2 * j + 1)),
                  pl.BlockSpec((None, 1, 2 * tn), lambda l, j: (l, 0, j))],
        out_specs=pl.BlockSpec((None, 8, 2 * tn), lambda l, j: (l, 0, j)),
        out_shape=jax.ShapeDtypeStruct((depth, 8, n), F32),
        args=(cond8, w_mod, w_mod, b_mod.reshape(depth, 1, n)),
        temp_bytes=2 * _nbytes((d, tn), BF16) + 2 * _nbytes((d, tn), F32))


def _norm(x, g, mods, k_shift, k_scale, out_dtype):
    m, d = x.shape
    modulated = mods is not None
    tm = _tile(m // mods.shape[0] if modulated else m, 512)

    def body(*refs):
        x_ref, g_ref = refs[0], refs[1]
        o_ref = refs[-1]
        xf = x_ref[...]
        y = xf * lax.rsqrt(jnp.mean(xf * xf, axis=-1, keepdims=True) + EPS) * g_ref[...]
        if modulated:
            mm = refs[2][...]
            y = y * (1.0 + mm[k_scale:k_scale + 1]) + mm[k_shift:k_shift + 1]
        o_ref[...] = y.astype(o_ref.dtype)

    in_specs = [pl.BlockSpec((tm, d), lambda i: (i, 0)), pl.BlockSpec((1, d), lambda i: (0, 0))]
    args = [x, g.reshape(1, d)]
    if modulated:
        rows_per_cond = m // mods.shape[0]
        in_specs.append(pl.BlockSpec((None, N_MOD, d), lambda i: ((i * tm) // rows_per_cond, 0, 0)))
        args.append(mods)
    return _call(body, name="norm", grid=(m // tm,), in_specs=in_specs,
                 out_specs=pl.BlockSpec((tm, d), lambda i: (i, 0)),
                 out_shape=jax.ShapeDtypeStruct((m, d), out_dtype), args=args,
                 temp_bytes=3 * _nbytes((tm, d), F32))


class _W(NamedTuple):
    arr: jax.Array
    layer: object
    k: int
    row_blk: int
    col0: int
    n: int

    def spec(self, tn, col=lambda i, j: j, **mode):
        assert self.col0 % tn == 0 and self.n % tn == 0
        c0 = self.col0 // tn
        if self.layer is None:
            return pl.BlockSpec((self.k, tn), lambda i, j: (self.row_blk, c0 + col(i, j)), **mode)
        return pl.BlockSpec((None, self.k, tn),
                            lambda i, j: (self.layer, self.row_blk, c0 + col(i, j)), **mode)


def _whole(arr):
    return _W(arr, None, arr.shape[0], 0, 0, arr.shape[1])


def _proj(name, lhs, rhs, epilogue, outs, *, tm, tn, extras=()):
    m = lhs[0].shape[0]
    n = rhs[0].n
    np_ = len(lhs)
    ne = len(extras)

    def body(*refs):
        acc = None
        for p in range(np_):
            part = _dot(refs[p][...].astype(BF16), refs[np_ + p][...].astype(BF16))
            acc = part if acc is None else acc + part
        epilogue(acc, refs[2 * np_:2 * np_ + ne], refs[2 * np_ + ne:])

    resident = n == tn
    rhs_mode = dict(pipeline_mode=pl.Buffered(1)) if resident else {}
    nj = n // tn
    col = (lambda i, j: j) if nj == 1 else (lambda i, j: jnp.where(i % 2 == 0, j, nj - 1 - j))
    in_specs = ([pl.BlockSpec((tm, a.shape[1]), lambda i, j: (i, 0)) for a in lhs]
                + [w.spec(tn, col, **rhs_mode) for w in rhs]
                + [pl.BlockSpec(bs, lambda i, j, im=im: im(i, col(i, j))) for (_, bs, im) in extras])
    args = list(lhs) + [w.arr for w in rhs] + [e[0] for e in extras]
    out_shape = [jax.ShapeDtypeStruct((m, n), dt) for dt in outs]
    out_specs = [pl.BlockSpec((tm, tn), lambda i, j: (i, col(i, j))) for _ in outs]
    return _call(body, name=name, grid=(m // tm, nj), in_specs=in_specs, out_specs=out_specs,
                 out_shape=out_shape, args=args, temp_bytes=6 * _nbytes((tm, tn), F32),
                 single_buffered=tuple(range(np_, 2 * np_)) if resident else ())


def _ep_scale(scale):
    def ep(acc, ex, outs):
        outs[0][...] = (acc * scale).astype(outs[0].dtype)
    return ep


def _rotate(a, cos, sin):
    width = a.shape[1]
    lane = lax.broadcasted_iota(jnp.int32, a.shape, 1)
    low = (lane & (2 * ROPE_QUARTER - 1)) < ROPE_QUARTER
    partner = jnp.where(low, pltpu.roll(a, width - ROPE_QUARTER, 1), pltpu.roll(a, ROPE_QUARTER, 1))
    return a * cos + partner * sin


def _ep_rope(scale, with_plain):
    def ep(acc, ex, outs):
        a = acc * scale
        reps = a.shape[1] // ex[0].shape[1]
        cos = jnp.concatenate([ex[0][...]] * reps, axis=1)
        sin = jnp.concatenate([ex[1][...]] * reps, axis=1)
        r = _rotate(a, cos, sin)
        if with_plain:
            outs[0][...] = a.astype(outs[0].dtype)
        outs[-1][...] = r.astype(outs[-1].dtype)
    return ep


def _ep_residual(k_gate):
    def ep(acc, ex, outs):
        gate = ex[1][...][k_gate:k_gate + 1]
        outs[0][...] = ex[0][...] + gate * acc
    return ep


def _rope_tables(seq):
    half = ROPE_DIM // 2
    inv = ROPE_BASE ** (-jnp.arange(0, half, 2, dtype=F32) / half)
    pos = jnp.arange(seq)
    ar = (pos // GRID_W).astype(F32)[:, None] * inv[None, :]
    ac = (pos % GRID_W).astype(F32)[:, None] * inv[None, :]
    cos = jnp.concatenate([jnp.cos(ar), jnp.cos(ar), jnp.cos(ac), jnp.cos(ac)], axis=1)
    sin = jnp.concatenate([-jnp.sin(ar), jnp.sin(ar), -jnp.sin(ac), jnp.sin(ac)], axis=1)
    return cos, sin


def _widen_tables(cos, sin, lead, trail, width):
    seq = cos.shape[0]
    c = jnp.concatenate([jnp.ones((seq, lead), F32), cos, jnp.ones((seq, trail), F32)], axis=1)
    s = jnp.concatenate([jnp.zeros((seq, lead), F32), sin, jnp.zeros((seq, trail), F32)], axis=1)
    reps = width // c.shape[1]
    assert reps * c.shape[1] == width
    return jnp.tile(c, (1, reps)), jnp.tile(s, (1, reps))


def _pv_and_rowsum(ps, vs):
    if sum(v.shape[0] for v in vs) < ROWSUM_IN_MATMUL_MIN_KEYS:
        den = functools.reduce(jnp.add, [jnp.sum(p, axis=-1, keepdims=True) for p in ps])
        return functools.reduce(jnp.add, [_dot(p.astype(v.dtype), v) for p, v in zip(ps, vs)]), den
    acc = None
    for p, v in zip(ps, vs):
        ones = (lax.broadcasted_iota(jnp.int32, (v.shape[0], V7X_LANES), 1) == 0).astype(v.dtype)
        part = _dot(p.astype(v.dtype), jnp.concatenate([v, ones], axis=1))
        acc = part if acc is None else acc + part
    width = vs[0].shape[1]
    return acc[:, :width], acc[:, width:width + 1]


def _diff_attention(lam_vecs, lam_init, qs, ks, vs, grp, hps):
    m = qs[0].shape[0]
    nseg = len(qs)
    tq = _tile(grp.seq, 256)
    qt = grp.seq // tq
    width = hps * A_HEAD_DIM
    n_hg = A_HEADS // hps
    k_lens = [k.shape[0] // grp.n_seq for k in ks]

    def body(*refs):
        lam_ref = refs[0]
        q_refs = refs[1:1 + nseg]
        k_refs = refs[1 + nseg:1 + 2 * nseg]
        v_refs = refs[1 + 2 * nseg:1 + 3 * nseg]
        o_ref = refs[-1]
        lv = lam_ref[...]
        lam = (jnp.exp(jnp.sum(lv[0:1] * lv[1:2], axis=-1, keepdims=True))
               - jnp.exp(jnp.sum(lv[2:3] * lv[3:4], axis=-1, keepdims=True)) + lam_init)
        first = lax.broadcasted_iota(jnp.int32, (tq, A_HEAD_DIM), 1) < A_QK_HALF
        for g in range(hps):
            cols = slice(g * A_HEAD_DIM, (g + 1) * A_HEAD_DIM)
            scores = []
            for q_ref, k_ref in zip(q_refs, k_refs):
                q = q_ref[:, cols]
                zero = jnp.zeros_like(q)
                q2 = jnp.concatenate([jnp.where(first, q, zero), jnp.where(first, zero, q)], axis=0)
                scores.append(_dot_nt(q2, k_ref[:, cols].astype(BF16)))
            mx = functools.reduce(jnp.maximum, [jnp.max(s, axis=-1, keepdims=True) for s in scores])
            ps = [jnp.exp2(s - mx) for s in scores]
            o2, den = _pv_and_rowsum(ps, [v_ref[:, cols].astype(BF16) for v_ref in v_refs])
            o2 = o2 / den
            o = o2[:tq] - lam * o2[tq:]
            o = o * lax.rsqrt(jnp.mean(o * o, axis=-1, keepdims=True) + EPS) * (1.0 - lam_init)
            o_ref[:, cols] = o.astype(o_ref.dtype)

    q_spec = pl.BlockSpec((tq, width), lambda b, h, i: (b * qt + i, h))
    kv_mode = dict(pipeline_mode=pl.Buffered(1)) if qt > 1 else {}
    in_specs = ([pl.BlockSpec(lam_vecs.shape, lambda b, h, i: (0, 0))]
                + [q_spec] * nseg
                + [pl.BlockSpec((kl, width), lambda b, h, i: (b, h), **kv_mode) for kl in k_lens] * 2)
    nk = sum(k_lens)
    return _call(body, name="diff_attention", grid=(grp.n_seq, n_hg, qt), in_specs=in_specs,
                 out_specs=q_spec, out_shape=jax.ShapeDtypeStruct((m, A_HEADS * A_V_DIM), BF16),
                 args=[lam_vecs] + list(qs) + list(ks) + list(vs),
                 temp_bytes=(hps + 1) * _nbytes((2 * tq, nk), F32),
                 single_buffered=tuple(range(1 + nseg, 1 + 3 * nseg)) if kv_mode else ())


def _mla_attention(qs, kvs, krs, grp, hps):
    m = qs[0].shape[0]
    nseg = len(qs)
    tq = _tile(grp.seq, 256)
    qt = grp.seq // tq
    n_hg = B_HEADS // hps
    k_lens = [kv.shape[0] // grp.n_seq for kv in kvs]

    def body(*refs):
        q_refs = refs[:nseg]
        kn_refs = refs[nseg:2 * nseg]
        kr_refs = refs[2 * nseg:3 * nseg]
        v_refs = refs[3 * nseg:4 * nseg]
        o_ref = refs[-1]
        krs_v = [r[...].astype(BF16) for r in kr_refs]
        for g in range(hps):
            kcols = slice(g * NOPE_DIM, (g + 1) * NOPE_DIM)
            vcols = slice(g * B_V_DIM, (g + 1) * B_V_DIM)
            scores = []
            for q_ref, kn_ref, kr in zip(q_refs, kn_refs, krs_v):
                q = q_ref[:, g * MLA_Q_PAD:(g + 1) * MLA_Q_PAD]
                k = jnp.concatenate([kn_ref[:, kcols], kr], axis=1)
                scores.append(_dot_nt(q, k))
            mx = functools.reduce(jnp.maximum, [jnp.max(s, axis=-1, keepdims=True) for s in scores])
            ps = [jnp.exp2(s - mx) for s in scores]
            o, den = _pv_and_rowsum(ps, [v_ref[:, vcols] for v_ref in v_refs])
            o_ref[:, vcols] = (o / den).astype(o_ref.dtype)

    q_spec = pl.BlockSpec((tq, hps * MLA_Q_PAD), lambda b, h, i: (b * qt + i, h))
    assert NOPE_DIM == B_V_DIM
    kv_mode = dict(pipeline_mode=pl.Buffered(1)) if qt > 1 else {}
    k_specs = [pl.BlockSpec((kl, hps * NOPE_DIM), lambda b, h, i: (b, h), **kv_mode) for kl in k_lens]
    v_specs = [pl.BlockSpec((kl, hps * B_V_DIM), lambda b, h, i: (b, n_hg + h), **kv_mode) for kl in k_lens]
    kr_specs = [pl.BlockSpec((kl, MLA_KR_PAD), lambda b, h, i: (b, 0)) for kl in k_lens]
    nk = sum(k_lens)
    single = tuple(range(nseg, 2 * nseg)) + tuple(range(3 * nseg, 4 * nseg)) if kv_mode else ()
    return _call(body, name="mla_attention", grid=(grp.n_seq, n_hg, qt),
                 in_specs=[q_spec] * nseg + k_specs + kr_specs + v_specs,
                 out_specs=pl.BlockSpec((tq, hps * B_V_DIM), lambda b, h, i: (b * qt + i, h)),
                 out_shape=jax.ShapeDtypeStruct((m, B_HEADS * B_V_DIM), BF16),
                 args=list(qs) + list(kvs) + list(krs) + list(kvs),
                 temp_bytes=(hps + 1) * _nbytes((tq, nk), F32) + 2 * _nbytes((nk, MLA_Q_PAD), BF16),
                 single_buffered=single)


def _pool_mixer(h, x, mods, w_pool, layer, pool_scale, grp, k_gate):
    m, d = x.shape
    _, n_groups, c, _ = w_pool.shape
    assert max(POOL_WINDOWS) // 2 <= POOL_HALO and n_groups == len(POOL_WINDOWS)
    rows_per_cond = m // mods.shape[0]
    r = _tile(rows_per_cond, 512, POOL_HALO)
    nt = m // r
    halo_per_tile = r // POOL_HALO
    kc = r + 2 * POOL_HALO
    period = math.lcm(r, grp.seq) // r
    member, inv_len = _pool_tables(grp.seq, r, period, kc)

    def body(hp_ref, hc_ref, hn_ref, mem_ref, inv_ref, w_ref, ps_ref, x_ref, m_ref, o_ref):
        hc = hc_ref[...]
        hcat = jnp.concatenate([hp_ref[...], hc, hn_ref[...]], axis=0)
        win = _dot(mem_ref[...], hcat)
        inv = jnp.concatenate([inv_ref[...]] * (c // V7X_LANES), axis=1)
        pooled = win * inv - hc.astype(F32)
        y = _dot(pooled.astype(BF16), w_ref[...].astype(BF16)) * ps_ref[...]
        gate = m_ref[...][k_gate:k_gate + 1]
        o_ref[...] = x_ref[...] + gate * y

    last_halo = m // POOL_HALO - 1
    in_specs = [
        pl.BlockSpec((POOL_HALO, c), lambda g, i: (jnp.maximum(i * halo_per_tile - 1, 0), g)),
        pl.BlockSpec((r, c), lambda g, i: (i, g)),
        pl.BlockSpec((POOL_HALO, c), lambda g, i: (jnp.minimum((i + 1) * halo_per_tile, last_halo), g)),
        pl.BlockSpec((None, None, r, kc), lambda g, i: (g, i % period, 0, 0)),
        pl.BlockSpec((None, r, V7X_LANES), lambda g, i: (g, i % period, 0)),
        pl.BlockSpec((None, None, c, c), lambda g, i: (layer, g, 0, 0)),
        pl.BlockSpec((1, c), lambda g, i: (0, g)),
        pl.BlockSpec((r, c), lambda g, i: (i, g)),
        pl.BlockSpec((None, N_MOD, c), lambda g, i: ((i * r) // rows_per_cond, 0, g)),
    ]
    return _call(body, name="pool_mixer", grid=(n_groups, nt), in_specs=in_specs,
                 out_specs=pl.BlockSpec((r, c), lambda g, i: (i, g)),
                 out_shape=jax.ShapeDtypeStruct((m, d), F32),
                 args=[h, h, h, member, inv_len, w_pool, pool_scale.reshape(1, d), x, mods],
                 temp_bytes=4 * _nbytes((r, c), F32))


def _pool_tables(seq, r, period, kc):
    p = jnp.arange(period)[:, None, None]
    row = p * r + jnp.arange(r)[None, :, None]
    col = p * r - POOL_HALO + jnp.arange(kc)[None, None, :]
    seq_start = (row // seq) * seq
    members, invs = [], []
    for w in POOL_WINDOWS:
        lo = jnp.maximum(row - w // 2, seq_start)
        hi = jnp.minimum(row + w - w // 2, seq_start + seq)
        members.append(((col >= lo) & (col < hi)).astype(BF16))
        invs.append(jnp.broadcast_to(1.0 / (hi - lo).astype(F32), (period, r, V7X_LANES)))
    inv = jnp.stack(invs).reshape(len(POOL_WINDOWS), period * r, V7X_LANES)
    return jnp.stack(members), inv


def _ffn_up(h, w_up, conv_w, conv_b, layer, grp, w_down=None):
    m, d = h.shape
    f = w_up.shape[2] // 2
    tm = _tile(m, max(grp.seq, 2048), grp.seq)
    tn = _tile(f, 256, V7X_LANES)
    nj = f // tn
    seq = grp.seq
    steps = (m // tm) * nj
    fuse_cast = w_down is not None and w_down.shape[1] % (steps * V7X_SUBLANES_BF16) == 0
    slab_rows = w_down.shape[1] // steps if fuse_cast else 0

    rc = _tile(seq, 512)
    pad = 8
    rs = _tile(tm, 512, V7X_SUBLANES_BF16)

    def body(*refs):
        h_ref, wa_ref, wb_ref, cwa_ref, cwb_ref, cba_ref, cbb_ref = refs[:7]
        if fuse_cast:
            wd_ref, o_ref, wdo_ref, acc_ref = refs[7:]
            wdo_ref[...] = wd_ref[...].astype(BF16)
        else:
            o_ref, acc_ref = refs[7:]
        row8 = lax.broadcasted_iota(jnp.int32, (8, tn), 0)

        def matmul(s):
            xs = h_ref[s * rs:(s + 1) * rs, :]
            for k, w_ref in enumerate((wa_ref, wb_ref)):
                acc_ref[k, pad + s * rs:pad + (s + 1) * rs] = _dot(xs, w_ref[...].astype(BF16))

        def piece(p0, p1):
            n = p1 - p0

            def conv(k, cw_ref, cb_ref):
                cw = cw_ref[...]
                u = acc_ref[k, pad + p0:pad + p1]
                prev = acc_ref[k, pad + p0 - 1:pad + p1 - 1]
                nxt = acc_ref[k, pad + p0 + 1:pad + p1 + 1]
                top = jnp.where(row8 == 0, 0.0, prev[:8]) if p0 % seq == 0 else prev[:8]
                prev = top if n == 8 else jnp.concatenate([top, prev[8:]], axis=0)
                bot = jnp.where(row8 == 7, 0.0, nxt[n - 8:]) if p1 % seq == 0 else nxt[n - 8:]
                nxt = bot if n == 8 else jnp.concatenate([nxt[:n - 8], bot], axis=0)
                return prev * cw[0:1] + u * cw[1:2] + nxt * cw[2:3] + cb_ref[...]

            a = conv(0, cwa_ref, cba_ref)
            b = conv(1, cwb_ref, cbb_ref)
            o_ref[p0:p1] = (_silu(a) * b).astype(o_ref.dtype)

        def epilogue(r0, r1):
            cuts = sorted({r0, r1} | {c for c in range(0, tm + 1, rc) if r0 < c < r1})
            for p0, p1 in zip(cuts[:-1], cuts[1:]):
                piece(p0, p1)

        n_stage = tm // rs
        for k in range(2):
            acc_ref[k, 0:pad] = jnp.zeros((pad, tn), F32)
            acc_ref[k, pad + tm:2 * pad + tm] = jnp.zeros((pad, tn), F32)
        for s in range(n_stage + 1):
            if s > 0:
                epilogue(max((s - 1) * rs - pad, 0), tm if s == n_stage else s * rs - pad)
            if s < n_stage:
                matmul(s)

    col = lambda i, j: jnp.where(i % 2 == 0, j, nj - 1 - j)
    in_specs = [
        pl.BlockSpec((tm, d), lambda i, j: (i, 0)),
        pl.BlockSpec((None, d, tn), lambda i, j: (layer, 0, col(i, j))),
        pl.BlockSpec((None, d, tn), lambda i, j: (layer, 0, nj + col(i, j))),
        pl.BlockSpec((None, CONV_W, tn), lambda i, j: (layer, 0, col(i, j))),
        pl.BlockSpec((None, CONV_W, tn), lambda i, j: (layer, 0, nj + col(i, j))),
        pl.BlockSpec((None, 1, tn), lambda i, j: (layer, 0, col(i, j))),
        pl.BlockSpec((None, 1, tn), lambda i, j: (layer, 0, nj + col(i, j))),
    ]
    cb = conv_b.reshape(conv_b.shape[0], 1, 2 * f)
    args = [h, w_up, w_up, conv_w, conv_w, cb, cb]
    out_specs = [pl.BlockSpec((tm, tn), lambda i, j: (i, col(i, j)))]
    out_shape = [jax.ShapeDtypeStruct((m, f), BF16)]
    if fuse_cast:
        slab = (slab_rows, w_down.shape[2])
        in_specs.append(pl.BlockSpec((None,) + slab, lambda i, j: (layer, i * nj + j, 0)))
        args.append(w_down)
        out_specs.append(pl.BlockSpec(slab, lambda i, j: (i * nj + j, 0)))
        out_shape.append(jax.ShapeDtypeStruct(w_down.shape[1:], BF16))
    res = _call(body, name="ffn_up", grid=(m // tm, nj), in_specs=in_specs, out_specs=out_specs,
                out_shape=out_shape, args=args,
                temp_bytes=10 * _nbytes((rc, tn), F32) + 4 * _nbytes((rs, tn), F32) + 2 * _nbytes((d, tn), BF16),
                scratch=[((2, tm + 2 * pad, tn), F32)])
    return res[0], (res[1] if fuse_cast else None)


def _residual_proj(name, lhs, rhs, x, mods, k_gate, *, tm, tn):
    m = x.shape[0]
    rows_per_cond = m // mods.shape[0]
    assert rows_per_cond % tm == 0
    extras = [(x, (tm, tn), lambda i, j: (i, j)),
              (mods, (None, N_MOD, tn), lambda i, j: ((i * tm) // rows_per_cond, 0, j))]
    return _proj(name, lhs, rhs, _ep_residual(k_gate), (F32,), tm=tm, tn=tn, extras=extras)[0]


class _AttnWeights(NamedTuple):
    qa: _W
    ka: _W
    va: _W
    lat: jax.Array
    g_cq: jax.Array
    g_ckv: jax.Array
    q_up: _W
    kv: _W
    o_a: _W
    o_b: _W
    lam_vecs: jax.Array


def _prep_attn_weights(j, w_in_b, w_in, g_cq, w_q_up, g_ckv, w_kv_up, w_o, lam_vecs):
    d = w_in.shape[0]
    a_q = A_HEADS * A_HEAD_DIM
    a_v = A_HEADS * A_V_DIM
    b_out = B_HEADS * B_V_DIM
    assert a_v == b_out
    q_lora = g_cq.shape[0]
    kv_lora = g_ckv.shape[0]
    o0, o1, o2 = a_q, 2 * a_q, 2 * a_q + a_v
    lat = jnp.pad(w_in[:, o2:], ((0, 0), (0, MLA_KR_PAD - ROPE_DIM))).astype(BF16)
    q_up = w_q_up.reshape(q_lora, B_HEADS, NOPE_DIM + ROPE_DIM)
    q_up = jnp.pad(q_up, ((0, 0), (0, 0), (0, MLA_Q_PAD - NOPE_DIM - ROPE_DIM)))
    kv = w_kv_up.reshape(kv_lora, B_HEADS, NOPE_DIM + B_V_DIM)
    kv = jnp.concatenate([kv[:, :, :NOPE_DIM].reshape(kv_lora, B_HEADS * NOPE_DIM),
                          kv[:, :, NOPE_DIM:].reshape(kv_lora, B_HEADS * B_V_DIM)], axis=1)
    c = lambda w: _whole(w.astype(BF16))
    win = lambda col0, n: _W(w_in_b, j, d, 0, col0, n)
    return _AttnWeights(
        qa=win(0, a_q), ka=win(o0, a_q), va=win(o1, a_v), lat=lat,
        g_cq=g_cq.reshape(1, q_lora), g_ckv=g_ckv.reshape(1, kv_lora),
        q_up=c(q_up.reshape(q_lora, B_HEADS * MLA_Q_PAD)), kv=c(kv),
        o_a=_W(w_o, j, a_v, 0, 0, w_o.shape[-1]), o_b=_W(w_o, j, b_out, 1, 0, w_o.shape[-1]),
        lam_vecs=lam_vecs)


def _latent_proj(h, w_lat, g_cq, g_ckv, *, tm, rope=None):
    m, d = h.shape
    q_lora, kv_lora = g_cq.shape[1], g_ckv.shape[1]
    n = w_lat.shape[1]
    assert n == q_lora + kv_lora + MLA_KR_PAD and q_lora % V7X_LANES == 0 and kv_lora % V7X_LANES == 0

    def rms(a, g):
        return a * lax.rsqrt(jnp.mean(a * a, axis=-1, keepdims=True) + EPS) * g

    def body(*refs):
        h_ref, w_ref, gq_ref, gkv_ref = refs[:4]
        cq_ref, ckv_ref, kr_ref = refs[-3:]
        acc = _dot(h_ref[...], w_ref[...])
        cq_ref[...] = rms(acc[:, :q_lora], gq_ref[...]).astype(cq_ref.dtype)
        ckv_ref[...] = rms(acc[:, q_lora:q_lora + kv_lora], gkv_ref[...])
        kr = acc[:, q_lora + kv_lora:]
        if rope is not None:
            kr = _rotate(kr, refs[4][...], refs[5][...])
        kr_ref[...] = kr.astype(kr_ref.dtype)

    row = lambda width: pl.BlockSpec((tm, width), lambda i: (i, 0))
    in_specs = [row(d), pl.BlockSpec((d, n), lambda i: (0, 0), pipeline_mode=pl.Buffered(1)),
                pl.BlockSpec((1, q_lora), lambda i: (0, 0)), pl.BlockSpec((1, kv_lora), lambda i: (0, 0))]
    args = [h, w_lat, g_cq, g_ckv]
    if rope is not None:
        cos, sin, tiles_per_seq = rope
        in_specs += [pl.BlockSpec((tm, MLA_KR_PAD), lambda i: (i % tiles_per_seq, 0))] * 2
        args += [cos, sin]
    return _call(body, name="latent_proj", grid=(m // tm,), in_specs=in_specs,
                 out_specs=[row(q_lora), row(kv_lora), row(MLA_KR_PAD)],
                 out_shape=[jax.ShapeDtypeStruct((m, q_lora), BF16), jax.ShapeDtypeStruct((m, kv_lora), F32),
                            jax.ShapeDtypeStruct((m, MLA_KR_PAD), F32 if rope is None else BF16)],
                 args=args, temp_bytes=4 * _nbytes((tm, n), F32), single_buffered=(1,))


def _attn_mixer(h, x, mods, w, grp, lam_init, cache):
    m, d = h.shape
    hps = A_HEADS if grp.seq <= 256 else 8
    qk_scale = A_QK_HALF ** -0.5 * LOG2_E
    mla_scale = (NOPE_DIM + ROPE_DIM) ** -0.5 * LOG2_E
    rows_unit = grp.seq if grp.rope else m

    def tiles(lhs, rhs, heavy_epilogue):
        k, n = rhs.k, rhs.n
        pref = 1024 if k < 1024 else (256 if heavy_epilogue else 512)
        return _tile(min(rows_unit, lhs.shape[0]), pref, V7X_SUBLANES_BF16), min(n, 2048)

    def plain(name, lhs, rhs, dtype, scale=1.0):
        tm_, tn_ = tiles(lhs, rhs, False)
        return _proj(name, [lhs], [rhs], _ep_scale(scale), (dtype,), tm=tm_, tn=tn_)[0]

    def roped(name, lhs, rhs, tables, dtype, scale, with_plain):
        tm_, tn_ = tiles(lhs, rhs, True)
        tiles_per_seq = grp.seq // tm_
        extras = [(t, (tm_, t.shape[1]), lambda i, j: (i % tiles_per_seq, 0)) for t in tables]
        outs = (dtype, dtype) if with_plain else (dtype,)
        return _proj(name, [lhs], [rhs], _ep_rope(scale, with_plain), outs, tm=tm_, tn=tn_, extras=extras)

    lat_tm = _tile(rows_unit, 256 if grp.rope else 512, V7X_SUBLANES_BF16)
    if not grp.rope:
        cqn, ckvn, kr = _latent_proj(h, w.lat, w.g_cq, w.g_ckv, tm=lat_tm)
        kv = plain("kv_up", ckvn, w.kv, BF16)
        qa = plain("q_diff", h, w.qa, BF16, qk_scale)
        ka = plain("k_diff", h, w.ka, F32)
        va = plain("v_diff", h, w.va, F32)
        q = plain("q_mla", cqn, w.q_up, BF16, mla_scale)
        oa = _diff_attention(w.lam_vecs, lam_init, [qa], [ka], [va], grp, hps)
        ob = _mla_attention([q], [kv], [kr], grp, hps)
        state = (ka, va, ckvn, kr[:, :ROPE_DIM])
    else:
        cos, sin = _rope_tables(grp.seq)
        t_diff = _widen_tables(cos, sin, 0, 0, A_HEAD_DIM)
        t_mla = _widen_tables(cos, sin, NOPE_DIM, MLA_Q_PAD - NOPE_DIM - ROPE_DIM, MLA_Q_PAD)
        t_kr = _widen_tables(cos, sin, 0, MLA_KR_PAD - ROPE_DIM, MLA_KR_PAD)
        ka_c, va_c, ckv_c, kr_c = cache
        cqn, ckvn, kr_r = _latent_proj(h, w.lat, w.g_cq, w.g_ckv, tm=lat_tm,
                                       rope=(t_kr[0], t_kr[1], grp.seq // lat_tm))
        kv = plain("kv_up", ckvn, w.kv, BF16)
        qa_u, qa_r = roped("q_diff", h, w.qa, t_diff, BF16, qk_scale, True)
        ka_r, = roped("k_diff", h, w.ka, t_diff, BF16, 1.0, False)
        va = plain("v_diff", h, w.va, BF16)
        q_u, q_r = roped("q_mla", cqn, w.q_up, t_mla, BF16, mla_scale, True)
        kv_c = plain("kv_up_ctx", ckv_c, w.kv, BF16)
        oa = _diff_attention(w.lam_vecs, lam_init, [qa_u, qa_r], [ka_c, ka_r], [va_c, va], grp, hps)
        ob = _mla_attention([q_u, q_r], [kv_c, kv], [kr_c, kr_r], grp, B_HEADS)
        state = None
    x1 = _residual_proj("o_proj", [oa, ob], [w.o_a, w.o_b], x, mods, 2,
                        tm=_tile(rows_unit, 1024, V7X_SUBLANES_BF16), tn=512)
    return x1, state


def _conv_ffn(x, mods, g2, w_up, conv_w, conv_b, w_down, w_down_b, layer, grp):
    h = _norm(x, g2, mods, 3, 4, BF16)
    act, made = _ffn_up(h, w_up, conv_w, conv_b, layer, grp, None if w_down_b is not None else w_down)
    if w_down_b is None:
        w_down_b = made if made is not None else w_down[layer].astype(BF16)
    out = _residual_proj("ffn_down", [act], [_whole(w_down_b)], x, mods, 5,
                         tm=_tile(x.shape[0] // mods.shape[0], 512, V7X_SUBLANES_BF16), tn=512)
    return out, w_down_b


def kernel(x_prompt, x_sample, cache_diff_k, cache_diff_v, cache_mla_ckv, cache_mla_krope, c, c_ctx,
           norm1_g, norm2_g, w_mod, b_mod, w_in, g_cq, w_q_up, g_ckv, w_kv_up,
           lambda_q1, lambda_k1, lambda_q2, lambda_k2, w_o, w_pool, pool_scale,
           w_up, conv_w, conv_b, w_down, g_final):
    batch, seq, d = x_prompt.shape
    dec_batch, dec_seq, _ = x_sample.shape
    depth = w_mod.shape[0]
    past = cache_diff_k.shape[2]
    groups = (_Group(batch, seq, False), _Group(dec_batch, dec_seq, True))

    n_cond = 1 + dec_batch
    cond8 = jnp.zeros((8, d), F32).at[0].set(c_ctx).at[1:n_cond].set(c)
    mods_all = _adaln(cond8, w_mod, b_mod).reshape(depth, 8, N_MOD, d)
    mods_g = (mods_all[:, 0:1], mods_all[:, 1:n_cond])

    w_in_b = w_in.astype(BF16)
    attn_w = {}
    for i in range(0, depth, 2):
        j = i // 2
        lam_vecs = jnp.stack([lambda_q1[j], lambda_k1[j], lambda_q2[j], lambda_k2[j]])
        attn_w[i] = _prep_attn_weights(j, w_in_b, w_in[j], g_cq[j], w_q_up[j], g_ckv[j], w_kv_up[j],
                                       w_o, lam_vecs)
    w_down_b = [None] * depth

    xs = [x_prompt.reshape(batch * seq, d), x_sample.reshape(dec_batch * dec_seq, d)]
    states = []
    for gi, grp in enumerate(groups):
        x = xs[gi]
        for i in range(depth):
            mods = mods_g[gi][i]
            h = _norm(x, norm1_g[i], mods, 0, 1, BF16)
            if i % 2 == 0:
                j = i // 2
                lam_init = 0.8 - 0.6 * math.exp(-0.3 * i)
                cache = None
                if grp.rope:
                    kr_c = jnp.pad(cache_mla_krope[:, j].reshape(dec_batch * past, ROPE_DIM),
                                   ((0, 0), (0, MLA_KR_PAD - ROPE_DIM)))
                    cache = (cache_diff_k[:, j].reshape(dec_batch * past, A_HEADS * A_HEAD_DIM),
                             cache_diff_v[:, j].reshape(dec_batch * past, A_HEADS * A_V_DIM),
                             cache_mla_ckv[:, j].reshape(dec_batch * past, -1), kr_c)
                x, st = _attn_mixer(h, x, mods, attn_w[i], grp, lam_init, cache)
                if st is not None:
                    states.append(st)
            else:
                x = _pool_mixer(h, x, mods, w_pool, i // 2, pool_scale[i // 2], grp, 2)
            x, w_down_b[i] = _conv_ffn(x, mods, norm2_g[i], w_up, conv_w, conv_b, w_down, w_down_b[i], i, grp)
        xs[gi] = _norm(x, g_final, None, 0, 0, F32)

    y_prompt = xs[0].reshape(batch, seq, d)
    y_sample = xs[1].reshape(dec_batch, dec_seq, d)
    stack = lambda k, tail: jnp.stack([s[k].reshape((batch, seq) + tail) for s in states], axis=1)
    new_diff_k = stack(0, (A_HEADS, A_HEAD_DIM))
    new_diff_v = stack(1, (A_HEADS, A_V_DIM))
    new_mla_ckv = stack(2, (g_ckv.shape[-1],))
    new_mla_krope = stack(3, (ROPE_DIM,))
    return (y_prompt, y_sample, new_diff_k, new_diff_v, new_mla_ckv, new_mla_krope)
```
